```python
import jax, jax.numpy as jnp
from jax import lax
import numpy as np

D_MODEL = 1024
BATCH = 16
SEQ = 2048
DEPTH = 4

HEAD_DIM = 64
D_CONV = 512
D_LRU = 512
N_CONV_GROUPS = D_CONV // HEAD_DIM
N_LRU_HEADS = D_LRU // HEAD_DIM
D_MIX = D_CONV + D_LRU
D_IN = 3 * D_CONV + 2 * D_LRU
CONV_A_WIDTH = 3
CONV_B_WIDTH = 4
LRU_C = 8.0
D_FF = 3072
N_EXPERTS = 8
TOP_K = 2
D_FF_EXPERT = 3584
MOE_BLOCK = 128
N_DENSE = (DEPTH + 1) // 2
N_MOE = DEPTH // 2
EPS = 1e-6

kernel_name = "hymba_conv_rglru_moe_trunk"


def rms_norm(x, g):
    x32 = x.astype(jnp.float32)
    y = x32 * lax.rsqrt(jnp.mean(x32 * x32, axis=-1, keepdims=True) + EPS)
    return (y * g.astype(jnp.float32)).astype(x.dtype)


def group_rms_norm(x, g, group):
    b, s, d = x.shape
    xg = x.astype(jnp.float32).reshape(b, s, d // group, group)
    xg = xg * lax.rsqrt(jnp.mean(xg * xg, axis=-1, keepdims=True) + EPS)
    return (xg.reshape(b, s, d) * g.astype(jnp.float32)).astype(x.dtype)


def causal_depthwise_conv(u, w, bias):
    k, c = w.shape
    y = lax.conv_general_dilated(
        u, w[:, None, :].astype(u.dtype), window_strides=(1,), padding=[(k - 1, 0)],
        dimension_numbers=("NWC", "WIO", "NWC"), feature_group_count=c)
    return y + bias.astype(u.dtype)


def rg_lru(x, w_r, b_r, w_i, b_i, lam):
    b, s, d = x.shape
    x32 = x.astype(jnp.float32)
    xh = x32.reshape(b, s, N_LRU_HEADS, HEAD_DIM)
    r = jax.nn.sigmoid(jnp.einsum("bshd,hde->bshe", xh, w_r.astype(jnp.float32)).reshape(b, s, d)
                       + b_r.astype(jnp.float32))
    i = jax.nn.sigmoid(jnp.einsum("bshd,hde->bshe", xh, w_i.astype(jnp.float32)).reshape(b, s, d)
                       + b_i.astype(jnp.float32))
    log_a = -LRU_C * r * jax.nn.softplus(-lam.astype(jnp.float32))
    a = jnp.exp(log_a)
    u = jnp.sqrt(-jnp.expm1(2.0 * log_a)) * (i * x32)

    def combine(left, right):
        a1, b1 = left
        a2, b2 = right
        return a1 * a2, a2 * b1 + b2

    _, h = lax.associative_scan(combine, (a, u), axis=1)
    return h.astype(x.dtype)


def hybrid_mixer(xn, w_in, conv_a_w, conv_a_b, conv_b_w, conv_b_b,
                 w_r, b_r, w_i, b_i, lam, g_mix, w_out):
    z = jnp.einsum("bsd,de->bse", xn, w_in)
    xa, gb, gc, xr, gr = jnp.split(
        z, [D_CONV, 2 * D_CONV, 3 * D_CONV, 3 * D_CONV + D_LRU], axis=-1)
    ya = gb * causal_depthwise_conv(gc * xa, conv_a_w, conv_a_b)
    yr = rg_lru(causal_depthwise_conv(xr, conv_b_w, conv_b_b), w_r, b_r, w_i, b_i, lam) * jax.nn.gelu(gr)
    y = jnp.concatenate([ya, yr], axis=-1)
    y = group_rms_norm(y, g_mix, HEAD_DIM)
    return jnp.einsum("bse,ed->bsd", y, w_out)


def swiglu(x, w1, w3, w2):
    h = jax.nn.silu(jnp.einsum("bsd,df->bsf", x, w1)) * jnp.einsum("bsd,df->bsf", x, w3)
    return jnp.einsum("bsf,fd->bsd", h, w2)


def moe_swiglu(x, w_router, w1, w3, w2):
    b, s, d = x.shape
    t = b * s
    xf = x.reshape(t, d)
    logits = jnp.einsum("td,de->te", xf.astype(jnp.float32), w_router.astype(jnp.float32))
    top_vals, top_idx = lax.top_k(logits, TOP_K)
    gates = jax.nn.softmax(top_vals, axis=-1)
    n_assign = t * TOP_K
    expert_ids = top_idx.reshape(-1).astype(jnp.int32)
    token_ids = jnp.arange(n_assign, dtype=jnp.int32) // TOP_K
    gate_flat = gates.reshape(-1)
    order = jnp.argsort(expert_ids)
    sorted_e = expert_ids[order]
    counts = jnp.bincount(expert_ids, length=N_EXPERTS)
    padded = (counts + MOE_BLOCK - 1) // MOE_BLOCK * MOE_BLOCK
    start = jnp.cumsum(counts) - counts
    pad_end = jnp.cumsum(padded)
    pad_start = pad_end - padded
    rank = jnp.arange(n_assign, dtype=jnp.int32) - start[sorted_e]
    dest = pad_start[sorted_e] + rank
    n_blocks = -(-n_assign // MOE_BLOCK) + N_EXPERTS
    cap = n_blocks * MOE_BLOCK
    buf_tok = jnp.zeros((cap,), jnp.int32).at[dest].set(token_ids[order])
    buf_gate = jnp.zeros((cap,), jnp.float32).at[dest].set(gate_flat[order])
    block_start = jnp.arange(n_blocks, dtype=jnp.int32) * MOE_BLOCK
    block_expert = jnp.minimum(jnp.searchsorted(pad_end, block_start, side="right"), N_EXPERTS - 1)
    xb = xf[buf_tok].reshape(n_blocks, MOE_BLOCK, d)

    def expert_block(args):
        xblk, e = args
        h = jax.nn.silu(xblk @ w1[e]) * (xblk @ w3[e])
        return h @ w2[e]

    yb = lax.map(expert_block, (xb, block_expert)).reshape(cap, d)
    yb = yb * buf_gate[:, None].astype(yb.dtype)
    out = jnp.zeros((t, d), x.dtype).at[buf_tok].add(yb.astype(x.dtype))
    return out.reshape(b, s, d)


def setup_inputs(seed: int = 0) -> dict:
    key = jax.random.key(seed)
    ks = jax.random.split(key, 24)
    f32 = jnp.float32

    def nrm(k, shape, fan_in):
        return jax.random.normal(k, shape, f32) * (fan_in ** -0.5)

    def gain(k, shape):
        return 1.0 + 0.05 * jax.random.normal(k, shape, f32)

    def small(k, shape):
        return 0.02 * jax.random.normal(k, shape, f32)

    u = jax.random.uniform(ks[10], (DEPTH, D_LRU), f32, 0.9, 0.999)
    s_root = u ** (1.0 / LRU_C)
    lam = jnp.log(s_root) - jnp.log1p(-s_root)
    return {
        "x": jax.random.normal(ks[0], (BATCH, SEQ, D_MODEL), f32),
        "norm_mix_g": gain(ks[1], (DEPTH, D_MODEL)),
        "w_in": nrm(ks[2], (DEPTH, D_MODEL, D_IN), D_MODEL),
        "conv_a_w": nrm(ks[3], (DEPTH, CONV_A_WIDTH, D_CONV), CONV_A_WIDTH),
        "conv_a_b": small(ks[4], (DEPTH, D_CONV)),
        "conv_b_w": nrm(ks[5], (DEPTH, CONV_B_WIDTH, D_LRU), CONV_B_WIDTH),
        "conv_b_b": small(ks[6], (DEPTH, D_LRU)),
        "lru_w_r": nrm(ks[7], (DEPTH, N_LRU_HEADS, HEAD_DIM, HEAD_DIM), HEAD_DIM),
        "lru_b_r": small(ks[8], (DEPTH, D_LRU)),
        "lru_w_i": nrm(ks[9], (DEPTH, N_LRU_HEADS, HEAD_DIM, HEAD_DIM), HEAD_DIM),
        "lru_b_i": small(ks[11], (DEPTH, D_LRU)),
        "lru_lambda": lam,
        "mix_out_g": gain(ks[12], (DEPTH, D_MIX)),
        "w_out": nrm(ks[13], (DEPTH, D_MIX, D_MODEL), D_MIX),
        "norm_ffn_g": gain(ks[14], (DEPTH, D_MODEL)),
        "ffn_w1": nrm(ks[15], (N_DENSE, D_MODEL, D_FF), D_MODEL),
        "ffn_w3": nrm(ks[16], (N_DENSE, D_MODEL, D_FF), D_MODEL),
        "ffn_w2": nrm(ks[17], (N_DENSE, D_FF, D_MODEL), D_FF),
        "router_w": nrm(ks[18], (N_MOE, D_MODEL, N_EXPERTS), D_MODEL),
        "expert_w1": nrm(ks[19], (N_MOE, N_EXPERTS, D_MODEL, D_FF_EXPERT), D_MODEL),
        "expert_w3": nrm(ks[20], (N_MOE, N_EXPERTS, D_MODEL, D_FF_EXPERT), D_MODEL),
        "expert_w2": nrm(ks[21], (N_MOE, N_EXPERTS, D_FF_EXPERT, D_MODEL), D_FF_EXPERT),
        "final_g": gain(ks[22], (D_MODEL,)),
    }


def reference(x, norm_mix_g, w_in, conv_a_w, conv_a_b, conv_b_w, conv_b_b,
              lru_w_r, lru_b_r, lru_w_i, lru_b_i, lru_lambda, mix_out_g, w_out,
              norm_ffn_g, ffn_w1, ffn_w3, ffn_w2, router_w, expert_w1, expert_w3,
              expert_w2, final_g):
    h = x
    for layer in range(DEPTH):
        hn = rms_norm(h, norm_mix_g[layer])
        h = h + hybrid_mixer(hn, w_in[layer], conv_a_w[layer], conv_a_b[layer],
                             conv_b_w[layer], conv_b_b[layer], lru_w_r[layer], lru_b_r[layer],
                             lru_w_i[layer], lru_b_i[layer], lru_lambda[layer],
                             mix_out_g[layer], w_out[layer])
        hn = rms_norm(h, norm_ffn_g[layer])
        j = layer // 2
        if layer % 2 == 0:
            h = h + swiglu(hn, ffn_w1[j], ffn_w3[j], ffn_w2[j])
        else:
            h = h + moe_swiglu(hn, router_w[j], expert_w1[j], expert_w3[j], expert_w2[j])
    return rms_norm(h, final_g)
```

```python
import functools

import jax
import jax.numpy as jnp
from jax import lax
from jax.experimental import pallas as pl
from jax.experimental.pallas import tpu as pltpu

EPS = 1e-6
HEAD_DIM = 64
D_CONV = 512
D_LRU = 512
LRU_C = 8.0
N_EXPERTS = 8
TOP_K = 2

SUBLANES = 8
LANES = 128

MIX_TS = 512
FFN_TM = 1024
FFN_TF = 512
MOE_TM = 512
ROUTE_TM = 512
ROW_TM = 256

F32 = jnp.float32
BF16 = jnp.bfloat16


def _rms_norm_rows(x, g):
    return x * lax.rsqrt(jnp.mean(x * x, axis=-1, keepdims=True) + EPS) * g


def _shift_rows(u, tail, k):
    r = pltpu.roll(u, k, 0)
    rt = pltpu.roll(tail, k, 0)
    rows = lax.broadcasted_iota(jnp.int32, rt.shape, 0)
    first = jnp.where(rows < k, rt, r[0:SUBLANES])
    return jnp.concatenate([first, r[SUBLANES:]], axis=0)


def _sigmoid(x):
    return 1.0 / (1.0 + jnp.exp(-x))


def _gelu_tanh(x):
    return 0.5 * x * (1.0 + jnp.tanh(0.7978845608028654 * (x + 0.044715 * (x * x * x))))


def _mixer_kernel(h_ref, gn_ref, win_ref, caw_ref, cab_ref, cbw_ref, cbb_ref, wg_ref,
                  br_ref, bi_ref, lam_ref, gmix_ref, g1_ref, e2_ref, wout_ref,
                  o_ref, taila_ref, tailb_ref, hstate_ref, a_scr, b_scr):
    ts = h_ref.shape[0]

    @pl.when(pl.program_id(1) == 0)
    def _():
        taila_ref[...] = jnp.zeros_like(taila_ref)
        tailb_ref[...] = jnp.zeros_like(tailb_ref)
        hstate_ref[...] = jnp.zeros_like(hstate_ref)

    x = h_ref[...]
    xn = _rms_norm_rows(x, gn_ref[...])
    z = jnp.dot(xn.astype(BF16), win_ref[...], preferred_element_type=F32)
    xa = z[:, 0:D_CONV]
    gb = z[:, D_CONV:2 * D_CONV]
    gc = z[:, 2 * D_CONV:3 * D_CONV]
    xr = z[:, 3 * D_CONV:3 * D_CONV + D_LRU]
    gr = z[:, 3 * D_CONV + D_LRU:]

    ua = gc * xa
    taila = taila_ref[...]
    ca = (caw_ref[2:3, :] * ua + caw_ref[1:2, :] * _shift_rows(ua, taila, 1)
          + caw_ref[0:1, :] * _shift_rows(ua, taila, 2) + cab_ref[...])
    taila_ref[...] = ua[ts - SUBLANES:, :]
    ya = gb * ca

    tailb = tailb_ref[...]
    xc = (cbw_ref[3:4, :] * xr + cbw_ref[2:3, :] * _shift_rows(xr, tailb, 1)
          + cbw_ref[1:2, :] * _shift_rows(xr, tailb, 2)
          + cbw_ref[0:1, :] * _shift_rows(xr, tailb, 3) + cbb_ref[...])
    tailb_ref[...] = xr[ts - SUBLANES:, :]

    gz = jnp.dot(xc.astype(BF16), wg_ref[...], preferred_element_type=F32)
    r = _sigmoid(gz[:, 0:D_LRU] + br_ref[...])
    i = _sigmoid(gz[:, D_LRU:] + bi_ref[...])
    nl = -lam_ref[...]
    softplus = jnp.maximum(nl, 0.0) + jnp.log1p(jnp.exp(-jnp.abs(nl)))
    log_a = (-LRU_C) * r * softplus
    a = jnp.exp(log_a)
    u = jnp.sqrt(1.0 - a * a) * (i * xc)

    rows = lax.broadcasted_iota(jnp.int32, a.shape, 0) & (SUBLANES - 1)
    for d in (1, 2, 4):
        a_sh = pltpu.roll(a, d, 0)
        u_sh = pltpu.roll(u, d, 0)
        m = rows >= d
        u = jnp.where(m, a * u_sh + u, u)
        a = jnp.where(m, a * a_sh, a)
    a_scr[...] = a
    b_scr[...] = u
    carry = hstate_ref[...]
    for g in range(ts // SUBLANES):
        sl = pl.ds(g * SUBLANES, SUBLANES)
        hg = a_scr[sl, :] * carry + b_scr[sl, :]
        b_scr[sl, :] = hg
        carry = jnp.broadcast_to(hg[SUBLANES - 1:SUBLANES, :], hg.shape)
    hstate_ref[...] = carry
    yr = b_scr[...] * _gelu_tanh(gr)

    y = jnp.concatenate([ya, yr], axis=-1)
    gm = jnp.dot((y * y).astype(BF16), g1_ref[...], preferred_element_type=F32)
    rs = lax.rsqrt(gm + EPS)
    rs_hi = rs.astype(BF16)
    rs_lo = (rs - rs_hi.astype(F32)).astype(BF16)
    rs_full = jnp.dot(jnp.concatenate([rs_hi, rs_lo], axis=-1), e2_ref[...],
                      preferred_element_type=F32)
    yn = y * rs_full * gmix_ref[...]
    o_ref[...] = x + jnp.dot(yn.astype(BF16), wout_ref[...], preferred_element_type=F32)


def _mixer(h, p):
    b, s, d = h.shape
    ts = min(MIX_TS, s)
    d_in = p["w_in"].shape[1]
    d_mix = D_CONV + D_LRU
    const = lambda shape: pl.BlockSpec(shape, lambda bi, j: (0,) * len(shape))
    return pl.pallas_call(
        _mixer_kernel,
        grid=(b, s // ts),
        in_specs=[
            pl.BlockSpec((None, ts, d), lambda bi, j: (bi, j, 0)),
            const((1, d)), const((d, d_in)),
            const((3, D_CONV)), const((1, D_CONV)), const((4, D_LRU)), const((1, D_LRU)),
            const((D_LRU, 2 * D_LRU)), const((1, D_LRU)), const((1, D_LRU)), const((1, D_LRU)),
            const((1, d_mix)), const((d_mix, LANES)), const((2 * LANES, d_mix)), const((d_mix, d)),
        ],
        out_specs=pl.BlockSpec((None, ts, d), lambda bi, j: (bi, j, 0)),
        out_shape=jax.ShapeDtypeStruct((b, s, d), F32),
        scratch_shapes=[
            pltpu.VMEM((SUBLANES, D_CONV), F32), pltpu.VMEM((SUBLANES, D_LRU), F32),
            pltpu.VMEM((SUBLANES, D_LRU), F32),
            pltpu.VMEM((ts, D_LRU), F32), pltpu.VMEM((ts, D_LRU), F32),
        ],
        compiler_params=pltpu.CompilerParams(
            dimension_semantics=("arbitrary", "arbitrary"), vmem_limit_bytes=56 * 1024 * 1024),
        name="mixer",
    )(h, p["norm_mix_g"], p["w_in"], p["conv_a_w"], p["conv_a_b"], p["conv_b_w"], p["conv_b_b"],
      p["w_gate"], p["b_r"], p["b_i"], p["lam"], p["g_mix"], p["g1"], p["e2"], p["w_out"])


def _swiglu_step(xn_scr, acc_scr, w1_ref, w3_ref, w2_ref):
    xn = xn_scr[...]
    h1 = jnp.dot(xn, w1_ref[...], preferred_element_type=F32)
    h3 = jnp.dot(xn, w3_ref[...], preferred_element_type=F32)
    g = (h1 * _sigmoid(h1)) * h3
    acc_scr[...] += jnp.dot(g.astype(BF16), w2_ref[...], preferred_element_type=F32)


def _dense_ffn_kernel(x_ref, g_ref, w1_ref, w3_ref, w2_ref, o_ref, xn_scr, acc_scr):
    f = pl.program_id(1)

    @pl.when(f == 0)
    def _():
        xn_scr[...] = _rms_norm_rows(x_ref[...], g_ref[...]).astype(BF16)
        acc_scr[...] = jnp.zeros_like(acc_scr)

    _swiglu_step(xn_scr, acc_scr, w1_ref, w3_ref, w2_ref)

    @pl.when(f == pl.num_programs(1) - 1)
    def _():
        o_ref[...] = x_ref[...] + acc_scr[...]


def _dense_ffn(h2, g, w1, w3, w2):
    t, d = h2.shape
    ff = w1.shape[1]
    tm = min(FFN_TM, t)
    tf = FFN_TF
    return pl.pallas_call(
        _dense_ffn_kernel,
        grid=(t // tm, ff // tf),
        in_specs=[
            pl.BlockSpec((tm, d), lambda i, f: (i, 0)),
            pl.BlockSpec((1, d), lambda i, f: (0, 0)),
            pl.BlockSpec((d, tf), lambda i, f: (0, f)),
            pl.BlockSpec((d, tf), lambda i, f: (0, f)),
            pl.BlockSpec((tf, d), lambda i, f: (f, 0)),
        ],
        out_specs=pl.BlockSpec((tm, d), lambda i, f: (i, 0)),
        out_shape=jax.ShapeDtypeStruct((t, d), F32),
        scratch_shapes=[pltpu.VMEM((tm, d), BF16), pltpu.VMEM((tm, d), F32)],
        compiler_params=pltpu.CompilerParams(
            dimension_semantics=("arbitrary", "arbitrary"), vmem_limit_bytes=56 * 1024 * 1024),
        name="dense_ffn",
    )(h2, g, w1, w3, w2)


def _expert_ffn_kernel(be_ref, nused_ref, x_ref, w1_ref, w3_ref, w2_ref, o_ref, xn_scr, acc_scr):
    i = pl.program_id(0)
    f = pl.program_id(1)
    used = i < nused_ref[0]

    @pl.when(jnp.logical_and(used, f == 0))
    def _():
        xn_scr[...] = x_ref[...].astype(BF16)
        acc_scr[...] = jnp.zeros_like(acc_scr)

    @pl.when(used)
    def _():
        _swiglu_step(xn_scr, acc_scr, w1_ref, w3_ref, w2_ref)

    @pl.when(f == pl.num_programs(1) - 1)
    def _():
        o_ref[...] = jnp.where(used, acc_scr[...], 0.0)


def _expert_ffn(xs, block_expert, n_used, w1, w3, w2):
    cap, d = xs.shape
    ff = w1.shape[2]
    tm = MOE_TM
    tf = FFN_TF
    grid_spec = pltpu.PrefetchScalarGridSpec(
        num_scalar_prefetch=2,
        grid=(cap // tm, ff // tf),
        in_specs=[
            pl.BlockSpec((tm, d), lambda i, f, be, nu: (i, 0)),
            pl.BlockSpec((None, d, tf), lambda i, f, be, nu: (be[i], 0, f)),
            pl.BlockSpec((None, d, tf), lambda i, f, be, nu: (be[i], 0, f)),
            pl.BlockSpec((None, tf, d), lambda i, f, be, nu: (be[i], f, 0)),
        ],
        out_specs=pl.BlockSpec((tm, d), lambda i, f, be, nu: (i, 0)),
        scratch_shapes=[pltpu.VMEM((tm, d), BF16), pltpu.VMEM((tm, d), F32)],
    )
    return pl.pallas_call(
        _expert_ffn_kernel,
        grid_spec=grid_spec,
        out_shape=jax.ShapeDtypeStruct((cap, d), F32),
        compiler_params=pltpu.CompilerParams(
            dimension_semantics=("arbitrary", "arbitrary"), vmem_limit_bytes=56 * 1024 * 1024),
        name="expert_ffn",
    )(block_expert, n_used, xs, w1, w3, w2)


def _router_kernel(x_ref, g_ref, wcat_ref, eid_ref, gate_ref, rank_ref, cnt_ref, carry_ref):
    tm = x_ref.shape[0]

    @pl.when(pl.program_id(0) == 0)
    def _():
        carry_ref[...] = jnp.zeros_like(carry_ref)

    xn = _rms_norm_rows(x_ref[...], g_ref[...])
    x_hi = xn.astype(BF16)
    x_lo = (xn - x_hi.astype(F32)).astype(BF16)
    wcat = wcat_ref[...]
    l_hi = jnp.dot(x_hi, wcat, preferred_element_type=F32)
    l_lo = jnp.dot(x_lo, wcat, preferred_element_type=F32)
    logits = l_hi[:, 0:N_EXPERTS] + l_hi[:, N_EXPERTS:] + l_lo[:, 0:N_EXPERTS]

    lane = lax.broadcasted_iota(jnp.int32, logits.shape, 1)
    m1 = jnp.max(logits, axis=-1, keepdims=True)
    i1 = jnp.min(jnp.where(logits == m1, lane, N_EXPERTS), axis=-1, keepdims=True)
    rest = jnp.where(lane == i1, -jnp.inf, logits)
    m2 = jnp.max(rest, axis=-1, keepdims=True)
    i2 = jnp.min(jnp.where(rest == m2, lane, N_EXPERTS), axis=-1, keepdims=True)
    e2 = jnp.exp(m2 - m1)
    denom = 1.0 + e2
    gate_ref[...] = jnp.concatenate([1.0 / denom, e2 / denom], axis=-1)
    eid_ref[...] = jnp.concatenate([i1, i2], axis=-1)

    sel1 = lane == i1
    sel2 = lane == i2
    onehot = jnp.where(jnp.logical_or(sel1, sel2), 1.0, 0.0)
    r_io = lax.broadcasted_iota(jnp.int32, (tm, tm), 0)
    c_io = lax.broadcasted_iota(jnp.int32, (tm, tm), 1)
    tri = jnp.where(c_io < r_io, 1.0, 0.0).astype(BF16)
    before = jnp.dot(tri, onehot.astype(BF16), preferred_element_type=F32) + carry_ref[...]
    rank1 = jnp.sum(jnp.where(sel1, before, 0.0), axis=-1, keepdims=True)
    rank2 = jnp.sum(jnp.where(sel2, before, 0.0), axis=-1, keepdims=True)
    rank_ref[...] = jnp.concatenate([rank1, rank2], axis=-1).astype(jnp.int32)
    total = carry_ref[...] + jnp.sum(onehot, axis=0, keepdims=True)
    carry_ref[...] = total
    cnt_ref[...] = total.astype(jnp.int32)


def _router(h2, g, wcat):
    t, d = h2.shape
    tm = min(ROUTE_TM, t)
    pair = lambda dt: jax.ShapeDtypeStruct((t, TOP_K), dt)
    return pl.pallas_call(
        _router_kernel,
        grid=(t // tm,),
        in_specs=[
            pl.BlockSpec((tm, d), lambda i: (i, 0)),
            pl.BlockSpec((1, d), lambda i: (0, 0)),
            pl.BlockSpec((d, 2 * N_EXPERTS), lambda i: (0, 0)),
        ],
        out_specs=[
            pl.BlockSpec((tm, TOP_K), lambda i: (i, 0)),
            pl.BlockSpec((tm, TOP_K), lambda i: (i, 0)),
            pl.BlockSpec((tm, TOP_K), lambda i: (i, 0)),
            pl.BlockSpec((1, N_EXPERTS), lambda i: (0, 0)),
        ],
        out_shape=[pair(jnp.int32), pair(F32), pair(jnp.int32),
                   jax.ShapeDtypeStruct((1, N_EXPERTS), jnp.int32)],
        scratch_shapes=[pltpu.VMEM((1, N_EXPERTS), F32)],
        compiler_params=pltpu.CompilerParams(dimension_semantics=("arbitrary",)),
        name="router",
    )(h2, g, wcat)


def _dispatch_kernel(dest_ref, x_ref, g_ref, zeros_ref, xs_ref, xn_scr, sem):
    del zeros_ref
    tm = x_ref.shape[0]
    xn_scr[...] = _rms_norm_rows(x_ref[...], g_ref[...])

    def row_copy(t, k):
        d = dest_ref[0, 0, TOP_K * t + k]
        return pltpu.make_async_copy(xn_scr.at[pl.ds(t, 1)], xs_ref.at[pl.ds(d, 1)], sem)

    def start(t, c):
        for k in range(TOP_K):
            row_copy(t, k).start()
        return c

    def wait(t, c):
        for k in range(TOP_K):
            row_copy(t, k).wait()
        return c

    lax.fori_loop(0, tm, start, 0)
    lax.fori_loop(0, tm, wait, 0)


def _dispatch(h2, g, dest, cap):
    t, d = h2.shape
    tm = min(ROW_TM, t)
    dest3 = dest.reshape(t // tm, 1, tm * TOP_K)
    zeros = jnp.zeros((cap, d), F32)
    return pl.pallas_call(
        _dispatch_kernel,
        grid=(t // tm,),
        in_specs=[
            pl.BlockSpec((1, 1, tm * TOP_K), lambda i: (i, 0, 0), memory_space=pltpu.SMEM),
            pl.BlockSpec((tm, d), lambda i: (i, 0)),
            pl.BlockSpec((1, d), lambda i: (0, 0)),
            pl.BlockSpec(memory_space=pl.ANY),
        ],
        out_specs=pl.BlockSpec(memory_space=pl.ANY),
        out_shape=jax.ShapeDtypeStruct((cap, d), F32),
        scratch_shapes=[pltpu.VMEM((tm, d), F32), pltpu.SemaphoreType.DMA],
        input_output_aliases={3: 0},
        compiler_params=pltpu.CompilerParams(dimension_semantics=("arbitrary",)),
        name="dispatch",
    )(dest3, h2, g, zeros)


def _combine_kernel(pos_ref, h_ref, gate_ref, fg_ref, ys_ref, o_ref, buf, sem, *, final_norm):
    tm = h_ref.shape[0]

    def row_copy(t, k):
        p = pos_ref[0, 0, TOP_K * t + k]
        return pltpu.make_async_copy(ys_ref.at[pl.ds(p, 1)], buf.at[k, pl.ds(t, 1)], sem)

    def start(t, c):
        for k in range(TOP_K):
            row_copy(t, k).start()
        return c

    def wait(t, c):
        for k in range(TOP_K):
            row_copy(t, k).wait()
        return c

    lax.fori_loop(0, tm, start, 0)
    lax.fori_loop(0, tm, wait, 0)
    gate = gate_ref[...]
    out = h_ref[...] + gate[:, 0:1] * buf[0] + gate[:, 1:2] * buf[1]
    if final_norm:
        out = _rms_norm_rows(out, fg_ref[...])
    o_ref[...] = out


def _combine(h2, gates, pos, ys, final_g, final_norm):
    t, d = h2.shape
    tm = min(ROW_TM, t)
    pos3 = pos.reshape(t // tm, 1, tm * TOP_K)
    return pl.pallas_call(
        functools.partial(_combine_kernel, final_norm=final_norm),
        grid=(t // tm,),
        in_specs=[
            pl.BlockSpec((1, 1, tm * TOP_K), lambda i: (i, 0, 0), memory_space=pltpu.SMEM),
            pl.BlockSpec((tm, d), lambda i: (i, 0)),
            pl.BlockSpec((tm, TOP_K), lambda i: (i, 0)),
            pl.BlockSpec((1, d), lambda i: (0, 0)),
            pl.BlockSpec(memory_space=pl.ANY),
        ],
        out_specs=pl.BlockSpec((tm, d), lambda i: (i, 0)),
        out_shape=jax.ShapeDtypeStruct((t, d), F32),
        scratch_shapes=[pltpu.VMEM((TOP_K, tm, d), F32), pltpu.SemaphoreType.DMA],
        compiler_params=pltpu.CompilerParams(dimension_semantics=("arbitrary",)),
        name="combine",
    )(pos3, h2, gates, final_g, ys)


def _moe(h2, g, wcat, w1, w3, w2, final_g, final_norm):
    t, d = h2.shape
    eid, gates, rank, counts = _router(h2, g, wcat)
    n_assign = t * TOP_K
    n_blocks = -(-n_assign // MOE_TM) + N_EXPERTS
    cap = n_blocks * MOE_TM
    counts = counts[0]
    padded = (counts + MOE_TM - 1) // MOE_TM * MOE_TM
    pad_end = jnp.cumsum(padded)
    pad_start = pad_end - padded
    dest = pad_start[eid] + rank
    block_start = jnp.arange(n_blocks, dtype=jnp.int32) * MOE_TM
    block_expert = jnp.minimum(
        jnp.searchsorted(pad_end, block_start, side="right"), N_EXPERTS - 1).astype(jnp.int32)
    n_used = (pad_end[-1:] // MOE_TM).astype(jnp.int32)
    xs = _dispatch(h2, g, dest, cap)
    ys = _expert_ffn(xs, block_expert, n_used, w1, w3, w2)
    return _combine(h2, gates, dest, ys, final_g, final_norm)


def _block_diag(w):
    nh, hd, _ = w.shape
    return jnp.einsum("hde,hg->hdge", w, jnp.eye(nh, dtype=w.dtype)).reshape(nh * hd, nh * hd)


def _group_matrices(d_mix):
    grp = jnp.arange(d_mix) // HEAD_DIM
    cols = jnp.arange(LANES)
    g1 = jnp.where(grp[:, None] == cols[None, :], 1.0 / HEAD_DIM, 0.0).astype(BF16)
    e = jnp.where(cols[:, None] == grp[None, :], 1.0, 0.0).astype(BF16)
    return g1, jnp.concatenate([e, e], axis=0)


def kernel(x, norm_mix_g, w_in, conv_a_w, conv_a_b, conv_b_w, conv_b_b, lru_w_r, lru_b_r,
           lru_w_i, lru_b_i, lru_lambda, mix_out_g, w_out, norm_ffn_g, ffn_w1, ffn_w3, ffn_w2,
           router_w, expert_w1, expert_w3, expert_w2, final_g):
    b, s, d = x.shape
    depth = w_in.shape[0]
    g1, e2 = _group_matrices(D_CONV + D_LRU)
    row = lambda v: v.reshape(1, -1)
    h = x
    for layer in range(depth):
        p = {
            "norm_mix_g": row(norm_mix_g[layer]),
            "w_in": w_in[layer].astype(BF16),
            "conv_a_w": conv_a_w[layer], "conv_a_b": row(conv_a_b[layer]),
            "conv_b_w": conv_b_w[layer], "conv_b_b": row(conv_b_b[layer]),
            "w_gate": jnp.concatenate(
                [_block_diag(lru_w_r[layer]), _block_diag(lru_w_i[layer])], axis=1).astype(BF16),
            "b_r": row(lru_b_r[layer]), "b_i": row(lru_b_i[layer]), "lam": row(lru_lambda[layer]),
            "g_mix": row(mix_out_g[layer]), "g1": g1, "e2": e2,
            "w_out": w_out[layer].astype(BF16),
        }
        h = _mixer(h, p)
        h2 = h.reshape(b * s, d)
        gf = row(norm_ffn_g[layer])
        j = layer // 2
        if layer % 2 == 0:
            h2 = _dense_ffn(h2, gf, ffn_w1[j].astype(BF16), ffn_w3[j].astype(BF16),
                            ffn_w2[j].astype(BF16))
        else:
            w_hi = router_w[j].astype(BF16)
            w_lo = (router_w[j] - w_hi.astype(F32)).astype(BF16)
            h2 = _moe(h2, gf, jnp.concatenate([w_hi, w_lo], axis=1),
                      expert_w1[j].astype(BF16), expert_w3[j].astype(BF16),
                      expert_w2[j].astype(BF16), row(final_g), layer == depth - 1)
        h = h2.reshape(b, s, d)
    if depth % 2 == 1:
        raise NotImplementedError("final norm is fused into the last routed layer")
    return h
```

```python
import functools

import jax
import jax.numpy as jnp
from jax import lax
from jax.experimental import pallas as pl
from jax.experimental.pallas import tpu as pltpu

EPS = 1e-6
HEAD_DIM = 64
D_CONV = 512
D_LRU = 512
LRU_C = 8.0
N_EXPERTS = 8
TOP_K = 2

SUBLANES = 8
LANES = 128

MIX_TS = 512
FFN_TM = 1024
FFN_TF = 512
MOE_TM = 512
ROUTE_TM = 512
ROW_TM = 512
ISSUE_UNROLL = 8
VMEM_LIMIT = 56 * 1024 * 1024

F32 = jnp.float32
BF16 = jnp.bfloat16


def _rms_norm_rows(x, g):
    return x * lax.rsqrt(jnp.mean(x * x, axis=-1, keepdims=True) + EPS) * g


def _shift_rows(u, tail, k):
    r = pltpu.roll(u, k, 0)
    rt = pltpu.roll(tail, k, 0)
    rows = lax.broadcasted_iota(jnp.int32, rt.shape, 0)
    first = jnp.where(rows < k, rt, r[0:SUBLANES])
    return jnp.concatenate([first, r[SUBLANES:]], axis=0)


def _sigmoid(x):
    return 1.0 / (1.0 + jnp.exp(-x))


def _gelu_tanh(x):
    return 0.5 * x * (1.0 + jnp.tanh(0.7978845608028654 * (x + 0.044715 * (x * x * x))))


def _mixer_kernel(h_ref, gn_ref, win_ref, caw_ref, cab_ref, cbw_ref, cbb_ref, wg_ref,
                  br_ref, bi_ref, lam_ref, gmix_ref, g1_ref, e2_ref, wout_ref,
                  o_ref, taila_ref, tailb_ref, hstate_ref, a_scr, b_scr):
    ts = h_ref.shape[0]

    @pl.when(pl.program_id(1) == 0)
    def _():
        taila_ref[...] = jnp.zeros_like(taila_ref)
        tailb_ref[...] = jnp.zeros_like(tailb_ref)
        hstate_ref[...] = jnp.zeros_like(hstate_ref)

    x = h_ref[...]
    xn = _rms_norm_rows(x, gn_ref[...])
    z = jnp.dot(xn.astype(BF16), win_ref[...], preferred_element_type=F32)
    xa = z[:, 0:D_CONV]
    gb = z[:, D_CONV:2 * D_CONV]
    gc = z[:, 2 * D_CONV:3 * D_CONV]
    xr = z[:, 3 * D_CONV:3 * D_CONV + D_LRU]
    gr = z[:, 3 * D_CONV + D_LRU:]

    ua = gc * xa
    taila = taila_ref[...]
    ca = (caw_ref[2:3, :] * ua + caw_ref[1:2, :] * _shift_rows(ua, taila, 1)
          + caw_ref[0:1, :] * _shift_rows(ua, taila, 2) + cab_ref[...])
    taila_ref[...] = ua[ts - SUBLANES:, :]
    ya = gb * ca

    tailb = tailb_ref[...]
    xc = (cbw_ref[3:4, :] * xr + cbw_ref[2:3, :] * _shift_rows(xr, tailb, 1)
          + cbw_ref[1:2, :] * _shift_rows(xr, tailb, 2)
          + cbw_ref[0:1, :] * _shift_rows(xr, tailb, 3) + cbb_ref[...])
    tailb_ref[...] = xr[ts - SUBLANES:, :]

    gz = jnp.dot(xc.astype(BF16), wg_ref[...], preferred_element_type=F32)
    r = _sigmoid(gz[:, 0:D_LRU] + br_ref[...])
    i = _sigmoid(gz[:, D_LRU:] + bi_ref[...])
    nl = -lam_ref[...]
    softplus = jnp.maximum(nl, 0.0) + jnp.log1p(jnp.exp(-jnp.abs(nl)))
    log_a = (-LRU_C) * r * softplus
    a = jnp.exp(log_a)
    u = jnp.sqrt(1.0 - a * a) * (i * xc)

    rows = lax.broadcasted_iota(jnp.int32, a.shape, 0) & (SUBLANES - 1)
    for d in (1, 2, 4):
        a_sh = pltpu.roll(a, d, 0)
        u_sh = pltpu.roll(u, d, 0)
        m = rows >= d
        u = jnp.where(m, a * u_sh + u, u)
        a = jnp.where(m, a * a_sh, a)
    a_scr[...] = a
    b_scr[...] = u
    carry = hstate_ref[...]
    for g in range(ts // SUBLANES):
        sl = pl.ds(g * SUBLANES, SUBLANES)
        hg = a_scr[sl, :] * carry + b_scr[sl, :]
        b_scr[sl, :] = hg
        carry = jnp.broadcast_to(hg[SUBLANES - 1:SUBLANES, :], hg.shape)
    hstate_ref[...] = carry
    yr = b_scr[...] * _gelu_tanh(gr)

    y = jnp.concatenate([ya, yr], axis=-1)
    gm = jnp.dot((y * y).astype(BF16), g1_ref[...], preferred_element_type=F32)
    rs = lax.rsqrt(gm + EPS)
    rs_hi = rs.astype(BF16)
    rs_lo = (rs - rs_hi.astype(F32)).astype(BF16)
    rs_full = jnp.dot(jnp.concatenate([rs_hi, rs_lo], axis=-1), e2_ref[...],
                      preferred_element_type=F32)
    yn = y * rs_full * gmix_ref[...]
    o_ref[...] = x + jnp.dot(yn.astype(BF16), wout_ref[...], preferred_element_type=F32)


def _mixer(h, p):
    b, s, d = h.shape
    ts = min(MIX_TS, s)
    d_in = p["w_in"].shape[1]
    d_mix = D_CONV + D_LRU
    const = lambda shape: pl.BlockSpec(shape, lambda bi, j: (0,) * len(shape))
    return pl.pallas_call(
        _mixer_kernel,
        grid=(b, s // ts),
        in_specs=[
            pl.BlockSpec((None, ts, d), lambda bi, j: (bi, j, 0)),
            const((1, d)), const((d, d_in)),
            const((3, D_CONV)), const((1, D_CONV)), const((4, D_LRU)), const((1, D_LRU)),
            const((D_LRU, 2 * D_LRU)), const((1, D_LRU)), const((1, D_LRU)), const((1, D_LRU)),
            const((1, d_mix)), const((d_mix, LANES)), const((2 * LANES, d_mix)), const((d_mix, d)),
        ],
        out_specs=pl.BlockSpec((None, ts, d), lambda bi, j: (bi, j, 0)),
        out_shape=jax.ShapeDtypeStruct((b, s, d), F32),
        scratch_shapes=[
            pltpu.VMEM((SUBLANES, D_CONV), F32), pltpu.VMEM((SUBLANES, D_LRU), F32),
            pltpu.VMEM((SUBLANES, D_LRU), F32),
            pltpu.VMEM((ts, D_LRU), F32), pltpu.VMEM((ts, D_LRU), F32),
        ],
        compiler_params=pltpu.CompilerParams(
            dimension_semantics=("arbitrary", "arbitrary"), vmem_limit_bytes=VMEM_LIMIT),
        name="mixer",
    )(h, p["norm_mix_g"], p["w_in"], p["conv_a_w"], p["conv_a_b"], p["conv_b_w"], p["conv_b_b"],
      p["w_gate"], p["b_r"], p["b_i"], p["lam"], p["g_mix"], p["g1"], p["e2"], p["w_out"])


def _swiglu_step(xn_scr, acc_scr, w1_ref, w3_ref, w2_ref):
    xn = xn_scr[...]
    h1 = jnp.dot(xn, w1_ref[...].astype(BF16), preferred_element_type=F32)
    h3 = jnp.dot(xn, w3_ref[...].astype(BF16), preferred_element_type=F32)
    g = (h1 * _sigmoid(h1)) * h3
    acc_scr[...] += jnp.dot(g.astype(BF16), w2_ref[...].astype(BF16),
                            preferred_element_type=F32)


def _dense_ffn_kernel(x_ref, g_ref, w1_ref, w3_ref, w2_ref, o_ref, xn_scr, acc_scr):
    f = pl.program_id(1)

    @pl.when(f == 0)
    def _():
        xn_scr[...] = _rms_norm_rows(x_ref[...], g_ref[...]).astype(BF16)
        acc_scr[...] = jnp.zeros_like(acc_scr)

    _swiglu_step(xn_scr, acc_scr, w1_ref, w3_ref, w2_ref)

    @pl.when(f == pl.num_programs(1) - 1)
    def _():
        o_ref[...] = x_ref[...] + acc_scr[...]


def _dense_ffn(h2, g, w1, w3, w2, j):
    t, d = h2.shape
    ff = w1.shape[2]
    tm = min(FFN_TM, t)
    tf = FFN_TF
    return pl.pallas_call(
        _dense_ffn_kernel,
        grid=(t // tm, ff // tf),
        in_specs=[
            pl.BlockSpec((tm, d), lambda i, f: (i, 0)),
            pl.BlockSpec((1, d), lambda i, f: (0, 0)),
            pl.BlockSpec((None, d, tf), lambda i, f: (j, 0, f)),
            pl.BlockSpec((None, d, tf), lambda i, f: (j, 0, f)),
            pl.BlockSpec((None, tf, d), lambda i, f: (j, f, 0)),
        ],
        out_specs=pl.BlockSpec((tm, d), lambda i, f: (i, 0)),
        out_shape=jax.ShapeDtypeStruct((t, d), F32),
        scratch_shapes=[pltpu.VMEM((tm, d), BF16), pltpu.VMEM((tm, d), F32)],
        compiler_params=pltpu.CompilerParams(
            dimension_semantics=("arbitrary", "arbitrary"), vmem_limit_bytes=VMEM_LIMIT),
        name="dense_ffn",
    )(h2, g, w1, w3, w2)


def _expert_ffn_kernel(be_ref, nused_ref, x_ref, w1_ref, w3_ref, w2_ref, o_ref, xn_scr, acc_scr):
    i = pl.program_id(0)
    f = pl.program_id(1)
    used = i < nused_ref[0]

    @pl.when(jnp.logical_and(used, f == 0))
    def _():
        xn_scr[...] = x_ref[...].astype(BF16)
        acc_scr[...] = jnp.zeros_like(acc_scr)

    @pl.when(used)
    def _():
        _swiglu_step(xn_scr, acc_scr, w1_ref, w3_ref, w2_ref)

    @pl.when(f == pl.num_programs(1) - 1)
    def _():
        o_ref[...] = jnp.where(used, acc_scr[...], 0.0)


def _expert_ffn(xs, block_expert, n_used, w1, w3, w2, j):
    cap, d = xs.shape
    ff = w1.shape[3]
    tm = MOE_TM
    tf = FFN_TF
    grid_spec = pltpu.PrefetchScalarGridSpec(
        num_scalar_prefetch=2,
        grid=(cap // tm, ff // tf),
        in_specs=[
            pl.BlockSpec((tm, d), lambda i, f, be, nu: (i, 0)),
            pl.BlockSpec((None, None, d, tf), lambda i, f, be, nu: (j, be[i], 0, f)),
            pl.BlockSpec((None, None, d, tf), lambda i, f, be, nu: (j, be[i], 0, f)),
            pl.BlockSpec((None, None, tf, d), lambda i, f, be, nu: (j, be[i], f, 0)),
        ],
        out_specs=pl.BlockSpec((tm, d), lambda i, f, be, nu: (i, 0)),
        scratch_shapes=[pltpu.VMEM((tm, d), BF16), pltpu.VMEM((tm, d), F32)],
    )
    return pl.pallas_call(
        _expert_ffn_kernel,
        grid_spec=grid_spec,
        out_shape=jax.ShapeDtypeStruct((cap, d), F32),
        compiler_params=pltpu.CompilerParams(
            dimension_semantics=("arbitrary", "arbitrary"), vmem_limit_bytes=VMEM_LIMIT),
        name="expert_ffn",
    )(block_expert, n_used, xs, w1, w3, w2)


def _router_kernel(x_ref, g_ref, wcat_ref, eid_ref, gate_ref, rank_ref, cnt_ref, carry_ref):
    tm = x_ref.shape[0]

    @pl.when(pl.program_id(0) == 0)
    def _():
        carry_ref[...] = jnp.zeros_like(carry_ref)

    xn = _rms_norm_rows(x_ref[...], g_ref[...])
    x_hi = xn.astype(BF16)
    x_lo = (xn - x_hi.astype(F32)).astype(BF16)
    wcat = wcat_ref[...]
    l_hi = jnp.dot(x_hi, wcat, preferred_element_type=F32)
    l_lo = jnp.dot(x_lo, wcat, preferred_element_type=F32)
    logits = l_hi[:, 0:N_EXPERTS] + l_hi[:, N_EXPERTS:] + l_lo[:, 0:N_EXPERTS]

    lane = lax.broadcasted_iota(jnp.int32, logits.shape, 1)
    m1 = jnp.max(logits, axis=-1, keepdims=True)
    i1 = jnp.min(jnp.where(logits == m1, lane, N_EXPERTS), axis=-1, keepdims=True)
    rest = jnp.where(lane == i1, -jnp.inf, logits)
    m2 = jnp.max(rest, axis=-1, keepdims=True)
    i2 = jnp.min(jnp.where(rest == m2, lane, N_EXPERTS), axis=-1, keepdims=True)
    e2 = jnp.exp(m2 - m1)
    denom = 1.0 + e2
    gate_ref[...] = jnp.concatenate([1.0 / denom, e2 / denom], axis=-1)
    eid_ref[...] = jnp.concatenate([i1, i2], axis=-1)

    sel1 = lane == i1
    sel2 = lane == i2
    onehot = jnp.where(jnp.logical_or(sel1, sel2), 1.0, 0.0)
    r_io = lax.broadcasted_iota(jnp.int32, (tm, tm), 0)
    c_io = lax.broadcasted_iota(jnp.int32, (tm, tm), 1)
    tri = jnp.where(c_io < r_io, 1.0, 0.0).astype(BF16)
    before = jnp.dot(tri, onehot.astype(BF16), preferred_element_type=F32) + carry_ref[...]
    rank1 = jnp.sum(jnp.where(sel1, before, 0.0), axis=-1, keepdims=True)
    rank2 = jnp.sum(jnp.where(sel2, before, 0.0), axis=-1, keepdims=True)
    rank_ref[...] = jnp.concatenate([rank1, rank2], axis=-1).astype(jnp.int32)
    total = carry_ref[...] + jnp.sum(onehot, axis=0, keepdims=True)
    carry_ref[...] = total
    cnt_ref[...] = total.astype(jnp.int32)


def _router(h2, g, wcat):
    t, d = h2.shape
    tm = min(ROUTE_TM, t)
    pair = lambda dt: jax.ShapeDtypeStruct((t, TOP_K), dt)
    return pl.pallas_call(
        _router_kernel,
        grid=(t // tm,),
        in_specs=[
            pl.BlockSpec((tm, d), lambda i: (i, 0)),
            pl.BlockSpec((1, d), lambda i: (0, 0)),
            pl.BlockSpec((d, 2 * N_EXPERTS), lambda i: (0, 0)),
        ],
        out_specs=[
            pl.BlockSpec((tm, TOP_K), lambda i: (i, 0)),
            pl.BlockSpec((tm, TOP_K), lambda i: (i, 0)),
            pl.BlockSpec((tm, TOP_K), lambda i: (i, 0)),
            pl.BlockSpec((1, N_EXPERTS), lambda i: (0, 0)),
        ],
        out_shape=[pair(jnp.int32), pair(F32), pair(jnp.int32),
                   jax.ShapeDtypeStruct((1, N_EXPERTS), jnp.int32)],
        scratch_shapes=[pltpu.VMEM((1, N_EXPERTS), F32)],
        compiler_params=pltpu.CompilerParams(dimension_semantics=("arbitrary",)),
        name="router",
    )(h2, g, wcat)


def _start_row_copies(n_rows, row_copy):
    def body(c, carry):
        for u in range(ISSUE_UNROLL):
            for k in range(TOP_K):
                row_copy(c * ISSUE_UNROLL + u, k).start(priority=k)
        return carry

    lax.fori_loop(0, n_rows // ISSUE_UNROLL, body, 0)


def _dispatch_kernel(zrow_ref, dest_ref, x_ref, g_ref, xs_ref, xn_scr, zero_scr, sem, zsem):
    i = pl.program_id(0)
    tm = x_ref.shape[0]
    slot = i % 2

    @pl.when(i == 0)
    def _():
        zero_scr[...] = jnp.zeros_like(zero_scr)
        zcopies = [
            pltpu.make_async_copy(
                zero_scr,
                xs_ref.at[pl.ds(pl.multiple_of(jnp.maximum(zrow_ref[n], 0), MOE_TM), MOE_TM)],
                zsem)
            for n in range(2 * N_EXPERTS)]
        for n, c in enumerate(zcopies):
            pl.when(zrow_ref[n] >= 0)(c.start)
        for n, c in enumerate(zcopies):
            pl.when(zrow_ref[n] >= 0)(c.wait)

    xn_scr[slot] = _rms_norm_rows(x_ref[...], g_ref[...])

    def row_copy(t, k):
        d = dest_ref[0, 0, TOP_K * t + k]
        return pltpu.make_async_copy(
            xn_scr.at[slot, pl.ds(t, 1)], xs_ref.at[pl.ds(d, 1)], sem.at[slot])

    _start_row_copies(tm, row_copy)

    def wait_slot(s):
        for _ in range(TOP_K):
            pltpu.make_async_copy(xn_scr.at[s], xs_ref.at[pl.ds(0, tm)], sem.at[s]).wait()

    @pl.when(i > 0)
    def _():
        wait_slot(1 - slot)

    @pl.when(i == pl.num_programs(0) - 1)
    def _():
        wait_slot(slot)


def _dispatch(h2, g, dest, zrows, cap):
    t, d = h2.shape
    tm = min(ROW_TM, t)
    dest3 = dest.reshape(t // tm, 1, tm * TOP_K)
    grid_spec = pltpu.PrefetchScalarGridSpec(
        num_scalar_prefetch=1,
        grid=(t // tm,),
        in_specs=[
            pl.BlockSpec((1, 1, tm * TOP_K), lambda i, z: (i, 0, 0), memory_space=pltpu.SMEM),
            pl.BlockSpec((tm, d), lambda i, z: (i, 0)),
            pl.BlockSpec((1, d), lambda i, z: (0, 0)),
        ],
        out_specs=pl.BlockSpec(memory_space=pl.ANY),
        scratch_shapes=[pltpu.VMEM((2, tm, d), F32), pltpu.VMEM((MOE_TM, d), F32),
                        pltpu.SemaphoreType.DMA((2,)), pltpu.SemaphoreType.DMA],
    )
    return pl.pallas_call(
        _dispatch_kernel,
        grid_spec=grid_spec,
        out_shape=jax.ShapeDtypeStruct((cap, d), F32),
        compiler_params=pltpu.CompilerParams(dimension_semantics=("arbitrary",)),
        name="dispatch",
    )(zrows, dest3, h2, g)


def _combine_kernel(pos_ref, posn_ref, h_ref, gate_ref, fg_ref, ys_ref, o_ref, buf, sem,
                    *, final_norm):
    i = pl.program_id(0)
    n = pl.num_programs(0)
    tm = h_ref.shape[0]
    slot = i % 2

    def start_gather(p_ref, s):
        def row_copy(t, k):
            p = p_ref[0, 0, TOP_K * t + k]
            return pltpu.make_async_copy(
                ys_ref.at[pl.ds(p, 1)], buf.at[s, k, pl.ds(t, 1)], sem.at[s])
        _start_row_copies(tm, row_copy)

    @pl.when(i == 0)
    def _():
        start_gather(pos_ref, slot)

    @pl.when(i + 1 < n)
    def _():
        start_gather(posn_ref, 1 - slot)

    for k in range(TOP_K):
        pltpu.make_async_copy(ys_ref.at[pl.ds(0, tm)], buf.at[slot, k], sem.at[slot]).wait()
    gate = gate_ref[...]
    out = h_ref[...] + gate[:, 0:1] * buf[slot, 0] + gate[:, 1:2] * buf[slot, 1]
    if final_norm:
        out = _rms_norm_rows(out, fg_ref[...])
    o_ref[...] = out


def _combine(h2, gates, pos, ys, final_g, final_norm):
    t, d = h2.shape
    tm = min(ROW_TM, t)
    nt = t // tm
    pos3 = pos.reshape(nt, 1, tm * TOP_K)
    return pl.pallas_call(
        functools.partial(_combine_kernel, final_norm=final_norm),
        grid=(nt,),
        in_specs=[
            pl.BlockSpec((1, 1, tm * TOP_K), lambda i: (i, 0, 0), memory_space=pltpu.SMEM),
            pl.BlockSpec((1, 1, tm * TOP_K), lambda i: (jnp.minimum(i + 1, nt - 1), 0, 0),
                         memory_space=pltpu.SMEM),
            pl.BlockSpec((tm, d), lambda i: (i, 0)),
            pl.BlockSpec((tm, TOP_K), lambda i: (i, 0)),
            pl.BlockSpec((1, d), lambda i: (0, 0)),
            pl.BlockSpec(memory_space=pl.ANY),
        ],
        out_specs=pl.BlockSpec((tm, d), lambda i: (i, 0)),
        out_shape=jax.ShapeDtypeStruct((t, d), F32),
        scratch_shapes=[pltpu.VMEM((2, TOP_K, tm, d), F32), pltpu.SemaphoreType.DMA((2,))],
        compiler_params=pltpu.CompilerParams(
            dimension_semantics=("arbitrary",), vmem_limit_bytes=VMEM_LIMIT),
        name="combine",
    )(pos3, pos3, h2, gates, final_g, ys)


def _moe(h2, g, wcat, w1, w3, w2, j, final_g, final_norm):
    t, d = h2.shape
    eid, gates, rank, counts = _router(h2, g, wcat)
    n_assign = t * TOP_K
    assert n_assign % MOE_TM == 0
    n_blocks = n_assign // MOE_TM + N_EXPERTS
    cap = n_blocks * MOE_TM
    counts = counts[0]
    padded = (counts + MOE_TM - 1) // MOE_TM * MOE_TM
    pad_end = jnp.cumsum(padded)
    pad_start = pad_end - padded
    dest = pad_start[eid] + rank
    block_start = jnp.arange(n_blocks, dtype=jnp.int32) * MOE_TM
    block_expert = jnp.minimum(
        jnp.searchsorted(pad_end, block_start, side="right"), N_EXPERTS - 1).astype(jnp.int32)
    n_used = (pad_end[-1:] // MOE_TM).astype(jnp.int32)
    tail_rows = pad_end[-1] + jnp.arange(N_EXPERTS, dtype=jnp.int32) * MOE_TM
    zrows = jnp.concatenate([
        jnp.where(padded > 0, pad_end - MOE_TM, -1),
        jnp.where(tail_rows < cap, tail_rows, -1),
    ]).astype(jnp.int32)
    xs = _dispatch(h2, g, dest, zrows, cap)
    ys = _expert_ffn(xs, block_expert, n_used, w1, w3, w2, j)
    return _combine(h2, gates, dest, ys, final_g, final_norm)


def _block_diag(w):
    nh, hd, _ = w.shape
    return jnp.einsum("hde,hg->hdge", w, jnp.eye(nh, dtype=w.dtype)).reshape(nh * hd, nh * hd)


def _group_matrices(d_mix):
    grp = jnp.arange(d_mix) // HEAD_DIM
    cols = jnp.arange(LANES)
    g1 = jnp.where(grp[:, None] == cols[None, :], 1.0 / HEAD_DIM, 0.0).astype(BF16)
    e = jnp.where(cols[:, None] == grp[None, :], 1.0, 0.0).astype(BF16)
    return g1, jnp.concatenate([e, e], axis=0)


def kernel(x, norm_mix_g, w_in, conv_a_w, conv_a_b, conv_b_w, conv_b_b, lru_w_r, lru_b_r,
           lru_w_i, lru_b_i, lru_lambda, mix_out_g, w_out, norm_ffn_g, ffn_w1, ffn_w3, ffn_w2,
           router_w, expert_w1, expert_w3, expert_w2, final_g):
    b, s, d = x.shape
    depth = w_in.shape[0]
    g1, e2 = _group_matrices(D_CONV + D_LRU)
    row = lambda v: v.reshape(1, -1)
    h = x
    for layer in range(depth):
        p = {
            "norm_mix_g": row(norm_mix_g[layer]),
            "w_in": w_in[layer].astype(BF16),
            "conv_a_w": conv_a_w[layer], "conv_a_b": row(conv_a_b[layer]),
            "conv_b_w": conv_b_w[layer], "conv_b_b": row(conv_b_b[layer]),
            "w_gate": jnp.concatenate(
                [_block_diag(lru_w_r[layer]), _block_diag(lru_w_i[layer])], axis=1).astype(BF16),
            "b_r": row(lru_b_r[layer]), "b_i": row(lru_b_i[layer]), "lam": row(lru_lambda[layer]),
            "g_mix": row(mix_out_g[layer]), "g1": g1, "e2": e2,
            "w_out": w_out[layer].astype(BF16),
        }
        h = _mixer(h, p)
        h2 = h.reshape(b * s, d)
        gf = row(norm_ffn_g[layer])
        j = layer // 2
        if layer % 2 == 0:
            h2 = _dense_ffn(h2, gf, ffn_w1, ffn_w3, ffn_w2, j)
        else:
            w_hi = router_w[j].astype(BF16)
            w_lo = (router_w[j] - w_hi.astype(F32)).astype(BF16)
            h2 = _moe(h2, gf, jnp.concatenate([w_hi, w_lo], axis=1),
                      expert_w1, expert_w3, expert_w2, j, row(final_g), layer == depth - 1)
        h = h2.reshape(b, s, d)
    if depth % 2 == 1:
        raise NotImplementedError("final norm is fused into the last routed layer")
    return h
```

```python
import functools

import jax
import jax.numpy as jnp
from jax import lax
from jax.experimental import pallas as pl
from jax.experimental.pallas import tpu as pltpu

EPS = 1e-6
HEAD_DIM = 64
D_CONV = 512
D_LRU = 512
LRU_C = 8.0
N_EXPERTS = 8
TOP_K = 2

SUBLANES = 8
LANES = 128

MIX_TS = 512
FFN_TM = 1024
FFN_TF = 512
MOE_TM = 1024
ROUTE_TM = 512
ROW_TM = 512
ISSUE_UNROLL = 8
VMEM_LIMIT = 56 * 1024 * 1024

F32 = jnp.float32
BF16 = jnp.bfloat16


def _rms_norm_rows(x, g):
    return x * lax.rsqrt(jnp.mean(x * x, axis=-1, keepdims=True) + EPS) * g


def _shift_rows(u, tail, k):
    r = pltpu.roll(u, k, 0)
    rt = pltpu.roll(tail, k, 0)
    rows = lax.broadcasted_iota(jnp.int32, rt.shape, 0)
    first = jnp.where(rows < k, rt, r[0:SUBLANES])
    return jnp.concatenate([first, r[SUBLANES:]], axis=0)


def _sigmoid(x):
    return 1.0 / (1.0 + jnp.exp(-x))


def _gelu_tanh(x):
    return 0.5 * x * (1.0 + jnp.tanh(0.7978845608028654 * (x + 0.044715 * (x * x * x))))


def _mixer_kernel(h_ref, gn_ref, win_ref, caw_ref, cab_ref, cbw_ref, cbb_ref, wg_ref,
                  br_ref, bi_ref, lam_ref, gmix_ref, g1_ref, e2_ref, wout_ref,
                  o_ref, taila_ref, tailb_ref, hstate_ref, a_scr, b_scr):
    ts = h_ref.shape[0]

    @pl.when(pl.program_id(1) == 0)
    def _():
        taila_ref[...] = jnp.zeros_like(taila_ref)
        tailb_ref[...] = jnp.zeros_like(tailb_ref)
        hstate_ref[...] = jnp.zeros_like(hstate_ref)

    x = h_ref[...]
    xn = _rms_norm_rows(x, gn_ref[...])
    z = jnp.dot(xn.astype(BF16), win_ref[...], preferred_element_type=F32)
    xa = z[:, 0:D_CONV]
    gb = z[:, D_CONV:2 * D_CONV]
    gc = z[:, 2 * D_CONV:3 * D_CONV]
    xr = z[:, 3 * D_CONV:3 * D_CONV + D_LRU]
    gr = z[:, 3 * D_CONV + D_LRU:]

    ua = gc * xa
    taila = taila_ref[...]
    ca = (caw_ref[2:3, :] * ua + caw_ref[1:2, :] * _shift_rows(ua, taila, 1)
          + caw_ref[0:1, :] * _shift_rows(ua, taila, 2) + cab_ref[...])
    taila_ref[...] = ua[ts - SUBLANES:, :]
    ya = gb * ca

    tailb = tailb_ref[...]
    xc = (cbw_ref[3:4, :] * xr + cbw_ref[2:3, :] * _shift_rows(xr, tailb, 1)
          + cbw_ref[1:2, :] * _shift_rows(xr, tailb, 2)
          + cbw_ref[0:1, :] * _shift_rows(xr, tailb, 3) + cbb_ref[...])
    tailb_ref[...] = xr[ts - SUBLANES:, :]

    gz = jnp.dot(xc.astype(BF16), wg_ref[...], preferred_element_type=F32)
    r = _sigmoid(gz[:, 0:D_LRU] + br_ref[...])
    i = _sigmoid(gz[:, D_LRU:] + bi_ref[...])
    nl = -lam_ref[...]
    softplus = jnp.maximum(nl, 0.0) + jnp.log1p(jnp.exp(-jnp.abs(nl)))
    log_a = (-LRU_C) * r * softplus
    a = jnp.exp(log_a)
    u = jnp.sqrt(1.0 - a * a) * (i * xc)

    rows = lax.broadcasted_iota(jnp.int32, a.shape, 0) & (SUBLANES - 1)
    for d in (1, 2, 4):
        a_sh = pltpu.roll(a, d, 0)
        u_sh = pltpu.roll(u, d, 0)
        m = rows >= d
        u = jnp.where(m, a * u_sh + u, u)
        a = jnp.where(m, a * a_sh, a)
    a_scr[...] = a
    b_scr[...] = u
    carry = hstate_ref[...]
    for g in range(ts // SUBLANES):
        sl = pl.ds(g * SUBLANES, SUBLANES)
        hg = a_scr[sl, :] * carry + b_scr[sl, :]
        b_scr[sl, :] = hg
        carry = jnp.broadcast_to(hg[SUBLANES - 1:SUBLANES, :], hg.shape)
    hstate_ref[...] = carry
    yr = b_scr[...] * _gelu_tanh(gr)

    y = jnp.concatenate([ya, yr], axis=-1)
    gm = jnp.dot((y * y).astype(BF16), g1_ref[...], preferred_element_type=F32)
    rs = lax.rsqrt(gm + EPS)
    rs_hi = rs.astype(BF16)
    rs_lo = (rs - rs_hi.astype(F32)).astype(BF16)
    rs_full = jnp.dot(jnp.concatenate([rs_hi, rs_lo], axis=-1), e2_ref[...],
                      preferred_element_type=F32)
    yn = y * rs_full * gmix_ref[...]
    o_ref[...] = x + jnp.dot(yn.astype(BF16), wout_ref[...], preferred_element_type=F32)


def _mixer(h, p):
    b, s, d = h.shape
    ts = min(MIX_TS, s)
    d_in = p["w_in"].shape[1]
    d_mix = D_CONV + D_LRU
    const = lambda shape: pl.BlockSpec(shape, lambda bi, j: (0,) * len(shape))
    return pl.pallas_call(
        _mixer_kernel,
        grid=(b, s // ts),
        in_specs=[
            pl.BlockSpec((None, ts, d), lambda bi, j: (bi, j, 0)),
            const((1, d)), const((d, d_in)),
            const((3, D_CONV)), const((1, D_CONV)), const((4, D_LRU)), const((1, D_LRU)),
            const((D_LRU, 2 * D_LRU)), const((1, D_LRU)), const((1, D_LRU)), const((1, D_LRU)),
            const((1, d_mix)), const((d_mix, LANES)), const((2 * LANES, d_mix)), const((d_mix, d)),
        ],
        out_specs=pl.BlockSpec((None, ts, d), lambda bi, j: (bi, j, 0)),
        out_shape=jax.ShapeDtypeStruct((b, s, d), F32),
        scratch_shapes=[
            pltpu.VMEM((SUBLANES, D_CONV), F32), pltpu.VMEM((SUBLANES, D_LRU), F32),
            pltpu.VMEM((SUBLANES, D_LRU), F32),
            pltpu.VMEM((ts, D_LRU), F32), pltpu.VMEM((ts, D_LRU), F32),
        ],
        compiler_params=pltpu.CompilerParams(
            dimension_semantics=("arbitrary", "arbitrary"), vmem_limit_bytes=VMEM_LIMIT),
        name="mixer",
    )(h, p["norm_mix_g"], p["w_in"], p["conv_a_w"], p["conv_a_b"], p["conv_b_w"], p["conv_b_b"],
      p["w_gate"], p["b_r"], p["b_i"], p["lam"], p["g_mix"], p["g1"], p["e2"], p["w_out"])


def _swiglu_step(xn_scr, acc_scr, w1_ref, w3_ref, w2_ref):
    xn = xn_scr[...]
    h1 = jnp.dot(xn, w1_ref[...].astype(BF16), preferred_element_type=F32)
    h3 = jnp.dot(xn, w3_ref[...].astype(BF16), preferred_element_type=F32)
    g = (h1 * _sigmoid(h1)) * h3
    acc_scr[...] += jnp.dot(g.astype(BF16), w2_ref[...].astype(BF16),
                            preferred_element_type=F32)


def _dense_ffn_kernel(x_ref, g_ref, w1_ref, w3_ref, w2_ref, o_ref, xn_scr, acc_scr):
    f = pl.program_id(1)

    @pl.when(f == 0)
    def _():
        xn_scr[...] = _rms_norm_rows(x_ref[...], g_ref[...]).astype(BF16)
        acc_scr[...] = jnp.zeros_like(acc_scr)

    _swiglu_step(xn_scr, acc_scr, w1_ref, w3_ref, w2_ref)

    @pl.when(f == pl.num_programs(1) - 1)
    def _():
        o_ref[...] = x_ref[...] + acc_scr[...]


def _dense_ffn(h2, g, w1, w3, w2, j):
    t, d = h2.shape
    ff = w1.shape[2]
    tm = min(FFN_TM, t)
    tf = FFN_TF
    return pl.pallas_call(
        _dense_ffn_kernel,
        grid=(t // tm, ff // tf),
        in_specs=[
            pl.BlockSpec((tm, d), lambda i, f: (i, 0)),
            pl.BlockSpec((1, d), lambda i, f: (0, 0)),
            pl.BlockSpec((None, d, tf), lambda i, f: (j, 0, f)),
            pl.BlockSpec((None, d, tf), lambda i, f: (j, 0, f)),
            pl.BlockSpec((None, tf, d), lambda i, f: (j, f, 0)),
        ],
        out_specs=pl.BlockSpec((tm, d), lambda i, f: (i, 0)),
        out_shape=jax.ShapeDtypeStruct((t, d), F32),
        scratch_shapes=[pltpu.VMEM((tm, d), BF16), pltpu.VMEM((tm, d), F32)],
        compiler_params=pltpu.CompilerParams(
            dimension_semantics=("arbitrary", "arbitrary"), vmem_limit_bytes=VMEM_LIMIT),
        name="dense_ffn",
    )(h2, g, w1, w3, w2)


def _token_major_index(idx, c, n_tokens):
    return idx + (pl.ds(c, n_tokens, stride=SUBLANES), slice(None))


def _load_token_major(ref, idx, n_tokens, d):
    assert d == SUBLANES * LANES
    return jnp.concatenate(
        [ref[_token_major_index(idx, c, n_tokens)] for c in range(SUBLANES)], axis=-1)


def _store_token_major(ref, idx, value):
    n_tokens, d = value.shape
    assert d == SUBLANES * LANES
    for c in range(SUBLANES):
        ref[_token_major_index(idx, c, n_tokens)] = value[:, c * LANES:(c + 1) * LANES]


def _token_rows(t):
    return pl.ds(pl.multiple_of(t * SUBLANES, SUBLANES), SUBLANES)


def _expert_ffn_kernel(be_ref, nused_ref, x_ref, w1_ref, w3_ref, w2_ref, o_ref, xn_scr, acc_scr):
    i = pl.program_id(0)
    f = pl.program_id(1)
    used = i < nused_ref[0]
    tm, d = xn_scr.shape

    @pl.when(jnp.logical_and(used, f == 0))
    def _():
        xn_scr[...] = _load_token_major(x_ref, (), tm, d).astype(BF16)
        acc_scr[...] = jnp.zeros_like(acc_scr)

    @pl.when(used)
    def _():
        _swiglu_step(xn_scr, acc_scr, w1_ref, w3_ref, w2_ref)

    @pl.when(f == pl.num_programs(1) - 1)
    def _():
        _store_token_major(o_ref, (), jnp.where(used, acc_scr[...], 0.0))


def _expert_ffn(xs, block_expert, n_used, w1, w3, w2, j):
    d = w1.shape[2]
    cap = xs.shape[0] // SUBLANES
    ff = w1.shape[3]
    tm = MOE_TM
    tf = FFN_TF
    grid_spec = pltpu.PrefetchScalarGridSpec(
        num_scalar_prefetch=2,
        grid=(cap // tm, ff // tf),
        in_specs=[
            pl.BlockSpec((tm * SUBLANES, LANES), lambda i, f, be, nu: (i, 0)),
            pl.BlockSpec((None, None, d, tf), lambda i, f, be, nu: (j, be[i], 0, f)),
            pl.BlockSpec((None, None, d, tf), lambda i, f, be, nu: (j, be[i], 0, f)),
            pl.BlockSpec((None, None, tf, d), lambda i, f, be, nu: (j, be[i], f, 0)),
        ],
        out_specs=pl.BlockSpec((tm * SUBLANES, LANES), lambda i, f, be, nu: (i, 0)),
        scratch_shapes=[pltpu.VMEM((tm, d), BF16), pltpu.VMEM((tm, d), F32)],
    )
    return pl.pallas_call(
        _expert_ffn_kernel,
        grid_spec=grid_spec,
        out_shape=jax.ShapeDtypeStruct((cap * SUBLANES, LANES), F32),
        compiler_params=pltpu.CompilerParams(
            dimension_semantics=("arbitrary", "arbitrary"), vmem_limit_bytes=VMEM_LIMIT),
        name="expert_ffn",
    )(block_expert, n_used, xs, w1, w3, w2)


def _router_kernel(x_ref, g_ref, wcat_ref, eid_ref, gate_ref, rank_ref, cnt_ref, carry_ref):
    tm = x_ref.shape[0]

    @pl.when(pl.program_id(0) == 0)
    def _():
        carry_ref[...] = jnp.zeros_like(carry_ref)

    xn = _rms_norm_rows(x_ref[...], g_ref[...])
    x_hi = xn.astype(BF16)
    x_lo = (xn - x_hi.astype(F32)).astype(BF16)
    wcat = wcat_ref[...]
    l_hi = jnp.dot(x_hi, wcat, preferred_element_type=F32)
    l_lo = jnp.dot(x_lo, wcat, preferred_element_type=F32)
    logits = l_hi[:, 0:N_EXPERTS] + l_hi[:, N_EXPERTS:] + l_lo[:, 0:N_EXPERTS]

    lane = lax.broadcasted_iota(jnp.int32, logits.shape, 1)
    m1 = jnp.max(logits, axis=-1, keepdims=True)
    i1 = jnp.min(jnp.where(logits == m1, lane, N_EXPERTS), axis=-1, keepdims=True)
    rest = jnp.where(lane == i1, -jnp.inf, logits)
    m2 = jnp.max(rest, axis=-1, keepdims=True)
    i2 = jnp.min(jnp.where(rest == m2, lane, N_EXPERTS), axis=-1, keepdims=True)
    e2 = jnp.exp(m2 - m1)
    denom = 1.0 + e2
    gate_ref[...] = jnp.concatenate([1.0 / denom, e2 / denom], axis=-1)
    eid_ref[...] = jnp.concatenate([i1, i2], axis=-1)

    sel1 = lane == i1
    sel2 = lane == i2
    onehot = jnp.where(jnp.logical_or(sel1, sel2), 1.0, 0.0)
    r_io = lax.broadcasted_iota(jnp.int32, (tm, tm), 0)
    c_io = lax.broadcasted_iota(jnp.int32, (tm, tm), 1)
    tri = jnp.where(c_io < r_io, 1.0, 0.0).astype(BF16)
    before = jnp.dot(tri, onehot.astype(BF16), preferred_element_type=F32) + carry_ref[...]
    rank1 = jnp.sum(jnp.where(sel1, before, 0.0), axis=-1, keepdims=True)
    rank2 = jnp.sum(jnp.where(sel2, before, 0.0), axis=-1, keepdims=True)
    rank_ref[...] = jnp.concatenate([rank1, rank2], axis=-1).astype(jnp.int32)
    total = carry_ref[...] + jnp.sum(onehot, axis=0, keepdims=True)
    carry_ref[...] = total
    cnt_ref[...] = total.astype(jnp.int32)


def _router(h2, g, wcat):
    t, d = h2.shape
    tm = min(ROUTE_TM, t)
    pair = lambda dt: jax.ShapeDtypeStruct((t, TOP_K), dt)
    return pl.pallas_call(
        _router_kernel,
        grid=(t // tm,),
        in_specs=[
            pl.BlockSpec((tm, d), lambda i: (i, 0)),
            pl.BlockSpec((1, d), lambda i: (0, 0)),
            pl.BlockSpec((d, 2 * N_EXPERTS), lambda i: (0, 0)),
        ],
        out_specs=[
            pl.BlockSpec((tm, TOP_K), lambda i: (i, 0)),
            pl.BlockSpec((tm, TOP_K), lambda i: (i, 0)),
            pl.BlockSpec((tm, TOP_K), lambda i: (i, 0)),
            pl.BlockSpec((1, N_EXPERTS), lambda i: (0, 0)),
        ],
        out_shape=[pair(jnp.int32), pair(F32), pair(jnp.int32),
                   jax.ShapeDtypeStruct((1, N_EXPERTS), jnp.int32)],
        scratch_shapes=[pltpu.VMEM((1, N_EXPERTS), F32)],
        compiler_params=pltpu.CompilerParams(dimension_semantics=("arbitrary",)),
        name="router",
    )(h2, g, wcat)


def _start_row_copies(n_rows, row_copy):
    def body(c, carry):
        for u in range(ISSUE_UNROLL):
            for k in range(TOP_K):
                row_copy(c * ISSUE_UNROLL + u, k).start(priority=k)
        return carry

    lax.fori_loop(0, n_rows // ISSUE_UNROLL, body, 0)


def _dispatch_kernel(zrow_ref, dest_ref, x_ref, g_ref, xs_ref, xn_scr, zero_scr, sem, zsem):
    i = pl.program_id(0)
    tm = x_ref.shape[0]
    slot = i % 2

    @pl.when(i == 0)
    def _():
        zero_scr[...] = jnp.zeros_like(zero_scr)
        zcopies = [
            pltpu.make_async_copy(
                zero_scr,
                xs_ref.at[pl.ds(pl.multiple_of(jnp.maximum(zrow_ref[n], 0) * SUBLANES, MOE_TM),
                                MOE_TM * SUBLANES)],
                zsem)
            for n in range(2 * N_EXPERTS)]
        for n, c in enumerate(zcopies):
            pl.when(zrow_ref[n] >= 0)(c.start)
        for n, c in enumerate(zcopies):
            pl.when(zrow_ref[n] >= 0)(c.wait)

    _store_token_major(xn_scr, (slot,), _rms_norm_rows(x_ref[...], g_ref[...]))

    def row_copy(t, k):
        d = dest_ref[0, 0, TOP_K * t + k]
        return pltpu.make_async_copy(
            xn_scr.at[slot, _token_rows(t)], xs_ref.at[_token_rows(d)], sem.at[slot])

    _start_row_copies(tm, row_copy)

    def wait_slot(s):
        for _ in range(TOP_K):
            pltpu.make_async_copy(
                xn_scr.at[s], xs_ref.at[pl.ds(0, tm * SUBLANES)], sem.at[s]).wait()

    @pl.when(i > 0)
    def _():
        wait_slot(1 - slot)

    @pl.when(i == pl.num_programs(0) - 1)
    def _():
        wait_slot(slot)


def _dispatch(h2, g, dest, zrows, cap):
    t, d = h2.shape
    tm = min(ROW_TM, t)
    dest3 = dest.reshape(t // tm, 1, tm * TOP_K)
    grid_spec = pltpu.PrefetchScalarGridSpec(
        num_scalar_prefetch=1,
        grid=(t // tm,),
        in_specs=[
            pl.BlockSpec((1, 1, tm * TOP_K), lambda i, z: (i, 0, 0), memory_space=pltpu.SMEM),
            pl.BlockSpec((tm, d), lambda i, z: (i, 0)),
            pl.BlockSpec((1, d), lambda i, z: (0, 0)),
        ],
        out_specs=pl.BlockSpec(memory_space=pl.ANY),
        scratch_shapes=[pltpu.VMEM((2, tm * SUBLANES, LANES), F32),
                        pltpu.VMEM((MOE_TM * SUBLANES, LANES), F32),
                        pltpu.SemaphoreType.DMA((2,)), pltpu.SemaphoreType.DMA],
    )
    return pl.pallas_call(
        _dispatch_kernel,
        grid_spec=grid_spec,
        out_shape=jax.ShapeDtypeStruct((cap * SUBLANES, LANES), F32),
        compiler_params=pltpu.CompilerParams(
            dimension_semantics=("arbitrary",), vmem_limit_bytes=VMEM_LIMIT),
        name="dispatch",
    )(zrows, dest3, h2, g)


def _combine_kernel(pos_ref, posn_ref, h_ref, gate_ref, fg_ref, ys_ref, o_ref, buf, sem,
                    *, final_norm):
    i = pl.program_id(0)
    n = pl.num_programs(0)
    tm = h_ref.shape[0]
    slot = i % 2

    def start_gather(p_ref, s):
        def row_copy(t, k):
            p = p_ref[0, 0, TOP_K * t + k]
            return pltpu.make_async_copy(
                ys_ref.at[_token_rows(p)], buf.at[s, k, _token_rows(t)], sem.at[s])
        _start_row_copies(tm, row_copy)

    @pl.when(i == 0)
    def _():
        start_gather(pos_ref, slot)

    @pl.when(i + 1 < n)
    def _():
        start_gather(posn_ref, 1 - slot)

    for k in range(TOP_K):
        pltpu.make_async_copy(
            ys_ref.at[pl.ds(0, tm * SUBLANES)], buf.at[slot, k], sem.at[slot]).wait()
    gate = gate_ref[...]
    d = h_ref.shape[1]
    out = (h_ref[...] + gate[:, 0:1] * _load_token_major(buf, (slot, 0), tm, d)
           + gate[:, 1:2] * _load_token_major(buf, (slot, 1), tm, d))
    if final_norm:
        out = _rms_norm_rows(out, fg_ref[...])
    o_ref[...] = out


def _combine(h2, gates, pos, ys, final_g, final_norm):
    t, d = h2.shape
    tm = min(ROW_TM, t)
    nt = t // tm
    pos3 = pos.reshape(nt, 1, tm * TOP_K)
    return pl.pallas_call(
        functools.partial(_combine_kernel, final_norm=final_norm),
        grid=(nt,),
        in_specs=[
            pl.BlockSpec((1, 1, tm * TOP_K), lambda i: (i, 0, 0), memory_space=pltpu.SMEM),
            pl.BlockSpec((1, 1, tm * TOP_K), lambda i: (jnp.minimum(i + 1, nt - 1), 0, 0),
                         memory_space=pltpu.SMEM),
            pl.BlockSpec((tm, d), lambda i: (i, 0)),
            pl.BlockSpec((tm, TOP_K), lambda i: (i, 0)),
            pl.BlockSpec((1, d), lambda i: (0, 0)),
            pl.BlockSpec(memory_space=pl.ANY),
        ],
        out_specs=pl.BlockSpec((tm, d), lambda i: (i, 0)),
        out_shape=jax.ShapeDtypeStruct((t, d), F32),
        scratch_shapes=[pltpu.VMEM((2, TOP_K, tm * SUBLANES, LANES), F32),
                        pltpu.SemaphoreType.DMA((2,))],
        compiler_params=pltpu.CompilerParams(
            dimension_semantics=("arbitrary",), vmem_limit_bytes=VMEM_LIMIT),
        name="combine",
    )(pos3, pos3, h2, gates, final_g, ys)


def _moe(h2, g, wcat, w1, w3, w2, j, final_g, final_norm):
    t, d = h2.shape
    eid, gates, rank, counts = _router(h2, g, wcat)
    n_assign = t * TOP_K
    assert n_assign % MOE_TM == 0
    n_blocks = n_assign // MOE_TM + N_EXPERTS
    cap = n_blocks * MOE_TM
    counts = counts[0]
    padded = (counts + MOE_TM - 1) // MOE_TM * MOE_TM
    pad_end = jnp.cumsum(padded)
    pad_start = pad_end - padded
    dest = pad_start[eid] + rank
    block_start = jnp.arange(n_blocks, dtype=jnp.int32) * MOE_TM
    block_expert = jnp.minimum(
        jnp.searchsorted(pad_end, block_start, side="right"), N_EXPERTS - 1).astype(jnp.int32)
    n_used = (pad_end[-1:] // MOE_TM).astype(jnp.int32)
    tail_rows = pad_end[-1] + jnp.arange(N_EXPERTS, dtype=jnp.int32) * MOE_TM
    zrows = jnp.concatenate([
        jnp.where(padded > 0, pad_end - MOE_TM, -1),
        jnp.where(tail_rows < cap, tail_rows, -1),
    ]).astype(jnp.int32)
    xs = _dispatch(h2, g, dest, zrows, cap)
    ys = _expert_ffn(xs, block_expert, n_used, w1, w3, w2, j)
    return _combine(h2, gates, dest, ys, final_g, final_norm)


def _block_diag(w):
    nh, hd, _ = w.shape
    return jnp.einsum("hde,hg->hdge", w, jnp.eye(nh, dtype=w.dtype)).reshape(nh * hd, nh * hd)


def _group_matrices(d_mix):
    grp = jnp.arange(d_mix) // HEAD_DIM
    cols = jnp.arange(LANES)
    g1 = jnp.where(grp[:, None] == cols[None, :], 1.0 / HEAD_DIM, 0.0).astype(BF16)
    e = jnp.where(cols[:, None] == grp[None, :], 1.0, 0.0).astype(BF16)
    return g1, jnp.concatenate([e, e], axis=0)


def kernel(x, norm_mix_g, w_in, conv_a_w, conv_a_b, conv_b_w, conv_b_b, lru_w_r, lru_b_r,
           lru_w_i, lru_b_i, lru_lambda, mix_out_g, w_out, norm_ffn_g, ffn_w1, ffn_w3, ffn_w2,
           router_w, expert_w1, expert_w3, expert_w2, final_g):
    b, s, d = x.shape
    depth = w_in.shape[0]
    g1, e2 = _group_matrices(D_CONV + D_LRU)
    row = lambda v: v.reshape(1, -1)
    h = x
    for layer in range(depth):
        p = {
            "norm_mix_g": row(norm_mix_g[layer]),
            "w_in": w_in[layer].astype(BF16),
            "conv_a_w": conv_a_w[layer], "conv_a_b": row(conv_a_b[layer]),
            "conv_b_w": conv_b_w[layer], "conv_b_b": row(conv_b_b[layer]),
            "w_gate": jnp.concatenate(
                [_block_diag(lru_w_r[layer]), _block_diag(lru_w_i[layer])], axis=1).astype(BF16),
            "b_r": row(lru_b_r[layer]), "b_i": row(lru_b_i[layer]), "lam": row(lru_lambda[layer]),
            "g_mix": row(mix_out_g[layer]), "g1": g1, "e2": e2,
            "w_out": w_out[layer].astype(BF16),
        }
        h = _mixer(h, p)
        h2 = h.reshape(b * s, d)
        gf = row(norm_ffn_g[layer])
        j = layer // 2
        if layer % 2 == 0:
            h2 = _dense_ffn(h2, gf, ffn_w1, ffn_w3, ffn_w2, j)
        else:
            w_hi = router_w[j].astype(BF16)
            w_lo = (router_w[j] - w_hi.astype(F32)).astype(BF16)
            h2 = _moe(h2, gf, jnp.concatenate([w_hi, w_lo], axis=1),
                      expert_w1, expert_w3, expert_w2, j, row(final_g), layer == depth - 1)
        h = h2.reshape(b, s, d)
    if depth % 2 == 1:
        raise NotImplementedError("final norm is fused into the last routed layer")
    return h
```

```python
import functools

import jax
import jax.numpy as jnp
from jax import lax
from jax.experimental import pallas as pl
from jax.experimental.pallas import tpu as pltpu

EPS = 1e-6
HEAD_DIM = 64
D_CONV = 512
D_LRU = 512
LRU_C = 8.0
N_EXPERTS = 8
TOP_K = 2

SUBLANES = 8
LANES = 128
MXU_DIM = 256

MIX_TS = 512
FFN_TM = 1024
FFN_TF = 512
MOE_TM = 1024
ROUTE_TM = 512
ROW_TM = 512
ISSUE_UNROLL = 8
VMEM_LIMIT = 56 * 1024 * 1024

F32 = jnp.float32
BF16 = jnp.bfloat16


def _rms_norm_rows(x, g):
    return x * lax.rsqrt(jnp.mean(x * x, axis=-1, keepdims=True) + EPS) * g


def _sigmoid(x):
    return 1.0 / (1.0 + jnp.exp(-x))


def _gelu_tanh(x):
    return 0.5 * x * (1.0 + jnp.tanh(0.7978845608028654 * (x + 0.044715 * (x * x * x))))


def _phases(v, g):
    return [v[r * g:(r + 1) * g] for r in range(SUBLANES)]


def _group_shift(v, first_row):
    rolled = pltpu.roll(v, 1, 0)
    rows = lax.broadcasted_iota(jnp.int32, (SUBLANES, v.shape[1]), 0)
    head = jnp.where(rows == 0, jnp.broadcast_to(first_row, (SUBLANES, v.shape[1])),
                     rolled[0:SUBLANES])
    return jnp.concatenate([head, rolled[SUBLANES:]], axis=0)


def _causal_conv(phases, tail, w_ref, b_ref):
    taps = w_ref.shape[0]
    bias = b_ref[...]
    shifted = {}

    def phase(idx):
        if idx >= 0:
            return phases[idx]
        if idx not in shifted:
            src = idx + SUBLANES
            shifted[idx] = _group_shift(phases[src], tail[src:src + 1])
        return shifted[idx]

    out = []
    for r in range(SUBLANES):
        acc = w_ref[taps - 1:taps, :] * phases[r] + bias
        for k in range(taps - 1):
            acc = acc + w_ref[k:k + 1, :] * phase(r - (taps - 1) + k)
        out.append(acc)
    return out


def _last_group_rows(phases):
    return jnp.concatenate([p[p.shape[0] - 1:p.shape[0]] for p in phases], axis=0)


def _mixer_kernel(h_ref, gn_ref, win_ref, caw_ref, cab_ref, cbw_ref, cbb_ref, wg_ref,
                  br_ref, bi_ref, lam_ref, gmix_ref, g1_ref, e2_ref, wout_ref,
                  o_ref, taila_ref, tailb_ref, hstate_ref, perm_scr):
    ts = h_ref.shape[0]
    grp = ts // SUBLANES

    @pl.when(pl.program_id(1) == 0)
    def _():
        taila_ref[...] = jnp.zeros_like(taila_ref)
        tailb_ref[...] = jnp.zeros_like(tailb_ref)
        hstate_ref[...] = jnp.zeros_like(hstate_ref)

    n_chunks = h_ref.shape[1] // LANES
    for c in range(n_chunks):
        perm_scr[c] = h_ref[:, c * LANES:(c + 1) * LANES]
    x = jnp.concatenate(
        [jnp.concatenate([perm_scr[c, pl.ds(r, grp, stride=SUBLANES), :]
                          for c in range(n_chunks)], axis=-1)
         for r in range(SUBLANES)], axis=0)
    xn = _rms_norm_rows(x, gn_ref[...])
    z = jnp.dot(xn.astype(BF16), win_ref[...], preferred_element_type=F32)
    xa = z[:, 0:D_CONV]
    gb = z[:, D_CONV:2 * D_CONV]
    gc = z[:, 2 * D_CONV:3 * D_CONV]
    xr = z[:, 3 * D_CONV:3 * D_CONV + D_LRU]
    gr = z[:, 3 * D_CONV + D_LRU:]

    ua = _phases(gc * xa, grp)
    ca = jnp.concatenate(_causal_conv(ua, taila_ref[...], caw_ref, cab_ref), axis=0)
    taila_ref[...] = _last_group_rows(ua)
    ya = gb * ca

    xrp = _phases(xr, grp)
    xc = jnp.concatenate(_causal_conv(xrp, tailb_ref[...], cbw_ref, cbb_ref), axis=0)
    tailb_ref[...] = _last_group_rows(xrp)

    xcb = xc.astype(BF16)
    gz = [jnp.dot(xcb[:, c * MXU_DIM:(c + 1) * MXU_DIM], wg_ref[c], preferred_element_type=F32)
          for c in range(D_LRU // MXU_DIM)]
    gz_r = jnp.concatenate([g[:, 0:MXU_DIM] for g in gz], axis=-1)
    gz_i = jnp.concatenate([g[:, MXU_DIM:] for g in gz], axis=-1)
    r = _sigmoid(gz_r + br_ref[...])
    i = _sigmoid(gz_i + bi_ref[...])
    nl = -lam_ref[...]
    softplus = jnp.maximum(nl, 0.0) + jnp.log1p(jnp.exp(-jnp.abs(nl)))
    log_a = (-LRU_C) * r * softplus
    a = jnp.exp(log_a)
    u = jnp.sqrt(1.0 - a * a) * (i * xc)

    ap = _phases(a, grp)
    up = _phases(u, grp)
    hloc = [up[0]]
    ploc = [ap[0]]
    for ph in range(1, SUBLANES):
        hloc.append(ap[ph] * hloc[ph - 1] + up[ph])
        ploc.append(ap[ph] * ploc[ph - 1])
    ps = ploc[SUBLANES - 1]
    hs = hloc[SUBLANES - 1]
    rows = lax.broadcasted_iota(jnp.int32, ps.shape, 0)
    d = 1
    while d < grp:
        m = rows >= d
        hs = jnp.where(m, ps * pltpu.roll(hs, d, 0) + hs, hs)
        ps = jnp.where(m, ps * pltpu.roll(ps, d, 0), ps)
        d *= 2
    c0 = hstate_ref[0:1, :]
    after = ps * c0 + hs
    cin = _group_shift(after, c0)
    hstate_ref[...] = jnp.broadcast_to(after[grp - 1:grp], hstate_ref.shape)
    hh = jnp.concatenate([hloc[ph] + ploc[ph] * cin for ph in range(SUBLANES)], axis=0)
    yr = hh * _gelu_tanh(gr)

    y = jnp.concatenate([ya, yr], axis=-1)
    gm = jnp.dot((y * y).astype(BF16), g1_ref[...], preferred_element_type=F32)
    rs = lax.rsqrt(gm + EPS)
    rs_hi = rs.astype(BF16)
    rs_lo = (rs - rs_hi.astype(F32)).astype(BF16)
    rs_full = jnp.dot(jnp.concatenate([rs_hi, rs_lo], axis=-1), e2_ref[...],
                      preferred_element_type=F32)
    yn = y * rs_full * gmix_ref[...]
    out = x + jnp.dot(yn.astype(BF16), wout_ref[...], preferred_element_type=F32)
    for c in range(n_chunks):
        for ph in range(SUBLANES):
            perm_scr[c, pl.ds(ph, grp, stride=SUBLANES), :] = (
                out[ph * grp:(ph + 1) * grp, c * LANES:(c + 1) * LANES])
    o_ref[...] = jnp.concatenate([perm_scr[c] for c in range(n_chunks)], axis=-1)


def _mixer(h, p):
    b, s, d = h.shape
    ts = min(MIX_TS, s)
    assert s % ts == 0 and ts % (SUBLANES * SUBLANES) == 0
    d_in = p["w_in"].shape[1]
    d_mix = D_CONV + D_LRU
    const = lambda shape: pl.BlockSpec(shape, lambda bi, j: (0,) * len(shape))
    return pl.pallas_call(
        _mixer_kernel,
        grid=(b, s // ts),
        in_specs=[
            pl.BlockSpec((None, ts, d), lambda bi, j: (bi, j, 0)),
            const((1, d)), const((d, d_in)),
            const((3, D_CONV)), const((1, D_CONV)), const((4, D_LRU)), const((1, D_LRU)),
            const((D_LRU // MXU_DIM, MXU_DIM, 2 * MXU_DIM)),
            const((1, D_LRU)), const((1, D_LRU)), const((1, D_LRU)),
            const((1, d_mix)), const((d_mix, LANES)), const((2 * LANES, d_mix)), const((d_mix, d)),
        ],
        out_specs=pl.BlockSpec((None, ts, d), lambda bi, j: (bi, j, 0)),
        out_shape=jax.ShapeDtypeStruct((b, s, d), F32),
        scratch_shapes=[
            pltpu.VMEM((SUBLANES, D_CONV), F32), pltpu.VMEM((SUBLANES, D_LRU), F32),
            pltpu.VMEM((SUBLANES, D_LRU), F32),
            pltpu.VMEM((d // LANES, ts, LANES), F32),
        ],
        compiler_params=pltpu.CompilerParams(
            dimension_semantics=("arbitrary", "arbitrary"), vmem_limit_bytes=VMEM_LIMIT),
        name="mixer",
    )(h, p["norm_mix_g"], p["w_in"], p["conv_a_w"], p["conv_a_b"], p["conv_b_w"], p["conv_b_b"],
      p["w_gate"], p["b_r"], p["b_i"], p["lam"], p["g_mix"], p["g1"], p["e2"], p["w_out"])


def _swiglu_step(xn_scr, acc_scr, w1_ref, w3_ref, w2_ref):
    xn = xn_scr[...]
    h1 = jnp.dot(xn, w1_ref[...].astype(BF16), preferred_element_type=F32)
    h3 = jnp.dot(xn, w3_ref[...].astype(BF16), preferred_element_type=F32)
    g = (h1 * _sigmoid(h1)) * h3
    acc_scr[...] += jnp.dot(g.astype(BF16), w2_ref[...].astype(BF16),
                            preferred_element_type=F32)


def _dense_ffn_kernel(x_ref, g_ref, w1_ref, w3_ref, w2_ref, o_ref, xn_scr, acc_scr):
    f = pl.program_id(1)

    @pl.when(f == 0)
    def _():
        xn_scr[...] = _rms_norm_rows(x_ref[...], g_ref[...]).astype(BF16)
        acc_scr[...] = jnp.zeros_like(acc_scr)

    _swiglu_step(xn_scr, acc_scr, w1_ref, w3_ref, w2_ref)

    @pl.when(f == pl.num_programs(1) - 1)
    def _():
        o_ref[...] = x_ref[...] + acc_scr[...]


def _dense_ffn(h2, g, w1, w3, w2, j):
    t, d = h2.shape
    ff = w1.shape[2]
    tm = min(FFN_TM, t)
    tf = FFN_TF
    return pl.pallas_call(
        _dense_ffn_kernel,
        grid=(t // tm, ff // tf),
        in_specs=[
            pl.BlockSpec((tm, d), lambda i, f: (i, 0)),
            pl.BlockSpec((1, d), lambda i, f: (0, 0)),
            pl.BlockSpec((None, d, tf), lambda i, f: (j, 0, f)),
            pl.BlockSpec((None, d, tf), lambda i, f: (j, 0, f)),
            pl.BlockSpec((None, tf, d), lambda i, f: (j, f, 0)),
        ],
        out_specs=pl.BlockSpec((tm, d), lambda i, f: (i, 0)),
        out_shape=jax.ShapeDtypeStruct((t, d), F32),
        scratch_shapes=[pltpu.VMEM((tm, d), BF16), pltpu.VMEM((tm, d), F32)],
        compiler_params=pltpu.CompilerParams(
            dimension_semantics=("arbitrary", "arbitrary"), vmem_limit_bytes=VMEM_LIMIT),
        name="dense_ffn",
    )(h2, g, w1, w3, w2)


def _token_major_index(idx, c, n_tokens):
    return idx + (pl.ds(c, n_tokens, stride=SUBLANES), slice(None))


def _load_token_major(ref, idx, n_tokens, d):
    assert d == SUBLANES * LANES
    return jnp.concatenate(
        [ref[_token_major_index(idx, c, n_tokens)] for c in range(SUBLANES)], axis=-1)


def _store_token_major(ref, idx, value):
    n_tokens, d = value.shape
    assert d == SUBLANES * LANES
    for c in range(SUBLANES):
        ref[_token_major_index(idx, c, n_tokens)] = value[:, c * LANES:(c + 1) * LANES]


def _token_rows(t):
    return pl.ds(pl.multiple_of(t * SUBLANES, SUBLANES), SUBLANES)


def _expert_ffn_kernel(be_ref, nused_ref, x_ref, w1_ref, w3_ref, w2_ref, o_ref, xn_scr, acc_scr):
    i = pl.program_id(0)
    f = pl.program_id(1)
    used = i < nused_ref[0]
    tm, d = xn_scr.shape

    @pl.when(jnp.logical_and(used, f == 0))
    def _():
        xn_scr[...] = _load_token_major(x_ref, (), tm, d).astype(BF16)
        acc_scr[...] = jnp.zeros_like(acc_scr)

    @pl.when(used)
    def _():
        _swiglu_step(xn_scr, acc_scr, w1_ref, w3_ref, w2_ref)

    @pl.when(f == pl.num_programs(1) - 1)
    def _():
        _store_token_major(o_ref, (), jnp.where(used, acc_scr[...], 0.0))


def _expert_ffn(xs, block_expert, n_used, w1, w3, w2, j):
    d = w1.shape[2]
    cap = xs.shape[0] // SUBLANES
    ff = w1.shape[3]
    tm = MOE_TM
    tf = FFN_TF
    grid_spec = pltpu.PrefetchScalarGridSpec(
        num_scalar_prefetch=2,
        grid=(cap // tm, ff // tf),
        in_specs=[
            pl.BlockSpec((tm * SUBLANES, LANES), lambda i, f, be, nu: (i, 0)),
            pl.BlockSpec((None, None, d, tf), lambda i, f, be, nu: (j, be[i], 0, f)),
            pl.BlockSpec((None, None, d, tf), lambda i, f, be, nu: (j, be[i], 0, f)),
            pl.BlockSpec((None, None, tf, d), lambda i, f, be, nu: (j, be[i], f, 0)),
        ],
        out_specs=pl.BlockSpec((tm * SUBLANES, LANES), lambda i, f, be, nu: (i, 0)),
        scratch_shapes=[pltpu.VMEM((tm, d), BF16), pltpu.VMEM((tm, d), F32)],
    )
    return pl.pallas_call(
        _expert_ffn_kernel,
        grid_spec=grid_spec,
        out_shape=jax.ShapeDtypeStruct((cap * SUBLANES, LANES), F32),
        compiler_params=pltpu.CompilerParams(
            dimension_semantics=("arbitrary", "arbitrary"), vmem_limit_bytes=VMEM_LIMIT),
        name="expert_ffn",
    )(block_expert, n_used, xs, w1, w3, w2)


def _router_kernel(x_ref, g_ref, wcat_ref, eid_ref, gate_ref, rank_ref, cnt_ref, carry_ref):
    tm = x_ref.shape[0]

    @pl.when(pl.program_id(0) == 0)
    def _():
        carry_ref[...] = jnp.zeros_like(carry_ref)

    xn = _rms_norm_rows(x_ref[...], g_ref[...])
    x_hi = xn.astype(BF16)
    x_lo = (xn - x_hi.astype(F32)).astype(BF16)
    wcat = wcat_ref[...]
    l_hi = jnp.dot(x_hi, wcat, preferred_element_type=F32)
    l_lo = jnp.dot(x_lo, wcat, preferred_element_type=F32)
    logits = l_hi[:, 0:N_EXPERTS] + l_hi[:, N_EXPERTS:] + l_lo[:, 0:N_EXPERTS]

    lane = lax.broadcasted_iota(jnp.int32, logits.shape, 1)
    m1 = jnp.max(logits, axis=-1, keepdims=True)
    i1 = jnp.min(jnp.where(logits == m1, lane, N_EXPERTS), axis=-1, keepdims=True)
    rest = jnp.where(lane == i1, -jnp.inf, logits)
    m2 = jnp.max(rest, axis=-1, keepdims=True)
    i2 = jnp.min(jnp.where(rest == m2, lane, N_EXPERTS), axis=-1, keepdims=True)
    e2 = jnp.exp(m2 - m1)
    denom = 1.0 + e2
    gate_ref[...] = jnp.concatenate([1.0 / denom, e2 / denom], axis=-1)
    eid_ref[...] = jnp.concatenate([i1, i2], axis=-1)

    sel1 = lane == i1
    sel2 = lane == i2
    onehot = jnp.where(jnp.logical_or(sel1, sel2), 1.0, 0.0)
    r_io = lax.broadcasted_iota(jnp.int32, (tm, tm), 0)
    c_io = lax.broadcasted_iota(jnp.int32, (tm, tm), 1)
    tri = jnp.where(c_io < r_io, 1.0, 0.0).astype(BF16)
    before = jnp.dot(tri, onehot.astype(BF16), preferred_element_type=F32) + carry_ref[...]
    rank1 = jnp.sum(jnp.where(sel1, before, 0.0), axis=-1, keepdims=True)
    rank2 = jnp.sum(jnp.where(sel2, before, 0.0), axis=-1, keepdims=True)
    rank_ref[...] = jnp.concatenate([rank1, rank2], axis=-1).astype(jnp.int32)
    total = carry_ref[...] + jnp.sum(onehot, axis=0, keepdims=True)
    carry_ref[...] = total
    cnt_ref[...] = total.astype(jnp.int32)


def _router(h2, g, wcat):
    t, d = h2.shape
    tm = min(ROUTE_TM, t)
    pair = lambda dt: jax.ShapeDtypeStruct((t, TOP_K), dt)
    return pl.pallas_call(
        _router_kernel,
        grid=(t // tm,),
        in_specs=[
            pl.BlockSpec((tm, d), lambda i: (i, 0)),
            pl.BlockSpec((1, d), lambda i: (0, 0)),
            pl.BlockSpec((d, 2 * N_EXPERTS), lambda i: (0, 0)),
        ],
        out_specs=[
            pl.BlockSpec((tm, TOP_K), lambda i: (i, 0)),
            pl.BlockSpec((tm, TOP_K), lambda i: (i, 0)),
            pl.BlockSpec((tm, TOP_K), lambda i: (i, 0)),
            pl.BlockSpec((1, N_EXPERTS), lambda i: (0, 0)),
        ],
        out_shape=[pair(jnp.int32), pair(F32), pair(jnp.int32),
                   jax.ShapeDtypeStruct((1, N_EXPERTS), jnp.int32)],
        scratch_shapes=[pltpu.VMEM((1, N_EXPERTS), F32)],
        compiler_params=pltpu.CompilerParams(dimension_semantics=("arbitrary",)),
        name="router",
    )(h2, g, wcat)


def _start_row_copies(n_rows, row_copy):
    def body(c, carry):
        for u in range(ISSUE_UNROLL):
            for k in range(TOP_K):
                row_copy(c * ISSUE_UNROLL + u, k).start(priority=k)
        return carry

    lax.fori_loop(0, n_rows // ISSUE_UNROLL, body, 0)


def _dispatch_kernel(zrow_ref, dest_ref, x_ref, g_ref, xs_ref, xn_scr, zero_scr, sem, zsem):
    i = pl.program_id(0)
    tm = x_ref.shape[0]
    slot = i % 2

    @pl.when(i == 0)
    def _():
        zero_scr[...] = jnp.zeros_like(zero_scr)
        zcopies = [
            pltpu.make_async_copy(
                zero_scr,
                xs_ref.at[pl.ds(pl.multiple_of(jnp.maximum(zrow_ref[n], 0) * SUBLANES, MOE_TM),
                                MOE_TM * SUBLANES)],
                zsem)
            for n in range(2 * N_EXPERTS)]
        for n, c in enumerate(zcopies):
            pl.when(zrow_ref[n] >= 0)(c.start)
        for n, c in enumerate(zcopies):
            pl.when(zrow_ref[n] >= 0)(c.wait)

    _store_token_major(xn_scr, (slot,), _rms_norm_rows(x_ref[...], g_ref[...]))

    def row_copy(t, k):
        d = dest_ref[0, 0, TOP_K * t + k]
        return pltpu.make_async_copy(
            xn_scr.at[slot, _token_rows(t)], xs_ref.at[_token_rows(d)], sem.at[slot])

    _start_row_copies(tm, row_copy)

    def wait_slot(s):
        for _ in range(TOP_K):
            pltpu.make_async_copy(
                xn_scr.at[s], xs_ref.at[pl.ds(0, tm * SUBLANES)], sem.at[s]).wait()

    @pl.when(i > 0)
    def _():
        wait_slot(1 - slot)

    @pl.when(i == pl.num_programs(0) - 1)
    def _():
        wait_slot(slot)


def _dispatch(h2, g, dest, zrows, cap):
    t, d = h2.shape
    tm = min(ROW_TM, t)
    dest3 = dest.reshape(t // tm, 1, tm * TOP_K)
    grid_spec = pltpu.PrefetchScalarGridSpec(
        num_scalar_prefetch=1,
        grid=(t // tm,),
        in_specs=[
            pl.BlockSpec((1, 1, tm * TOP_K), lambda i, z: (i, 0, 0), memory_space=pltpu.SMEM),
            pl.BlockSpec((tm, d), lambda i, z: (i, 0)),
            pl.BlockSpec((1, d), lambda i, z: (0, 0)),
        ],
        out_specs=pl.BlockSpec(memory_space=pl.ANY),
        scratch_shapes=[pltpu.VMEM((2, tm * SUBLANES, LANES), F32),
                        pltpu.VMEM((MOE_TM * SUBLANES, LANES), F32),
                        pltpu.SemaphoreType.DMA((2,)), pltpu.SemaphoreType.DMA],
    )
    return pl.pallas_call(
        _dispatch_kernel,
        grid_spec=grid_spec,
        out_shape=jax.ShapeDtypeStruct((cap * SUBLANES, LANES), F32),
        compiler_params=pltpu.CompilerParams(
            dimension_semantics=("arbitrary",), vmem_limit_bytes=VMEM_LIMIT),
        name="dispatch",
    )(zrows, dest3, h2, g)


def _combine_kernel(pos_ref, posn_ref, h_ref, gate_ref, fg_ref, ys_ref, o_ref, buf, sem,
                    *, final_norm):
    i = pl.program_id(0)
    n = pl.num_programs(0)
    tm = h_ref.shape[0]
    slot = i % 2

    def start_gather(p_ref, s):
        def row_copy(t, k):
            p = p_ref[0, 0, TOP_K * t + k]
            return pltpu.make_async_copy(
                ys_ref.at[_token_rows(p)], buf.at[s, k, _token_rows(t)], sem.at[s])
        _start_row_copies(tm, row_copy)

    @pl.when(i == 0)
    def _():
        start_gather(pos_ref, slot)

    @pl.when(i + 1 < n)
    def _():
        start_gather(posn_ref, 1 - slot)

    for k in range(TOP_K):
        pltpu.make_async_copy(
            ys_ref.at[pl.ds(0, tm * SUBLANES)], buf.at[slot, k], sem.at[slot]).wait()
    gate = gate_ref[...]
    d = h_ref.shape[1]
    out = (h_ref[...] + gate[:, 0:1] * _load_token_major(buf, (slot, 0), tm, d)
           + gate[:, 1:2] * _load_token_major(buf, (slot, 1), tm, d))
    if final_norm:
        out = _rms_norm_rows(out, fg_ref[...])
    o_ref[...] = out


def _combine(h2, gates, pos, ys, final_g, final_norm):
    t, d = h2.shape
    tm = min(ROW_TM, t)
    nt = t // tm
    pos3 = pos.reshape(nt, 1, tm * TOP_K)
    return pl.pallas_call(
        functools.partial(_combine_kernel, final_norm=final_norm),
        grid=(nt,),
        in_specs=[
            pl.BlockSpec((1, 1, tm * TOP_K), lambda i: (i, 0, 0), memory_space=pltpu.SMEM),
            pl.BlockSpec((1, 1, tm * TOP_K), lambda i: (jnp.minimum(i + 1, nt - 1), 0, 0),
                         memory_space=pltpu.SMEM),
            pl.BlockSpec((tm, d), lambda i: (i, 0)),
            pl.BlockSpec((tm, TOP_K), lambda i: (i, 0)),
            pl.BlockSpec((1, d), lambda i: (0, 0)),
            pl.BlockSpec(memory_space=pl.ANY),
        ],
        out_specs=pl.BlockSpec((tm, d), lambda i: (i, 0)),
        out_shape=jax.ShapeDtypeStruct((t, d), F32),
        scratch_shapes=[pltpu.VMEM((2, TOP_K, tm * SUBLANES, LANES), F32),
                        pltpu.SemaphoreType.DMA((2,))],
        compiler_params=pltpu.CompilerParams(
            dimension_semantics=("arbitrary",), vmem_limit_bytes=VMEM_LIMIT),
        name="combine",
    )(pos3, pos3, h2, gates, final_g, ys)


def _moe(h2, g, wcat, w1, w3, w2, j, final_g, final_norm):
    t, d = h2.shape
    eid, gates, rank, counts = _router(h2, g, wcat)
    n_assign = t * TOP_K
    assert n_assign % MOE_TM == 0
    n_blocks = n_assign // MOE_TM + N_EXPERTS
    cap = n_blocks * MOE_TM
    counts = counts[0]
    padded = (counts + MOE_TM - 1) // MOE_TM * MOE_TM
    pad_end = jnp.cumsum(padded)
    pad_start = pad_end - padded
    dest = pad_start[eid] + rank
    block_start = jnp.arange(n_blocks, dtype=jnp.int32) * MOE_TM
    block_expert = jnp.minimum(
        jnp.searchsorted(pad_end, block_start, side="right"), N_EXPERTS - 1).astype(jnp.int32)
    n_used = (pad_end[-1:] // MOE_TM).astype(jnp.int32)
    tail_rows = pad_end[-1] + jnp.arange(N_EXPERTS, dtype=jnp.int32) * MOE_TM
    zrows = jnp.concatenate([
        jnp.where(padded > 0, pad_end - MOE_TM, -1),
        jnp.where(tail_rows < cap, tail_rows, -1),
    ]).astype(jnp.int32)
    xs = _dispatch(h2, g, dest, zrows, cap)
    ys = _expert_ffn(xs, block_expert, n_used, w1, w3, w2, j)
    return _combine(h2, gates, dest, ys, final_g, final_norm)


def _gate_blocks(w_r, w_i):
    nh, hd, _ = w_r.shape
    per = MXU_DIM // hd
    eye = jnp.eye(per, dtype=w_r.dtype)

    def blocks(w):
        wb = w.reshape(nh // per, per, hd, hd)
        return jnp.einsum("cpde,pq->cpdqe", wb, eye).reshape(nh // per, MXU_DIM, MXU_DIM)

    return jnp.concatenate([blocks(w_r), blocks(w_i)], axis=-1)


def _group_matrices(d_mix):
    grp = jnp.arange(d_mix) // HEAD_DIM
    cols = jnp.arange(LANES)
    g1 = jnp.where(grp[:, None] == cols[None, :], 1.0 / HEAD_DIM, 0.0).astype(BF16)
    e = jnp.where(cols[:, None] == grp[None, :], 1.0, 0.0).astype(BF16)
    return g1, jnp.concatenate([e, e], axis=0)


def kernel(x, norm_mix_g, w_in, conv_a_w, conv_a_b, conv_b_w, conv_b_b, lru_w_r, lru_b_r,
           lru_w_i, lru_b_i, lru_lambda, mix_out_g, w_out, norm_ffn_g, ffn_w1, ffn_w3, ffn_w2,
           router_w, expert_w1, expert_w3, expert_w2, final_g):
    b, s, d = x.shape
    depth = w_in.shape[0]
    assert depth % 2 == 0
    g1, e2 = _group_matrices(D_CONV + D_LRU)
    row = lambda v: v.reshape(1, -1)
    h = x
    for layer in range(depth):
        p = {
            "norm_mix_g": row(norm_mix_g[layer]),
            "w_in": w_in[layer].astype(BF16),
            "conv_a_w": conv_a_w[layer], "conv_a_b": row(conv_a_b[layer]),
            "conv_b_w": conv_b_w[layer], "conv_b_b": row(conv_b_b[layer]),
            "w_gate": _gate_blocks(lru_w_r[layer], lru_w_i[layer]).astype(BF16),
            "b_r": row(lru_b_r[layer]), "b_i": row(lru_b_i[layer]), "lam": row(lru_lambda[layer]),
            "g_mix": row(mix_out_g[layer]), "g1": g1, "e2": e2,
            "w_out": w_out[layer].astype(BF16),
        }
        h = _mixer(h, p)
        h2 = h.reshape(b * s, d)
        gf = row(norm_ffn_g[layer])
        j = layer // 2
        if layer % 2 == 0:
            h2 = _dense_ffn(h2, gf, ffn_w1, ffn_w3, ffn_w2, j)
        else:
            w_hi = router_w[j].astype(BF16)
            w_lo = (router_w[j] - w_hi.astype(F32)).astype(BF16)
            h2 = _moe(h2, gf, jnp.concatenate([w_hi, w_lo], axis=1),
                      expert_w1, expert_w3, expert_w2, j, row(final_g), layer == depth - 1)
        h = h2.reshape(b, s, d)
    return h
```

```python
import functools

import jax
import jax.numpy as jnp
from jax import lax
from jax.experimental import pallas as pl
from jax.experimental.pallas import tpu as pltpu

EPS = 1e-6
HEAD_DIM = 64
D_CONV = 512
D_LRU = 512
LRU_C = 8.0
N_EXPERTS = 8
TOP_K = 2

SUBLANES = 8
LANES = 128
MXU_DIM = 256

MIX_TS = 512
FFN_TM = 1024
FFN_TF = 512
MOE_TM = 1024
ROUTE_TM = 512
ROW_TM = 512
ISSUE_UNROLL = 8
VMEM_LIMIT = 56 * 1024 * 1024

F32 = jnp.float32
BF16 = jnp.bfloat16


def _rms_norm_rows(x, g):
    return x * lax.rsqrt(jnp.mean(x * x, axis=-1, keepdims=True) + EPS) * g


def _sigmoid(x):
    return 1.0 / (1.0 + jnp.exp(-x))


def _gelu_tanh(x):
    return 0.5 * x * (1.0 + jnp.tanh(0.7978845608028654 * (x + 0.044715 * (x * x * x))))


def _phases(v, g):
    return [v[r * g:(r + 1) * g] for r in range(SUBLANES)]


def _group_shift(v, first_row):
    rolled = pltpu.roll(v, 1, 0)
    rows = lax.broadcasted_iota(jnp.int32, (SUBLANES, v.shape[1]), 0)
    head = jnp.where(rows == 0, jnp.broadcast_to(first_row, (SUBLANES, v.shape[1])),
                     rolled[0:SUBLANES])
    return jnp.concatenate([head, rolled[SUBLANES:]], axis=0)


def _causal_conv(phases, tail, w_ref, b_ref):
    taps = w_ref.shape[0]
    bias = b_ref[...]
    shifted = {}

    def phase(idx):
        if idx >= 0:
            return phases[idx]
        if idx not in shifted:
            src = idx + SUBLANES
            shifted[idx] = _group_shift(phases[src], tail[src:src + 1])
        return shifted[idx]

    out = []
    for r in range(SUBLANES):
        acc = w_ref[taps - 1:taps, :] * phases[r] + bias
        for k in range(taps - 1):
            acc = acc + w_ref[k:k + 1, :] * phase(r - (taps - 1) + k)
        out.append(acc)
    return out


def _last_group_rows(phases):
    return jnp.concatenate([p[p.shape[0] - 1:p.shape[0]] for p in phases], axis=0)


def _mixer_kernel(h_ref, gn_ref, win_ref, caw_ref, cab_ref, cbw_ref, cbb_ref, wg_ref,
                  br_ref, bi_ref, lam_ref, gmix_ref, g1_ref, e2_ref, wout_ref,
                  o_ref, taila_ref, tailb_ref, hstate_ref, perm_scr):
    ts = h_ref.shape[0]
    grp = ts // SUBLANES

    @pl.when(pl.program_id(1) == 0)
    def _():
        taila_ref[...] = jnp.zeros_like(taila_ref)
        tailb_ref[...] = jnp.zeros_like(tailb_ref)
        hstate_ref[...] = jnp.zeros_like(hstate_ref)

    n_chunks = h_ref.shape[1] // LANES
    for c in range(n_chunks):
        perm_scr[c] = h_ref[:, c * LANES:(c + 1) * LANES]
    x = jnp.concatenate(
        [jnp.concatenate([perm_scr[c, pl.ds(r, grp, stride=SUBLANES), :]
                          for c in range(n_chunks)], axis=-1)
         for r in range(SUBLANES)], axis=0)
    xn = _rms_norm_rows(x, gn_ref[...])
    z = jnp.dot(xn.astype(BF16), win_ref[...], preferred_element_type=F32)
    xa = z[:, 0:D_CONV]
    gb = z[:, D_CONV:2 * D_CONV]
    gc = z[:, 2 * D_CONV:3 * D_CONV]
    xr = z[:, 3 * D_CONV:3 * D_CONV + D_LRU]
    gr = z[:, 3 * D_CONV + D_LRU:]

    ua = _phases(gc * xa, grp)
    ca = jnp.concatenate(_causal_conv(ua, taila_ref[...], caw_ref, cab_ref), axis=0)
    taila_ref[...] = _last_group_rows(ua)
    ya = gb * ca

    xrp = _phases(xr, grp)
    xc = jnp.concatenate(_causal_conv(xrp, tailb_ref[...], cbw_ref, cbb_ref), axis=0)
    tailb_ref[...] = _last_group_rows(xrp)

    xcb = xc.astype(BF16)
    gz = [jnp.dot(xcb[:, c * MXU_DIM:(c + 1) * MXU_DIM], wg_ref[c], preferred_element_type=F32)
          for c in range(D_LRU // MXU_DIM)]
    gz_r = jnp.concatenate([g[:, 0:MXU_DIM] for g in gz], axis=-1)
    gz_i = jnp.concatenate([g[:, MXU_DIM:] for g in gz], axis=-1)
    r = _sigmoid(gz_r + br_ref[...])
    i = _sigmoid(gz_i + bi_ref[...])
    nl = -lam_ref[...]
    softplus = jnp.maximum(nl, 0.0) + jnp.log1p(jnp.exp(-jnp.abs(nl)))
    log_a = (-LRU_C) * r * softplus
    a = jnp.exp(log_a)
    u = jnp.sqrt(1.0 - a * a) * (i * xc)

    ap = _phases(a, grp)
    up = _phases(u, grp)
    hloc = [up[0]]
    ploc = [ap[0]]
    for ph in range(1, SUBLANES):
        hloc.append(ap[ph] * hloc[ph - 1] + up[ph])
        ploc.append(ap[ph] * ploc[ph - 1])
    ps = ploc[SUBLANES - 1]
    hs = hloc[SUBLANES - 1]
    rows = lax.broadcasted_iota(jnp.int32, ps.shape, 0)
    d = 1
    while d < grp:
        m = rows >= d
        hs = jnp.where(m, ps * pltpu.roll(hs, d, 0) + hs, hs)
        ps = jnp.where(m, ps * pltpu.roll(ps, d, 0), ps)
        d *= 2
    c0 = hstate_ref[0:1, :]
    after = ps * c0 + hs
    cin = _group_shift(after, c0)
    hstate_ref[...] = jnp.broadcast_to(after[grp - 1:grp], hstate_ref.shape)
    hh = jnp.concatenate([hloc[ph] + ploc[ph] * cin for ph in range(SUBLANES)], axis=0)
    yr = hh * _gelu_tanh(gr)

    y = jnp.concatenate([ya, yr], axis=-1)
    gm = jnp.dot((y * y).astype(BF16), g1_ref[...], preferred_element_type=F32)
    rs = lax.rsqrt(gm + EPS)
    rs_hi = rs.astype(BF16)
    rs_lo = (rs - rs_hi.astype(F32)).astype(BF16)
    rs_full = jnp.dot(jnp.concatenate([rs_hi, rs_lo], axis=-1), e2_ref[...],
                      preferred_element_type=F32)
    yn = y * rs_full * gmix_ref[...]
    out = x + jnp.dot(yn.astype(BF16), wout_ref[...], preferred_element_type=F32)
    for c in range(n_chunks):
        for ph in range(SUBLANES):
            perm_scr[c, pl.ds(ph, grp, stride=SUBLANES), :] = (
                out[ph * grp:(ph + 1) * grp, c * LANES:(c + 1) * LANES])
    o_ref[...] = jnp.concatenate([perm_scr[c] for c in range(n_chunks)], axis=-1)


def _mixer(h, p):
    b, s, d = h.shape
    ts = min(MIX_TS, s)
    assert s % ts == 0 and ts % (SUBLANES * SUBLANES) == 0
    d_in = p["w_in"].shape[1]
    d_mix = D_CONV + D_LRU
    const = lambda shape: pl.BlockSpec(shape, lambda bi, j: (0,) * len(shape))
    return pl.pallas_call(
        _mixer_kernel,
        grid=(b, s // ts),
        in_specs=[
            pl.BlockSpec((None, ts, d), lambda bi, j: (bi, j, 0)),
            const((1, d)), const((d, d_in)),
            const((3, D_CONV)), const((1, D_CONV)), const((4, D_LRU)), const((1, D_LRU)),
            const((D_LRU // MXU_DIM, MXU_DIM, 2 * MXU_DIM)),
            const((1, D_LRU)), const((1, D_LRU)), const((1, D_LRU)),
            const((1, d_mix)), const((d_mix, LANES)), const((2 * LANES, d_mix)), const((d_mix, d)),
        ],
        out_specs=pl.BlockSpec((None, ts, d), lambda bi, j: (bi, j, 0)),
        out_shape=jax.ShapeDtypeStruct((b, s, d), F32),
        scratch_shapes=[
            pltpu.VMEM((SUBLANES, D_CONV), F32), pltpu.VMEM((SUBLANES, D_LRU), F32),
            pltpu.VMEM((SUBLANES, D_LRU), F32),
            pltpu.VMEM((d // LANES, ts, LANES), F32),
        ],
        compiler_params=pltpu.CompilerParams(
            dimension_semantics=("arbitrary", "arbitrary"), vmem_limit_bytes=VMEM_LIMIT),
        name="mixer",
    )(h, p["norm_mix_g"], p["w_in"], p["conv_a_w"], p["conv_a_b"], p["conv_b_w"], p["conv_b_b"],
      p["w_gate"], p["b_r"], p["b_i"], p["lam"], p["g_mix"], p["g1"], p["e2"], p["w_out"])


def _swiglu_chunk(xn, w1_ref, w3_ref, w2_ref):
    h1 = jnp.dot(xn, w1_ref[...].astype(BF16), preferred_element_type=F32)
    h3 = jnp.dot(xn, w3_ref[...].astype(BF16), preferred_element_type=F32)
    g = (h1 * _sigmoid(h1)) * h3
    return jnp.dot(g.astype(BF16), w2_ref[...].astype(BF16), preferred_element_type=F32)


def _swiglu_steps(f, nf, load_xn, xn_scr, acc_scr, w_refs, finish):
    assert nf >= 2

    @pl.when(f == 0)
    def _():
        xn = load_xn()
        xn_scr[...] = xn
        acc_scr[...] = _swiglu_chunk(xn, *w_refs)

    @pl.when(jnp.logical_and(f > 0, f < nf - 1))
    def _():
        acc_scr[...] += _swiglu_chunk(xn_scr[...], *w_refs)

    @pl.when(f == nf - 1)
    def _():
        finish(acc_scr[...] + _swiglu_chunk(xn_scr[...], *w_refs))


def _dense_ffn_kernel(x_ref, g_ref, w1_ref, w3_ref, w2_ref, o_ref, xn_scr, acc_scr, *, nf):
    def load_xn():
        return _rms_norm_rows(x_ref[...], g_ref[...]).astype(BF16)

    def finish(y):
        o_ref[...] = x_ref[...] + y

    _swiglu_steps(pl.program_id(1), nf, load_xn, xn_scr, acc_scr,
                  (w1_ref, w3_ref, w2_ref), finish)


def _dense_ffn(h2, g, w1, w3, w2, j):
    t, d = h2.shape
    ff = w1.shape[2]
    tm = min(FFN_TM, t)
    tf = FFN_TF
    return pl.pallas_call(
        functools.partial(_dense_ffn_kernel, nf=ff // tf),
        grid=(t // tm, ff // tf),
        in_specs=[
            pl.BlockSpec((tm, d), lambda i, f: (i, 0)),
            pl.BlockSpec((1, d), lambda i, f: (0, 0)),
            pl.BlockSpec((None, d, tf), lambda i, f: (j, 0, f)),
            pl.BlockSpec((None, d, tf), lambda i, f: (j, 0, f)),
            pl.BlockSpec((None, tf, d), lambda i, f: (j, f, 0)),
        ],
        out_specs=pl.BlockSpec((tm, d), lambda i, f: (i, 0)),
        out_shape=jax.ShapeDtypeStruct((t, d), F32),
        scratch_shapes=[pltpu.VMEM((tm, d), BF16), pltpu.VMEM((tm, d), F32)],
        compiler_params=pltpu.CompilerParams(
            dimension_semantics=("arbitrary", "arbitrary"), vmem_limit_bytes=VMEM_LIMIT),
        name="dense_ffn",
    )(h2, g, w1, w3, w2)


def _token_major_index(idx, c, n_tokens):
    return idx + (pl.ds(c, n_tokens, stride=SUBLANES), slice(None))


def _load_token_major(ref, idx, n_tokens, d):
    assert d == SUBLANES * LANES
    return jnp.concatenate(
        [ref[_token_major_index(idx, c, n_tokens)] for c in range(SUBLANES)], axis=-1)


def _store_token_major(ref, idx, value):
    n_tokens, d = value.shape
    assert d == SUBLANES * LANES
    for c in range(SUBLANES):
        ref[_token_major_index(idx, c, n_tokens)] = value[:, c * LANES:(c + 1) * LANES]


def _token_rows(t):
    return pl.ds(pl.multiple_of(t * SUBLANES, SUBLANES), SUBLANES)


def _expert_ffn_kernel(be_ref, nused_ref, x_ref, w1_ref, w3_ref, w2_ref, o_ref, xn_scr, acc_scr,
                       *, nf):
    i = pl.program_id(0)
    f = pl.program_id(1)
    used = i < nused_ref[0]
    tm, d = xn_scr.shape

    def load_xn():
        return _load_token_major(x_ref, (), tm, d).astype(BF16)

    def finish(y):
        _store_token_major(o_ref, (), y)

    @pl.when(used)
    def _():
        _swiglu_steps(f, nf, load_xn, xn_scr, acc_scr, (w1_ref, w3_ref, w2_ref), finish)

    @pl.when(jnp.logical_and(jnp.logical_not(used), f == nf - 1))
    def _():
        o_ref[...] = jnp.zeros_like(o_ref)


def _expert_ffn(xs, block_expert, n_used, w1, w3, w2, j):
    d = w1.shape[2]
    cap = xs.shape[0] // SUBLANES
    ff = w1.shape[3]
    tm = MOE_TM
    tf = FFN_TF
    grid_spec = pltpu.PrefetchScalarGridSpec(
        num_scalar_prefetch=2,
        grid=(cap // tm, ff // tf),
        in_specs=[
            pl.BlockSpec((tm * SUBLANES, LANES), lambda i, f, be, nu: (i, 0)),
            pl.BlockSpec((None, None, d, tf), lambda i, f, be, nu: (j, be[i], 0, f)),
            pl.BlockSpec((None, None, d, tf), lambda i, f, be, nu: (j, be[i], 0, f)),
            pl.BlockSpec((None, None, tf, d), lambda i, f, be, nu: (j, be[i], f, 0)),
        ],
        out_specs=pl.BlockSpec((tm * SUBLANES, LANES), lambda i, f, be, nu: (i, 0)),
        scratch_shapes=[pltpu.VMEM((tm, d), BF16), pltpu.VMEM((tm, d), F32)],
    )
    return pl.pallas_call(
        functools.partial(_expert_ffn_kernel, nf=ff // tf),
        grid_spec=grid_spec,
        out_shape=jax.ShapeDtypeStruct((cap * SUBLANES, LANES), F32),
        compiler_params=pltpu.CompilerParams(
            dimension_semantics=("arbitrary", "arbitrary"), vmem_limit_bytes=VMEM_LIMIT),
        name="expert_ffn",
    )(block_expert, n_used, xs, w1, w3, w2)


def _router_kernel(x_ref, g_ref, wcat_ref, eid_ref, gate_ref, rank_ref, cnt_ref, carry_ref):
    tm = x_ref.shape[0]

    @pl.when(pl.program_id(0) == 0)
    def _():
        carry_ref[...] = jnp.zeros_like(carry_ref)

    xn = _rms_norm_rows(x_ref[...], g_ref[...])
    x_hi = xn.astype(BF16)
    x_lo = (xn - x_hi.astype(F32)).astype(BF16)
    nt_dims = (((1,), (1,)), ((), ()))
    wcat = wcat_ref[...]
    l_hi = lax.dot_general(wcat, x_hi, nt_dims, preferred_element_type=F32)
    l_lo = lax.dot_general(wcat, x_lo, nt_dims, preferred_element_type=F32)
    logits = l_hi[0:N_EXPERTS] + l_hi[N_EXPERTS:] + l_lo[0:N_EXPERTS]

    sub = lax.broadcasted_iota(jnp.int32, logits.shape, 0)
    m1 = jnp.max(logits, axis=0, keepdims=True)
    i1 = jnp.min(jnp.where(logits == m1, sub, N_EXPERTS), axis=0, keepdims=True)
    rest = jnp.where(sub == i1, -jnp.inf, logits)
    m2 = jnp.max(rest, axis=0, keepdims=True)
    i2 = jnp.min(jnp.where(rest == m2, sub, N_EXPERTS), axis=0, keepdims=True)
    e2 = jnp.exp(m2 - m1)
    denom = 1.0 + e2
    gate_ref[...] = jnp.concatenate([1.0 / denom, e2 / denom], axis=0)
    eid_ref[...] = jnp.concatenate([i1, i2], axis=0)

    sel1 = sub == i1
    sel2 = sub == i2
    onehot = jnp.where(jnp.logical_or(sel1, sel2), 1.0, 0.0)
    r_io = lax.broadcasted_iota(jnp.int32, (tm, tm), 0)
    c_io = lax.broadcasted_iota(jnp.int32, (tm, tm), 1)
    tri = jnp.where(r_io < c_io, 1.0, 0.0).astype(BF16)
    before = jnp.dot(onehot.astype(BF16), tri, preferred_element_type=F32) + carry_ref[...]
    rank1 = jnp.sum(jnp.where(sel1, before, 0.0), axis=0, keepdims=True)
    rank2 = jnp.sum(jnp.where(sel2, before, 0.0), axis=0, keepdims=True)
    rank_ref[...] = jnp.concatenate([rank1, rank2], axis=0).astype(jnp.int32)
    total = carry_ref[...] + jnp.sum(onehot, axis=1, keepdims=True)
    carry_ref[...] = total
    cnt_ref[...] = total.astype(jnp.int32)


def _router(h2, g, wcat):
    t, d = h2.shape
    tm = min(ROUTE_TM, t)
    pair = lambda dt: jax.ShapeDtypeStruct((TOP_K, t), dt)
    return pl.pallas_call(
        _router_kernel,
        grid=(t // tm,),
        in_specs=[
            pl.BlockSpec((tm, d), lambda i: (i, 0)),
            pl.BlockSpec((1, d), lambda i: (0, 0)),
            pl.BlockSpec((2 * N_EXPERTS, d), lambda i: (0, 0)),
        ],
        out_specs=[
            pl.BlockSpec((TOP_K, tm), lambda i: (0, i)),
            pl.BlockSpec((TOP_K, tm), lambda i: (0, i)),
            pl.BlockSpec((TOP_K, tm), lambda i: (0, i)),
            pl.BlockSpec((N_EXPERTS, 1), lambda i: (0, 0)),
        ],
        out_shape=[pair(jnp.int32), pair(F32), pair(jnp.int32),
                   jax.ShapeDtypeStruct((N_EXPERTS, 1), jnp.int32)],
        scratch_shapes=[pltpu.VMEM((N_EXPERTS, 1), F32)],
        compiler_params=pltpu.CompilerParams(dimension_semantics=("arbitrary",)),
        name="router",
    )(h2, g, wcat)


def _tile_indices(idx, tm):
    k, t = idx.shape
    return idx.reshape(k, t // tm, tm).transpose(1, 0, 2)


def _start_row_copies(n_rows, row_copy):
    def body(c, carry):
        for u in range(ISSUE_UNROLL):
            for k in range(TOP_K):
                row_copy(c * ISSUE_UNROLL + u, k).start(priority=k)
        return carry

    lax.fori_loop(0, n_rows // ISSUE_UNROLL, body, 0)


def _dispatch_kernel(zrow_ref, dest_ref, x_ref, g_ref, xs_ref, xn_scr, zero_scr, sem, zsem):
    i = pl.program_id(0)
    tm = x_ref.shape[0]
    slot = i % 2

    @pl.when(i == 0)
    def _():
        zero_scr[...] = jnp.zeros_like(zero_scr)
        zcopies = [
            pltpu.make_async_copy(
                zero_scr,
                xs_ref.at[pl.ds(pl.multiple_of(jnp.maximum(zrow_ref[n], 0) * SUBLANES, MOE_TM),
                                MOE_TM * SUBLANES)],
                zsem)
            for n in range(2 * N_EXPERTS)]
        for n, c in enumerate(zcopies):
            pl.when(zrow_ref[n] >= 0)(c.start)
        for n, c in enumerate(zcopies):
            pl.when(zrow_ref[n] >= 0)(c.wait)

    _store_token_major(xn_scr, (slot,), _rms_norm_rows(x_ref[...], g_ref[...]))

    def row_copy(t, k):
        d = dest_ref[0, k, t]
        return pltpu.make_async_copy(
            xn_scr.at[slot, _token_rows(t)], xs_ref.at[_token_rows(d)], sem.at[slot])

    _start_row_copies(tm, row_copy)

    def wait_slot(s):
        for _ in range(TOP_K):
            pltpu.make_async_copy(
                xn_scr.at[s], xs_ref.at[pl.ds(0, tm * SUBLANES)], sem.at[s]).wait()

    @pl.when(i > 0)
    def _():
        wait_slot(1 - slot)

    @pl.when(i == pl.num_programs(0) - 1)
    def _():
        wait_slot(slot)


def _dispatch(h2, g, dest, zrows, cap):
    t, d = h2.shape
    tm = min(ROW_TM, t)
    dest3 = _tile_indices(dest, tm)
    grid_spec = pltpu.PrefetchScalarGridSpec(
        num_scalar_prefetch=1,
        grid=(t // tm,),
        in_specs=[
            pl.BlockSpec((1, TOP_K, tm), lambda i, z: (i, 0, 0), memory_space=pltpu.SMEM),
            pl.BlockSpec((tm, d), lambda i, z: (i, 0)),
            pl.BlockSpec((1, d), lambda i, z: (0, 0)),
        ],
        out_specs=pl.BlockSpec(memory_space=pl.ANY),
        scratch_shapes=[pltpu.VMEM((2, tm * SUBLANES, LANES), F32),
                        pltpu.VMEM((MOE_TM * SUBLANES, LANES), F32),
                        pltpu.SemaphoreType.DMA((2,)), pltpu.SemaphoreType.DMA],
    )
    return pl.pallas_call(
        _dispatch_kernel,
        grid_spec=grid_spec,
        out_shape=jax.ShapeDtypeStruct((cap * SUBLANES, LANES), F32),
        compiler_params=pltpu.CompilerParams(
            dimension_semantics=("arbitrary",), vmem_limit_bytes=VMEM_LIMIT),
        name="dispatch",
    )(zrows, dest3, h2, g)


def _combine_kernel(pos_ref, posn_ref, h_ref, gate_ref, fg_ref, ys_ref, o_ref, buf, sem,
                    *, final_norm):
    i = pl.program_id(0)
    n = pl.num_programs(0)
    tm = h_ref.shape[0]
    slot = i % 2

    def start_gather(p_ref, s):
        def row_copy(t, k):
            p = p_ref[0, k, t]
            return pltpu.make_async_copy(
                ys_ref.at[_token_rows(p)], buf.at[s, k, _token_rows(t)], sem.at[s])
        _start_row_copies(tm, row_copy)

    @pl.when(i == 0)
    def _():
        start_gather(pos_ref, slot)

    @pl.when(i + 1 < n)
    def _():
        start_gather(posn_ref, 1 - slot)

    for k in range(TOP_K):
        pltpu.make_async_copy(
            ys_ref.at[pl.ds(0, tm * SUBLANES)], buf.at[slot, k], sem.at[slot]).wait()
    gate = gate_ref[...]
    d = h_ref.shape[1]
    out = (h_ref[...] + gate[:, 0:1] * _load_token_major(buf, (slot, 0), tm, d)
           + gate[:, 1:2] * _load_token_major(buf, (slot, 1), tm, d))
    if final_norm:
        out = _rms_norm_rows(out, fg_ref[...])
    o_ref[...] = out


def _combine(h2, gates, pos, ys, final_g, final_norm):
    t, d = h2.shape
    tm = min(ROW_TM, t)
    nt = t // tm
    pos3 = _tile_indices(pos, tm)
    return pl.pallas_call(
        functools.partial(_combine_kernel, final_norm=final_norm),
        grid=(nt,),
        in_specs=[
            pl.BlockSpec((1, TOP_K, tm), lambda i: (i, 0, 0), memory_space=pltpu.SMEM),
            pl.BlockSpec((1, TOP_K, tm), lambda i: (jnp.minimum(i + 1, nt - 1), 0, 0),
                         memory_space=pltpu.SMEM),
            pl.BlockSpec((tm, d), lambda i: (i, 0)),
            pl.BlockSpec((tm, TOP_K), lambda i: (i, 0)),
            pl.BlockSpec((1, d), lambda i: (0, 0)),
            pl.BlockSpec(memory_space=pl.ANY),
        ],
        out_specs=pl.BlockSpec((tm, d), lambda i: (i, 0)),
        out_shape=jax.ShapeDtypeStruct((t, d), F32),
        scratch_shapes=[pltpu.VMEM((2, TOP_K, tm * SUBLANES, LANES), F32),
                        pltpu.SemaphoreType.DMA((2,))],
        compiler_params=pltpu.CompilerParams(
            dimension_semantics=("arbitrary",), vmem_limit_bytes=VMEM_LIMIT),
        name="combine",
    )(pos3, pos3, h2, gates, final_g, ys)


def _moe(h2, g, wcat, w1, w3, w2, j, final_g, final_norm):
    t, d = h2.shape
    eid, gates, rank, counts = _router(h2, g, wcat)
    n_assign = t * TOP_K
    assert n_assign % MOE_TM == 0
    n_blocks = n_assign // MOE_TM + N_EXPERTS
    cap = n_blocks * MOE_TM
    counts = counts[:, 0]
    padded = (counts + MOE_TM - 1) // MOE_TM * MOE_TM
    pad_end = jnp.cumsum(padded)
    pad_start = pad_end - padded
    experts = jnp.arange(N_EXPERTS, dtype=jnp.int32)[:, None, None]
    dest = rank + jnp.sum(jnp.where(eid[None] == experts, pad_start[:, None, None], 0), axis=0)
    block_start = jnp.arange(n_blocks, dtype=jnp.int32) * MOE_TM
    block_expert = jnp.minimum(
        jnp.sum(block_start[:, None] >= pad_end[None, :], axis=1), N_EXPERTS - 1
    ).astype(jnp.int32)
    n_used = (pad_end[-1:] // MOE_TM).astype(jnp.int32)
    tail_rows = pad_end[-1] + jnp.arange(N_EXPERTS, dtype=jnp.int32) * MOE_TM
    zrows = jnp.concatenate([
        jnp.where(padded > 0, pad_end - MOE_TM, -1),
        jnp.where(tail_rows < cap, tail_rows, -1),
    ]).astype(jnp.int32)
    xs = _dispatch(h2, g, dest, zrows, cap)
    ys = _expert_ffn(xs, block_expert, n_used, w1, w3, w2, j)
    return _combine(h2, gates.T, dest, ys, final_g, final_norm)


def _gate_blocks(w_r, w_i):
    nh, hd, _ = w_r.shape
    per = MXU_DIM // hd
    eye = jnp.eye(per, dtype=w_r.dtype)

    def blocks(w):
        wb = w.reshape(nh // per, per, hd, hd)
        return jnp.einsum("cpde,pq->cpdqe", wb, eye).reshape(nh // per, MXU_DIM, MXU_DIM)

    return jnp.concatenate([blocks(w_r), blocks(w_i)], axis=-1)


def _group_matrices(d_mix):
    grp = jnp.arange(d_mix) // HEAD_DIM
    cols = jnp.arange(LANES)
    g1 = jnp.where(grp[:, None] == cols[None, :], 1.0 / HEAD_DIM, 0.0).astype(BF16)
    e = jnp.where(cols[:, None] == grp[None, :], 1.0, 0.0).astype(BF16)
    return g1, jnp.concatenate([e, e], axis=0)


def kernel(x, norm_mix_g, w_in, conv_a_w, conv_a_b, conv_b_w, conv_b_b, lru_w_r, lru_b_r,
           lru_w_i, lru_b_i, lru_lambda, mix_out_g, w_out, norm_ffn_g, ffn_w1, ffn_w3, ffn_w2,
           router_w, expert_w1, expert_w3, expert_w2, final_g):
    b, s, d = x.shape
    depth = w_in.shape[0]
    assert depth % 2 == 0
    g1, e2 = _group_matrices(D_CONV + D_LRU)
    row = lambda v: v.reshape(1, -1)
    h = x
    for layer in range(depth):
        p = {
            "norm_mix_g": row(norm_mix_g[layer]),
            "w_in": w_in[layer].astype(BF16),
            "conv_a_w": conv_a_w[layer], "conv_a_b": row(conv_a_b[layer]),
            "conv_b_w": conv_b_w[layer], "conv_b_b": row(conv_b_b[layer]),
            "w_gate": _gate_blocks(lru_w_r[layer], lru_w_i[layer]).astype(BF16),
            "b_r": row(lru_b_r[layer]), "b_i": row(lru_b_i[layer]), "lam": row(lru_lambda[layer]),
            "g_mix": row(mix_out_g[layer]), "g1": g1, "e2": e2,
            "w_out": w_out[layer].astype(BF16),
        }
        h = _mixer(h, p)
        h2 = h.reshape(b * s, d)
        gf = row(norm_ffn_g[layer])
        j = layer // 2
        if layer % 2 == 0:
            h2 = _dense_ffn(h2, gf, ffn_w1, ffn_w3, ffn_w2, j)
        else:
            w_hi = router_w[j].astype(BF16)
            w_lo = (router_w[j] - w_hi.astype(F32)).astype(BF16)
            h2 = _moe(h2, gf, jnp.concatenate([w_hi.T, w_lo.T], axis=0),
                      expert_w1, expert_w3, expert_w2, j, row(final_g), layer == depth - 1)
        h = h2.reshape(b, s, d)
    return h
```

```python
import functools

import jax
import jax.numpy as jnp
from jax import lax
from jax.experimental import pallas as pl
from jax.experimental.pallas import tpu as pltpu

EPS = 1e-6
HEAD_DIM = 64
D_CONV = 512
D_LRU = 512
LRU_C = 8.0
N_EXPERTS = 8
TOP_K = 2

SUBLANES = 8
LANES = 128
MXU_DIM = 256

MIX_TS = 512
FFN_TM = 1024
FFN_TF = 512
MOE_TM = 768
ROUTE_TM = 512
ROW_TM = 512
ISSUE_UNROLL = 8
VMEM_LIMIT = 56 * 1024 * 1024

F32 = jnp.float32
BF16 = jnp.bfloat16


def _rms_norm_rows(x, g):
    return x * lax.rsqrt(jnp.mean(x * x, axis=-1, keepdims=True) + EPS) * g


def _sigmoid(x):
    return 1.0 / (1.0 + jnp.exp(-x))


def _gelu_tanh(x):
    return 0.5 * x * (1.0 + jnp.tanh(0.7978845608028654 * (x + 0.044715 * (x * x * x))))


def _phases(v, g):
    return [v[r * g:(r + 1) * g] for r in range(SUBLANES)]


def _group_shift(v, first_row):
    rolled = pltpu.roll(v, 1, 0)
    rows = lax.broadcasted_iota(jnp.int32, (SUBLANES, v.shape[1]), 0)
    head = jnp.where(rows == 0, jnp.broadcast_to(first_row, (SUBLANES, v.shape[1])),
                     rolled[0:SUBLANES])
    return jnp.concatenate([head, rolled[SUBLANES:]], axis=0)


def _causal_conv(phases, tail, w_ref, b_ref):
    taps = w_ref.shape[0]
    bias = b_ref[...]
    shifted = {}

    def phase(idx):
        if idx >= 0:
            return phases[idx]
        if idx not in shifted:
            src = idx + SUBLANES
            shifted[idx] = _group_shift(phases[src], tail[src:src + 1])
        return shifted[idx]

    out = []
    for r in range(SUBLANES):
        acc = w_ref[taps - 1:taps, :] * phases[r] + bias
        for k in range(taps - 1):
            acc = acc + w_ref[k:k + 1, :] * phase(r - (taps - 1) + k)
        out.append(acc)
    return out


def _last_group_rows(phases):
    return jnp.concatenate([p[p.shape[0] - 1:p.shape[0]] for p in phases], axis=0)


def _mixer_kernel(h_ref, gn_ref, win_ref, caw_ref, cab_ref, cbw_ref, cbb_ref, wg_ref,
                  br_ref, bi_ref, lam_ref, gmix_ref, g1_ref, e2_ref, wout_ref,
                  o_ref, taila_ref, tailb_ref, hstate_ref, perm_scr):
    ts = h_ref.shape[0]
    grp = ts // SUBLANES

    @pl.when(pl.program_id(1) == 0)
    def _():
        taila_ref[...] = jnp.zeros_like(taila_ref)
        tailb_ref[...] = jnp.zeros_like(tailb_ref)
        hstate_ref[...] = jnp.zeros_like(hstate_ref)

    n_chunks = h_ref.shape[1] // LANES
    for c in range(n_chunks):
        perm_scr[c] = h_ref[:, c * LANES:(c + 1) * LANES]
    x = jnp.concatenate(
        [jnp.concatenate([perm_scr[c, pl.ds(r, grp, stride=SUBLANES), :]
                          for c in range(n_chunks)], axis=-1)
         for r in range(SUBLANES)], axis=0)
    xn = _rms_norm_rows(x, gn_ref[...])
    z = jnp.dot(xn.astype(BF16), win_ref[...], preferred_element_type=F32)
    xa = z[:, 0:D_CONV]
    gb = z[:, D_CONV:2 * D_CONV]
    gc = z[:, 2 * D_CONV:3 * D_CONV]
    xr = z[:, 3 * D_CONV:3 * D_CONV + D_LRU]
    gr = z[:, 3 * D_CONV + D_LRU:]

    ua = _phases(gc * xa, grp)
    ca = jnp.concatenate(_causal_conv(ua, taila_ref[...], caw_ref, cab_ref), axis=0)
    taila_ref[...] = _last_group_rows(ua)
    ya = gb * ca

    xrp = _phases(xr, grp)
    xc = jnp.concatenate(_causal_conv(xrp, tailb_ref[...], cbw_ref, cbb_ref), axis=0)
    tailb_ref[...] = _last_group_rows(xrp)

    xcb = xc.astype(BF16)
    gz = [jnp.dot(xcb[:, c * MXU_DIM:(c + 1) * MXU_DIM], wg_ref[c], preferred_element_type=F32)
          for c in range(D_LRU // MXU_DIM)]
    gz_r = jnp.concatenate([g[:, 0:MXU_DIM] for g in gz], axis=-1)
    gz_i = jnp.concatenate([g[:, MXU_DIM:] for g in gz], axis=-1)
    r = _sigmoid(gz_r + br_ref[...])
    i = _sigmoid(gz_i + bi_ref[...])
    nl = -lam_ref[...]
    softplus = jnp.maximum(nl, 0.0) + jnp.log1p(jnp.exp(-jnp.abs(nl)))
    log_a = (-LRU_C) * r * softplus
    a = jnp.exp(log_a)
    u = jnp.sqrt(1.0 - a * a) * (i * xc)

    ap = _phases(a, grp)
    up = _phases(u, grp)
    hloc = [up[0]]
    ploc = [ap[0]]
    for ph in range(1, SUBLANES):
        hloc.append(ap[ph] * hloc[ph - 1] + up[ph])
        ploc.append(ap[ph] * ploc[ph - 1])
    ps = ploc[SUBLANES - 1]
    hs = hloc[SUBLANES - 1]
    rows = lax.broadcasted_iota(jnp.int32, ps.shape, 0)
    d = 1
    while d < grp:
        m = rows >= d
        hs = jnp.where(m, ps * pltpu.roll(hs, d, 0) + hs, hs)
        ps = jnp.where(m, ps * pltpu.roll(ps, d, 0), ps)
        d *= 2
    c0 = hstate_ref[0:1, :]
    after = ps * c0 + hs
    cin = _group_shift(after, c0)
    hstate_ref[...] = jnp.broadcast_to(after[grp - 1:grp], hstate_ref.shape)
    hh = jnp.concatenate([hloc[ph] + ploc[ph] * cin for ph in range(SUBLANES)], axis=0)
    yr = hh * _gelu_tanh(gr)

    y = jnp.concatenate([ya, yr], axis=-1)
    gm = jnp.dot((y * y).astype(BF16), g1_ref[...], preferred_element_type=F32)
    rs = lax.rsqrt(gm + EPS)
    rs_hi = rs.astype(BF16)
    rs_lo = (rs - rs_hi.astype(F32)).astype(BF16)
    rs_full = jnp.dot(jnp.concatenate([rs_hi, rs_lo], axis=-1), e2_ref[...],
                      preferred_element_type=F32)
    yn = y * rs_full * gmix_ref[...]
    out = x + jnp.dot(yn.astype(BF16), wout_ref[...], preferred_element_type=F32)
    for c in range(n_chunks):
        for ph in range(SUBLANES):
            perm_scr[c, pl.ds(ph, grp, stride=SUBLANES), :] = (
                out[ph * grp:(ph + 1) * grp, c * LANES:(c + 1) * LANES])
    o_ref[...] = jnp.concatenate([perm_scr[c] for c in range(n_chunks)], axis=-1)


def _mixer(h, p):
    b, s, d = h.shape
    ts = min(MIX_TS, s)
    assert s % ts == 0 and ts % (SUBLANES * SUBLANES) == 0
    d_in = p["w_in"].shape[1]
    d_mix = D_CONV + D_LRU
    const = lambda shape: pl.BlockSpec(shape, lambda bi, j: (0,) * len(shape))
    return pl.pallas_call(
        _mixer_kernel,
        grid=(b, s // ts),
        in_specs=[
            pl.BlockSpec((None, ts, d), lambda bi, j: (bi, j, 0)),
            const((1, d)), const((d, d_in)),
            const((3, D_CONV)), const((1, D_CONV)), const((4, D_LRU)), const((1, D_LRU)),
            const((D_LRU // MXU_DIM, MXU_DIM, 2 * MXU_DIM)),
            const((1, D_LRU)), const((1, D_LRU)), const((1, D_LRU)),
            const((1, d_mix)), const((d_mix, LANES)), const((2 * LANES, d_mix)), const((d_mix, d)),
        ],
        out_specs=pl.BlockSpec((None, ts, d), lambda bi, j: (bi, j, 0)),
        out_shape=jax.ShapeDtypeStruct((b, s, d), F32),
        scratch_shapes=[
            pltpu.VMEM((SUBLANES, D_CONV), F32), pltpu.VMEM((SUBLANES, D_LRU), F32),
            pltpu.VMEM((SUBLANES, D_LRU), F32),
            pltpu.VMEM((d // LANES, ts, LANES), F32),
        ],
        compiler_params=pltpu.CompilerParams(
            dimension_semantics=("arbitrary", "arbitrary"), vmem_limit_bytes=VMEM_LIMIT),
        name="mixer",
    )(h, p["norm_mix_g"], p["w_in"], p["conv_a_w"], p["conv_a_b"], p["conv_b_w"], p["conv_b_b"],
      p["w_gate"], p["b_r"], p["b_i"], p["lam"], p["g_mix"], p["g1"], p["e2"], p["w_out"])


def _swiglu_chunk(xn, w1_ref, w3_ref, w2_ref):
    h1 = jnp.dot(xn, w1_ref[...], preferred_element_type=F32)
    h3 = jnp.dot(xn, w3_ref[...], preferred_element_type=F32)
    g = (h1 * _sigmoid(h1)) * h3
    return jnp.dot(g.astype(BF16), w2_ref[...], preferred_element_type=F32)


def _swiglu_kernel(be_ref, first_ref, nused_ref, x_ref, g_ref, w1_hbm, w3_hbm, w2_hbm, o_ref,
                   xn_scr, acc_scr, res1, res3, res2, st1, st3, st2, sem,
                   *, nf, tf, layer, routed):
    assert nf >= 2
    i = pl.program_id(0)
    f = pl.program_id(1)
    used = i < nused_ref[0]
    tm, d = xn_scr.shape
    e = be_ref[i]

    def chunk_copies(fc):
        cols = pl.ds(pl.multiple_of(fc * tf, tf), tf)
        return (pltpu.make_async_copy(w1_hbm.at[layer, e, :, cols], st1, sem.at[0]),
                pltpu.make_async_copy(w3_hbm.at[layer, e, :, cols], st3, sem.at[1]),
                pltpu.make_async_copy(w2_hbm.at[layer, e, cols, :], st2, sem.at[2]))

    @pl.when(jnp.logical_and(used, first_ref[i] == 1))
    def _():
        @pl.when(f == 0)
        def _():
            for c in chunk_copies(f):
                c.start()

        for c in chunk_copies(f):
            c.wait()
        res1[f] = st1[...].astype(BF16)
        res3[f] = st3[...].astype(BF16)
        res2[f] = st2[...].astype(BF16)

        @pl.when(f + 1 < nf)
        def _():
            for c in chunk_copies(f + 1):
                c.start()

    w_refs = (res1.at[f], res3.at[f], res2.at[f])

    def load_xn():
        if routed:
            return _load_token_major(x_ref, (), tm, d).astype(BF16)
        return _rms_norm_rows(x_ref[...], g_ref[...]).astype(BF16)

    def finish(y):
        if routed:
            _store_token_major(o_ref, (), y)
        else:
            o_ref[...] = x_ref[...] + y

    @pl.when(jnp.logical_and(used, f == 0))
    def _():
        xn = load_xn()
        xn_scr[...] = xn
        acc_scr[...] = _swiglu_chunk(xn, *w_refs)

    @pl.when(jnp.logical_and(used, jnp.logical_and(f > 0, f < nf - 1)))
    def _():
        acc_scr[...] += _swiglu_chunk(xn_scr[...], *w_refs)

    @pl.when(jnp.logical_and(used, f == nf - 1))
    def _():
        finish(acc_scr[...] + _swiglu_chunk(xn_scr[...], *w_refs))

    @pl.when(jnp.logical_and(jnp.logical_not(used), f == nf - 1))
    def _():
        o_ref[...] = jnp.zeros_like(o_ref)


def _swiglu(x, g, block_expert, first, n_used, w1, w3, w2, layer, tm, routed, name):
    d, ff = w1.shape[2], w1.shape[3]
    tf = FFN_TF
    nf = ff // tf
    rows = x.shape[0] // SUBLANES if routed else x.shape[0]
    blk = (tm * SUBLANES, LANES) if routed else (tm, d)
    grid_spec = pltpu.PrefetchScalarGridSpec(
        num_scalar_prefetch=3,
        grid=(rows // tm, nf),
        in_specs=[
            pl.BlockSpec(blk, lambda i, f, *_: (i, 0)),
            pl.BlockSpec((1, d), lambda i, f, *_: (0, 0)),
            pl.BlockSpec(memory_space=pl.ANY),
            pl.BlockSpec(memory_space=pl.ANY),
            pl.BlockSpec(memory_space=pl.ANY),
        ],
        out_specs=pl.BlockSpec(blk, lambda i, f, *_: (i, 0)),
        scratch_shapes=[
            pltpu.VMEM((tm, d), BF16), pltpu.VMEM((tm, d), F32),
            pltpu.VMEM((nf, d, tf), BF16), pltpu.VMEM((nf, d, tf), BF16),
            pltpu.VMEM((nf, tf, d), BF16),
            pltpu.VMEM((d, tf), F32), pltpu.VMEM((d, tf), F32), pltpu.VMEM((tf, d), F32),
            pltpu.SemaphoreType.DMA((3,)),
        ],
    )
    return pl.pallas_call(
        functools.partial(_swiglu_kernel, nf=nf, tf=tf, layer=layer, routed=routed),
        grid_spec=grid_spec,
        out_shape=jax.ShapeDtypeStruct(x.shape, F32),
        compiler_params=pltpu.CompilerParams(
            dimension_semantics=("arbitrary", "arbitrary"), vmem_limit_bytes=VMEM_LIMIT),
        name=name,
    )(block_expert, first, n_used, x, g, w1, w3, w2)


def _dense_ffn(h2, g, w1, w3, w2, j):
    t = h2.shape[0]
    tm = min(FFN_TM, t)
    nb = t // tm
    block_expert = jnp.zeros((nb,), jnp.int32)
    first = jnp.zeros((nb,), jnp.int32).at[0].set(1)
    n_used = jnp.full((1,), nb, jnp.int32)
    return _swiglu(h2, g, block_expert, first, n_used,
                   w1[:, None], w3[:, None], w2[:, None], j, tm, False, "dense_ffn")


def _token_major_index(idx, c, n_tokens):
    return idx + (pl.ds(c, n_tokens, stride=SUBLANES), slice(None))


def _load_token_major(ref, idx, n_tokens, d):
    assert d == SUBLANES * LANES
    return jnp.concatenate(
        [ref[_token_major_index(idx, c, n_tokens)] for c in range(SUBLANES)], axis=-1)


def _store_token_major(ref, idx, value):
    n_tokens, d = value.shape
    assert d == SUBLANES * LANES
    for c in range(SUBLANES):
        ref[_token_major_index(idx, c, n_tokens)] = value[:, c * LANES:(c + 1) * LANES]


def _token_rows(t):
    return pl.ds(pl.multiple_of(t * SUBLANES, SUBLANES), SUBLANES)


def _expert_ffn(xs, g, block_expert, n_used, w1, w3, w2, j):
    first = jnp.concatenate(
        [jnp.ones((1,), jnp.int32), (block_expert[1:] != block_expert[:-1]).astype(jnp.int32)])
    return _swiglu(xs, g, block_expert, first, n_used, w1, w3, w2, j, MOE_TM, True, "expert_ffn")


def _router_kernel(x_ref, g_ref, wcat_ref, eid_ref, gate_ref, rank_ref, cnt_ref, carry_ref):
    tm = x_ref.shape[0]

    @pl.when(pl.program_id(0) == 0)
    def _():
        carry_ref[...] = jnp.zeros_like(carry_ref)

    xn = _rms_norm_rows(x_ref[...], g_ref[...])
    x_hi = xn.astype(BF16)
    x_lo = (xn - x_hi.astype(F32)).astype(BF16)
    nt_dims = (((1,), (1,)), ((), ()))
    wcat = wcat_ref[...]
    l_hi = lax.dot_general(wcat, x_hi, nt_dims, preferred_element_type=F32)
    l_lo = lax.dot_general(wcat, x_lo, nt_dims, preferred_element_type=F32)
    logits = l_hi[0:N_EXPERTS] + l_hi[N_EXPERTS:] + l_lo[0:N_EXPERTS]

    sub = lax.broadcasted_iota(jnp.int32, logits.shape, 0)
    m1 = jnp.max(logits, axis=0, keepdims=True)
    i1 = jnp.min(jnp.where(logits == m1, sub, N_EXPERTS), axis=0, keepdims=True)
    rest = jnp.where(sub == i1, -jnp.inf, logits)
    m2 = jnp.max(rest, axis=0, keepdims=True)
    i2 = jnp.min(jnp.where(rest == m2, sub, N_EXPERTS), axis=0, keepdims=True)
    e2 = jnp.exp(m2 - m1)
    denom = 1.0 + e2
    gate_ref[...] = jnp.concatenate([1.0 / denom, e2 / denom], axis=0)
    eid_ref[...] = jnp.concatenate([i1, i2], axis=0)

    sel1 = sub == i1
    sel2 = sub == i2
    onehot = jnp.where(jnp.logical_or(sel1, sel2), 1.0, 0.0)
    r_io = lax.broadcasted_iota(jnp.int32, (tm, tm), 0)
    c_io = lax.broadcasted_iota(jnp.int32, (tm, tm), 1)
    tri = jnp.where(r_io < c_io, 1.0, 0.0).astype(BF16)
    before = jnp.dot(onehot.astype(BF16), tri, preferred_element_type=F32) + carry_ref[...]
    rank1 = jnp.sum(jnp.where(sel1, before, 0.0), axis=0, keepdims=True)
    rank2 = jnp.sum(jnp.where(sel2, before, 0.0), axis=0, keepdims=True)
    rank_ref[...] = jnp.concatenate([rank1, rank2], axis=0).astype(jnp.int32)
    total = carry_ref[...] + jnp.sum(onehot, axis=1, keepdims=True)
    carry_ref[...] = total
    cnt_ref[...] = total.astype(jnp.int32)


def _router(h2, g, wcat):
    t, d = h2.shape
    tm = min(ROUTE_TM, t)
    pair = lambda dt: jax.ShapeDtypeStruct((TOP_K, t), dt)
    return pl.pallas_call(
        _router_kernel,
        grid=(t // tm,),
        in_specs=[
            pl.BlockSpec((tm, d), lambda i: (i, 0)),
            pl.BlockSpec((1, d), lambda i: (0, 0)),
            pl.BlockSpec((2 * N_EXPERTS, d), lambda i: (0, 0)),
        ],
        out_specs=[
            pl.BlockSpec((TOP_K, tm), lambda i: (0, i)),
            pl.BlockSpec((TOP_K, tm), lambda i: (0, i)),
            pl.BlockSpec((TOP_K, tm), lambda i: (0, i)),
            pl.BlockSpec((N_EXPERTS, 1), lambda i: (0, 0)),
        ],
        out_shape=[pair(jnp.int32), pair(F32), pair(jnp.int32),
                   jax.ShapeDtypeStruct((N_EXPERTS, 1), jnp.int32)],
        scratch_shapes=[pltpu.VMEM((N_EXPERTS, 1), F32)],
        compiler_params=pltpu.CompilerParams(dimension_semantics=("arbitrary",)),
        name="router",
    )(h2, g, wcat)


def _tile_indices(idx, tm):
    k, t = idx.shape
    return idx.reshape(k, t // tm, tm).transpose(1, 0, 2).reshape(t // tm, 1, k * tm)


def _start_row_copies(n_rows, row_copy):
    def body(c, carry):
        for u in range(ISSUE_UNROLL):
            for k in range(TOP_K):
                row_copy(c * ISSUE_UNROLL + u, k).start(priority=k)
        return carry

    lax.fori_loop(0, n_rows // ISSUE_UNROLL, body, 0)


def _dispatch_kernel(zrow_ref, dest_ref, x_ref, g_ref, xs_ref, xn_scr, zero_scr, sem, zsem):
    i = pl.program_id(0)
    tm = x_ref.shape[0]
    slot = i % 2

    @pl.when(i == 0)
    def _():
        zero_scr[...] = jnp.zeros_like(zero_scr)
        zcopies = [
            pltpu.make_async_copy(
                zero_scr,
                xs_ref.at[pl.ds(pl.multiple_of(jnp.maximum(zrow_ref[n], 0) * SUBLANES, MOE_TM),
                                MOE_TM * SUBLANES)],
                zsem)
            for n in range(2 * N_EXPERTS)]
        for n, c in enumerate(zcopies):
            pl.when(zrow_ref[n] >= 0)(c.start)
        for n, c in enumerate(zcopies):
            pl.when(zrow_ref[n] >= 0)(c.wait)

    _store_token_major(xn_scr, (slot,), _rms_norm_rows(x_ref[...], g_ref[...]))

    def row_copy(t, k):
        d = dest_ref[0, 0, k * tm + t]
        return pltpu.make_async_copy(
            xn_scr.at[slot, _token_rows(t)], xs_ref.at[_token_rows(d)], sem.at[slot])

    _start_row_copies(tm, row_copy)

    def wait_slot(s):
        for _ in range(TOP_K):
            pltpu.make_async_copy(
                xn_scr.at[s], xs_ref.at[pl.ds(0, tm * SUBLANES)], sem.at[s]).wait()

    @pl.when(i > 0)
    def _():
        wait_slot(1 - slot)

    @pl.when(i == pl.num_programs(0) - 1)
    def _():
        wait_slot(slot)


def _dispatch(h2, g, dest, zrows, cap):
    t, d = h2.shape
    tm = min(ROW_TM, t)
    dest3 = _tile_indices(dest, tm)
    grid_spec = pltpu.PrefetchScalarGridSpec(
        num_scalar_prefetch=1,
        grid=(t // tm,),
        in_specs=[
            pl.BlockSpec((1, 1, TOP_K * tm), lambda i, z: (i, 0, 0), memory_space=pltpu.SMEM),
            pl.BlockSpec((tm, d), lambda i, z: (i, 0)),
            pl.BlockSpec((1, d), lambda i, z: (0, 0)),
        ],
        out_specs=pl.BlockSpec(memory_space=pl.ANY),
        scratch_shapes=[pltpu.VMEM((2, tm * SUBLANES, LANES), F32),
                        pltpu.VMEM((MOE_TM * SUBLANES, LANES), F32),
                        pltpu.SemaphoreType.DMA((2,)), pltpu.SemaphoreType.DMA],
    )
    return pl.pallas_call(
        _dispatch_kernel,
        grid_spec=grid_spec,
        out_shape=jax.ShapeDtypeStruct((cap * SUBLANES, LANES), F32),
        compiler_params=pltpu.CompilerParams(
            dimension_semantics=("arbitrary",), vmem_limit_bytes=VMEM_LIMIT),
        name="dispatch",
    )(zrows, dest3, h2, g)


def _combine_kernel(pos_ref, posn_ref, h_ref, gate_ref, fg_ref, ys_ref, o_ref, buf, sem,
                    *, final_norm):
    i = pl.program_id(0)
    n = pl.num_programs(0)
    tm = h_ref.shape[0]
    slot = i % 2

    def start_gather(p_ref, s):
        def row_copy(t, k):
            p = p_ref[0, 0, k * tm + t]
            return pltpu.make_async_copy(
                ys_ref.at[_token_rows(p)], buf.at[s, k, _token_rows(t)], sem.at[s])
        _start_row_copies(tm, row_copy)

    @pl.when(i == 0)
    def _():
        start_gather(pos_ref, slot)

    @pl.when(i + 1 < n)
    def _():
        start_gather(posn_ref, 1 - slot)

    for k in range(TOP_K):
        pltpu.make_async_copy(
            ys_ref.at[pl.ds(0, tm * SUBLANES)], buf.at[slot, k], sem.at[slot]).wait()
    gate = gate_ref[...]
    d = h_ref.shape[1]
    out = (h_ref[...] + gate[:, 0:1] * _load_token_major(buf, (slot, 0), tm, d)
           + gate[:, 1:2] * _load_token_major(buf, (slot, 1), tm, d))
    if final_norm:
        out = _rms_norm_rows(out, fg_ref[...])
    o_ref[...] = out


def _combine(h2, gates, pos, ys, final_g, final_norm):
    t, d = h2.shape
    tm = min(ROW_TM, t)
    nt = t // tm
    pos3 = _tile_indices(pos, tm)
    return pl.pallas_call(
        functools.partial(_combine_kernel, final_norm=final_norm),
        grid=(nt,),
        in_specs=[
            pl.BlockSpec((1, 1, TOP_K * tm), lambda i: (i, 0, 0), memory_space=pltpu.SMEM),
            pl.BlockSpec((1, 1, TOP_K * tm), lambda i: (jnp.minimum(i + 1, nt - 1), 0, 0),
                         memory_space=pltpu.SMEM),
            pl.BlockSpec((tm, d), lambda i: (i, 0)),
            pl.BlockSpec((tm, TOP_K), lambda i: (i, 0)),
            pl.BlockSpec((1, d), lambda i: (0, 0)),
            pl.BlockSpec(memory_space=pl.ANY),
        ],
        out_specs=pl.BlockSpec((tm, d), lambda i: (i, 0)),
        out_shape=jax.ShapeDtypeStruct((t, d), F32),
        scratch_shapes=[pltpu.VMEM((2, TOP_K, tm * SUBLANES, LANES), F32),
                        pltpu.SemaphoreType.DMA((2,))],
        compiler_params=pltpu.CompilerParams(
            dimension_semantics=("arbitrary",), vmem_limit_bytes=VMEM_LIMIT),
        name="combine",
    )(pos3, pos3, h2, gates, final_g, ys)


def _moe(h2, g, wcat, w1, w3, w2, j, final_g, final_norm):
    t, d = h2.shape
    eid, gates, rank, counts = _router(h2, g, wcat)
    n_assign = t * TOP_K
    n_blocks = -(-n_assign // MOE_TM) + N_EXPERTS
    cap = n_blocks * MOE_TM
    counts = counts[:, 0]
    padded = (counts + MOE_TM - 1) // MOE_TM * MOE_TM
    pad_end = jnp.cumsum(padded)
    pad_start = pad_end - padded
    experts = jnp.arange(N_EXPERTS, dtype=jnp.int32)[:, None, None]
    dest = rank + jnp.sum(jnp.where(eid[None] == experts, pad_start[:, None, None], 0), axis=0)
    block_start = jnp.arange(n_blocks, dtype=jnp.int32) * MOE_TM
    block_expert = jnp.minimum(
        jnp.sum(block_start[:, None] >= pad_end[None, :], axis=1), N_EXPERTS - 1
    ).astype(jnp.int32)
    n_used = (pad_end[-1:] // MOE_TM).astype(jnp.int32)
    tail_rows = pad_end[-1] + jnp.arange(N_EXPERTS, dtype=jnp.int32) * MOE_TM
    zrows = jnp.concatenate([
        jnp.where(padded > 0, pad_end - MOE_TM, -1),
        jnp.where(tail_rows < cap, tail_rows, -1),
    ]).astype(jnp.int32)
    xs = _dispatch(h2, g, dest, zrows, cap)
    ys = _expert_ffn(xs, g, block_expert, n_used, w1, w3, w2, j)
    return _combine(h2, gates.T, dest, ys, final_g, final_norm)


def _gate_blocks(w_r, w_i):
    nh, hd, _ = w_r.shape
    per = MXU_DIM // hd
    eye = jnp.eye(per, dtype=w_r.dtype)

    def blocks(w):
        wb = w.reshape(nh // per, per, hd, hd)
        return jnp.einsum("cpde,pq->cpdqe", wb, eye).reshape(nh // per, MXU_DIM, MXU_DIM)

    return jnp.concatenate([blocks(w_r), blocks(w_i)], axis=-1)


def _group_matrices(d_mix):
    grp = jnp.arange(d_mix) // HEAD_DIM
    cols = jnp.arange(LANES)
    g1 = jnp.where(grp[:, None] == cols[None, :], 1.0 / HEAD_DIM, 0.0).astype(BF16)
    e = jnp.where(cols[:, None] == grp[None, :], 1.0, 0.0).astype(BF16)
    return g1, jnp.concatenate([e, e], axis=0)


def kernel(x, norm_mix_g, w_in, conv_a_w, conv_a_b, conv_b_w, conv_b_b, lru_w_r, lru_b_r,
           lru_w_i, lru_b_i, lru_lambda, mix_out_g, w_out, norm_ffn_g, ffn_w1, ffn_w3, ffn_w2,
           router_w, expert_w1, expert_w3, expert_w2, final_g):
    b, s, d = x.shape
    depth = w_in.shape[0]
    assert depth % 2 == 0
    g1, e2 = _group_matrices(D_CONV + D_LRU)
    row = lambda v: v.reshape(1, -1)
    h = x
    for layer in range(depth):
        p = {
            "norm_mix_g": row(norm_mix_g[layer]),
            "w_in": w_in[layer].astype(BF16),
            "conv_a_w": conv_a_w[layer], "conv_a_b": row(conv_a_b[layer]),
            "conv_b_w": conv_b_w[layer], "conv_b_b": row(conv_b_b[layer]),
            "w_gate": _gate_blocks(lru_w_r[layer], lru_w_i[layer]).astype(BF16),
            "b_r": row(lru_b_r[layer]), "b_i": row(lru_b_i[layer]), "lam": row(lru_lambda[layer]),
            "g_mix": row(mix_out_g[layer]), "g1": g1, "e2": e2,
            "w_out": w_out[layer].astype(BF16),
        }
        h = _mixer(h, p)
        h2 = h.reshape(b * s, d)
        gf = row(norm_ffn_g[layer])
        j = layer // 2
        if layer % 2 == 0:
            h2 = _dense_ffn(h2, gf, ffn_w1, ffn_w3, ffn_w2, j)
        else:
            w_hi = router_w[j].astype(BF16)
            w_lo = (router_w[j] - w_hi.astype(F32)).astype(BF16)
            h2 = _moe(h2, gf, jnp.concatenate([w_hi.T, w_lo.T], axis=0),
                      expert_w1, expert_w3, expert_w2, j, row(final_g), layer == depth - 1)
        h = h2.reshape(b, s, d)
    return h
```

```python
import functools

import jax
import jax.numpy as jnp
from jax import lax
from jax.experimental import pallas as pl
from jax.experimental.pallas import tpu as pltpu

EPS = 1e-6
HEAD_DIM = 64
D_CONV = 512
D_LRU = 512
LRU_C = 8.0
N_EXPERTS = 8
TOP_K = 2

SUBLANES = 8
LANES = 128
MXU_DIM = 256

MIX_TS = 1024
FFN_TM = 1024
FFN_TF = 512
MOE_TM = 768
ROUTE_TM = 512
ROW_TM = 1024
ISSUE_UNROLL = 8
VMEM_LIMIT = 56 * 1024 * 1024

F32 = jnp.float32
BF16 = jnp.bfloat16


def _rms_norm_rows(x, g):
    return x * lax.rsqrt(jnp.mean(x * x, axis=-1, keepdims=True) + EPS) * g


def _sigmoid(x):
    return 1.0 / (1.0 + jnp.exp(-x))


def _gelu_tanh(x):
    return 0.5 * x * (1.0 + jnp.tanh(0.7978845608028654 * (x + 0.044715 * (x * x * x))))


def _phases(v, g):
    return [v[r * g:(r + 1) * g] for r in range(SUBLANES)]


def _group_shift(v, first_row):
    rolled = pltpu.roll(v, 1, 0)
    rows = lax.broadcasted_iota(jnp.int32, (SUBLANES, v.shape[1]), 0)
    head = jnp.where(rows == 0, jnp.broadcast_to(first_row, (SUBLANES, v.shape[1])),
                     rolled[0:SUBLANES])
    return jnp.concatenate([head, rolled[SUBLANES:]], axis=0)


def _causal_conv(phases, tail, w_ref, b_ref):
    taps = w_ref.shape[0]
    bias = b_ref[...]
    shifted = {}

    def phase(idx):
        if idx >= 0:
            return phases[idx]
        if idx not in shifted:
            src = idx + SUBLANES
            shifted[idx] = _group_shift(phases[src], tail[src:src + 1])
        return shifted[idx]

    out = []
    for r in range(SUBLANES):
        acc = w_ref[taps - 1:taps, :] * phases[r] + bias
        for k in range(taps - 1):
            acc = acc + w_ref[k:k + 1, :] * phase(r - (taps - 1) + k)
        out.append(acc)
    return out


def _last_group_rows(phases):
    return jnp.concatenate([p[p.shape[0] - 1:p.shape[0]] for p in phases], axis=0)


def _to_phase_major(src_ref, perm_scr):
    ts, d = src_ref.shape
    grp = ts // SUBLANES
    for c in range(d // LANES):
        perm_scr[c] = src_ref[:, c * LANES:(c + 1) * LANES]
    return jnp.concatenate(
        [jnp.concatenate([perm_scr[c, pl.ds(r, grp, stride=SUBLANES), :]
                          for c in range(d // LANES)], axis=-1)
         for r in range(SUBLANES)], axis=0)


def _store_time_major(dst_ref, value, perm_scr):
    ts, d = value.shape
    grp = ts // SUBLANES
    for c in range(d // LANES):
        for ph in range(SUBLANES):
            perm_scr[c, pl.ds(ph, grp, stride=SUBLANES), :] = (
                value[ph * grp:(ph + 1) * grp, c * LANES:(c + 1) * LANES])
    dst_ref[...] = jnp.concatenate([perm_scr[c] for c in range(d // LANES)], axis=-1)


def _mixer_kernel(h_ref, gn_ref, win_ref, caw_ref, cab_ref, cbw_ref, cbb_ref, wg_ref,
                  br_ref, bi_ref, lam_ref, gmix_ref, g1_ref, e2_ref, wout_ref,
                  o_ref, taila_ref, tailb_ref, hstate_ref, perm_scr, *, time_major_in):
    ts = h_ref.shape[0]
    grp = ts // SUBLANES

    @pl.when(pl.program_id(1) == 0)
    def _():
        taila_ref[...] = jnp.zeros_like(taila_ref)
        tailb_ref[...] = jnp.zeros_like(tailb_ref)
        hstate_ref[...] = jnp.zeros_like(hstate_ref)

    x = _to_phase_major(h_ref, perm_scr) if time_major_in else h_ref[...]
    xn = _rms_norm_rows(x, gn_ref[...])
    z = jnp.dot(xn.astype(BF16), win_ref[...], preferred_element_type=F32)
    xa = z[:, 0:D_CONV]
    gb = z[:, D_CONV:2 * D_CONV]
    gc = z[:, 2 * D_CONV:3 * D_CONV]
    xr = z[:, 3 * D_CONV:3 * D_CONV + D_LRU]
    gr = z[:, 3 * D_CONV + D_LRU:]

    ua = _phases(gc * xa, grp)
    ca = jnp.concatenate(_causal_conv(ua, taila_ref[...], caw_ref, cab_ref), axis=0)
    taila_ref[...] = _last_group_rows(ua)
    ya = gb * ca

    xrp = _phases(xr, grp)
    xc = jnp.concatenate(_causal_conv(xrp, tailb_ref[...], cbw_ref, cbb_ref), axis=0)
    tailb_ref[...] = _last_group_rows(xrp)

    xcb = xc.astype(BF16)
    gz = [jnp.dot(xcb[:, c * MXU_DIM:(c + 1) * MXU_DIM], wg_ref[c], preferred_element_type=F32)
          for c in range(D_LRU // MXU_DIM)]
    gz_r = jnp.concatenate([g[:, 0:MXU_DIM] for g in gz], axis=-1)
    gz_i = jnp.concatenate([g[:, MXU_DIM:] for g in gz], axis=-1)
    r = _sigmoid(gz_r + br_ref[...])
    i = _sigmoid(gz_i + bi_ref[...])
    nl = -lam_ref[...]
    softplus = jnp.maximum(nl, 0.0) + jnp.log1p(jnp.exp(-jnp.abs(nl)))
    log_a = (-LRU_C) * r * softplus
    a = jnp.exp(log_a)
    u = jnp.sqrt(1.0 - a * a) * (i * xc)

    ap = _phases(a, grp)
    up = _phases(u, grp)
    hloc = [up[0]]
    ploc = [ap[0]]
    for ph in range(1, SUBLANES):
        hloc.append(ap[ph] * hloc[ph - 1] + up[ph])
        ploc.append(ap[ph] * ploc[ph - 1])
    ps = ploc[SUBLANES - 1]
    hs = hloc[SUBLANES - 1]
    rows = lax.broadcasted_iota(jnp.int32, ps.shape, 0)
    d = 1
    while d < grp:
        m = rows >= d
        hs = jnp.where(m, ps * pltpu.roll(hs, d, 0) + hs, hs)
        ps = jnp.where(m, ps * pltpu.roll(ps, d, 0), ps)
        d *= 2
    c0 = hstate_ref[0:1, :]
    after = ps * c0 + hs
    cin = _group_shift(after, c0)
    hstate_ref[...] = jnp.broadcast_to(after[grp - 1:grp], hstate_ref.shape)
    hh = jnp.concatenate([hloc[ph] + ploc[ph] * cin for ph in range(SUBLANES)], axis=0)
    yr = hh * _gelu_tanh(gr)

    y = jnp.concatenate([ya, yr], axis=-1)
    gm = jnp.dot((y * y).astype(BF16), g1_ref[...], preferred_element_type=F32)
    rs = lax.rsqrt(gm + EPS)
    rs_hi = rs.astype(BF16)
    rs_lo = (rs - rs_hi.astype(F32)).astype(BF16)
    rs_full = jnp.dot(jnp.concatenate([rs_hi, rs_lo], axis=-1), e2_ref[...],
                      preferred_element_type=F32)
    yn = y * rs_full * gmix_ref[...]
    o_ref[...] = x + jnp.dot(yn.astype(BF16), wout_ref[...], preferred_element_type=F32)


def _mixer(h, p, time_major_in):
    b, s, d = h.shape
    ts = MIX_TS
    assert s % ts == 0 and ts % (SUBLANES * SUBLANES) == 0
    d_in = p["w_in"].shape[1]
    d_mix = D_CONV + D_LRU
    const = lambda shape: pl.BlockSpec(shape, lambda bi, j: (0,) * len(shape))
    return pl.pallas_call(
        functools.partial(_mixer_kernel, time_major_in=time_major_in),
        grid=(b, s // ts),
        in_specs=[
            pl.BlockSpec((None, ts, d), lambda bi, j: (bi, j, 0)),
            const((1, d)), const((d, d_in)),
            const((3, D_CONV)), const((1, D_CONV)), const((4, D_LRU)), const((1, D_LRU)),
            const((D_LRU // MXU_DIM, MXU_DIM, 2 * MXU_DIM)),
            const((1, D_LRU)), const((1, D_LRU)), const((1, D_LRU)),
            const((1, d_mix)), const((d_mix, LANES)), const((2 * LANES, d_mix)), const((d_mix, d)),
        ],
        out_specs=pl.BlockSpec((None, ts, d), lambda bi, j: (bi, j, 0)),
        out_shape=jax.ShapeDtypeStruct((b, s, d), F32),
        scratch_shapes=[
            pltpu.VMEM((SUBLANES, D_CONV), F32), pltpu.VMEM((SUBLANES, D_LRU), F32),
            pltpu.VMEM((SUBLANES, D_LRU), F32),
            pltpu.VMEM((d // LANES, ts, LANES), F32),
        ],
        compiler_params=pltpu.CompilerParams(
            dimension_semantics=("arbitrary", "arbitrary"), vmem_limit_bytes=VMEM_LIMIT),
        name="mixer",
    )(h, p["norm_mix_g"], p["w_in"], p["conv_a_w"], p["conv_a_b"], p["conv_b_w"], p["conv_b_b"],
      p["w_gate"], p["b_r"], p["b_i"], p["lam"], p["g_mix"], p["g1"], p["e2"], p["w_out"])


def _swiglu_chunk(xn, w1_ref, w3_ref, w2_ref):
    h1 = jnp.dot(xn, w1_ref[...], preferred_element_type=F32)
    h3 = jnp.dot(xn, w3_ref[...], preferred_element_type=F32)
    g = (h1 * _sigmoid(h1)) * h3
    return jnp.dot(g.astype(BF16), w2_ref[...], preferred_element_type=F32)


def _swiglu_kernel(be_ref, first_ref, nused_ref, x_ref, g_ref, w1_hbm, w3_hbm, w2_hbm, o_ref,
                   xn_scr, acc_scr, res1, res3, res2, st1, st3, st2, sem,
                   *, nf, tf, layer, routed):
    assert nf >= 2
    i = pl.program_id(0)
    f = pl.program_id(1)
    used = i < nused_ref[0]
    tm, d = xn_scr.shape
    e = be_ref[i]

    def chunk_copies(fc):
        cols = pl.ds(pl.multiple_of(fc * tf, tf), tf)
        return (pltpu.make_async_copy(w1_hbm.at[layer, e, :, cols], st1, sem.at[0]),
                pltpu.make_async_copy(w3_hbm.at[layer, e, :, cols], st3, sem.at[1]),
                pltpu.make_async_copy(w2_hbm.at[layer, e, cols, :], st2, sem.at[2]))

    @pl.when(jnp.logical_and(used, first_ref[i] == 1))
    def _():
        @pl.when(f == 0)
        def _():
            for c in chunk_copies(f):
                c.start()

        for c in chunk_copies(f):
            c.wait()
        res1[f] = st1[...].astype(BF16)
        res3[f] = st3[...].astype(BF16)
        res2[f] = st2[...].astype(BF16)

        @pl.when(f + 1 < nf)
        def _():
            for c in chunk_copies(f + 1):
                c.start()

    w_refs = (res1.at[f], res3.at[f], res2.at[f])

    def load_xn():
        if routed:
            return _load_token_major(x_ref, (), tm, d).astype(BF16)
        return _rms_norm_rows(x_ref[...], g_ref[...]).astype(BF16)

    def finish(y):
        if routed:
            _store_token_major(o_ref, (), y)
        else:
            o_ref[...] = x_ref[...] + y

    @pl.when(jnp.logical_and(used, f == 0))
    def _():
        xn = load_xn()
        xn_scr[...] = xn
        acc_scr[...] = _swiglu_chunk(xn, *w_refs)

    @pl.when(jnp.logical_and(used, jnp.logical_and(f > 0, f < nf - 1)))
    def _():
        acc_scr[...] += _swiglu_chunk(xn_scr[...], *w_refs)

    @pl.when(jnp.logical_and(used, f == nf - 1))
    def _():
        finish(acc_scr[...] + _swiglu_chunk(xn_scr[...], *w_refs))

    @pl.when(jnp.logical_and(jnp.logical_not(used), f == nf - 1))
    def _():
        o_ref[...] = jnp.zeros_like(o_ref)


def _swiglu(x, g, block_expert, first, n_used, w1, w3, w2, layer, tm, routed, name):
    d, ff = w1.shape[2], w1.shape[3]
    tf = FFN_TF
    nf = ff // tf
    rows = x.shape[0] // SUBLANES if routed else x.shape[0]
    blk = (tm * SUBLANES, LANES) if routed else (tm, d)
    grid_spec = pltpu.PrefetchScalarGridSpec(
        num_scalar_prefetch=3,
        grid=(rows // tm, nf),
        in_specs=[
            pl.BlockSpec(blk, lambda i, f, *_: (i, 0)),
            pl.BlockSpec((1, d), lambda i, f, *_: (0, 0)),
            pl.BlockSpec(memory_space=pl.ANY),
            pl.BlockSpec(memory_space=pl.ANY),
            pl.BlockSpec(memory_space=pl.ANY),
        ],
        out_specs=pl.BlockSpec(blk, lambda i, f, *_: (i, 0)),
        scratch_shapes=[
            pltpu.VMEM((tm, d), BF16), pltpu.VMEM((tm, d), F32),
            pltpu.VMEM((nf, d, tf), BF16), pltpu.VMEM((nf, d, tf), BF16),
            pltpu.VMEM((nf, tf, d), BF16),
            pltpu.VMEM((d, tf), F32), pltpu.VMEM((d, tf), F32), pltpu.VMEM((tf, d), F32),
            pltpu.SemaphoreType.DMA((3,)),
        ],
    )
    return pl.pallas_call(
        functools.partial(_swiglu_kernel, nf=nf, tf=tf, layer=layer, routed=routed),
        grid_spec=grid_spec,
        out_shape=jax.ShapeDtypeStruct(x.shape, F32),
        compiler_params=pltpu.CompilerParams(
            dimension_semantics=("arbitrary", "arbitrary"), vmem_limit_bytes=VMEM_LIMIT),
        name=name,
    )(block_expert, first, n_used, x, g, w1, w3, w2)


def _dense_ffn(h2, g, w1, w3, w2, j):
    t = h2.shape[0]
    tm = min(FFN_TM, t)
    nb = t // tm
    block_expert = jnp.zeros((nb,), jnp.int32)
    first = jnp.zeros((nb,), jnp.int32).at[0].set(1)
    n_used = jnp.full((1,), nb, jnp.int32)
    return _swiglu(h2, g, block_expert, first, n_used,
                   w1[:, None], w3[:, None], w2[:, None], j, tm, False, "dense_ffn")


def _token_major_index(idx, c, n_tokens):
    return idx + (pl.ds(c, n_tokens, stride=SUBLANES), slice(None))


def _load_token_major(ref, idx, n_tokens, d):
    assert d == SUBLANES * LANES
    return jnp.concatenate(
        [ref[_token_major_index(idx, c, n_tokens)] for c in range(SUBLANES)], axis=-1)


def _store_token_major(ref, idx, value):
    n_tokens, d = value.shape
    assert d == SUBLANES * LANES
    for c in range(SUBLANES):
        ref[_token_major_index(idx, c, n_tokens)] = value[:, c * LANES:(c + 1) * LANES]


def _token_rows(t):
    return pl.ds(pl.multiple_of(t * SUBLANES, SUBLANES), SUBLANES)


def _expert_ffn(xs, g, block_expert, n_used, w1, w3, w2, j):
    first = jnp.concatenate(
        [jnp.ones((1,), jnp.int32), (block_expert[1:] != block_expert[:-1]).astype(jnp.int32)])
    return _swiglu(xs, g, block_expert, first, n_used, w1, w3, w2, j, MOE_TM, True, "expert_ffn")


def _router_kernel(x_ref, g_ref, wcat_ref, eid_ref, gate_ref, rank_ref, cnt_ref, carry_ref):
    tm = x_ref.shape[0]

    @pl.when(pl.program_id(0) == 0)
    def _():
        carry_ref[...] = jnp.zeros_like(carry_ref)

    xn = _rms_norm_rows(x_ref[...], g_ref[...])
    x_hi = xn.astype(BF16)
    x_lo = (xn - x_hi.astype(F32)).astype(BF16)
    nt_dims = (((1,), (1,)), ((), ()))
    wcat = wcat_ref[...]
    l_hi = lax.dot_general(wcat, x_hi, nt_dims, preferred_element_type=F32)
    l_lo = lax.dot_general(wcat, x_lo, nt_dims, preferred_element_type=F32)
    logits = l_hi[0:N_EXPERTS] + l_hi[N_EXPERTS:] + l_lo[0:N_EXPERTS]

    sub = lax.broadcasted_iota(jnp.int32, logits.shape, 0)
    m1 = jnp.max(logits, axis=0, keepdims=True)
    i1 = jnp.min(jnp.where(logits == m1, sub, N_EXPERTS), axis=0, keepdims=True)
    rest = jnp.where(sub == i1, -jnp.inf, logits)
    m2 = jnp.max(rest, axis=0, keepdims=True)
    i2 = jnp.min(jnp.where(rest == m2, sub, N_EXPERTS), axis=0, keepdims=True)
    e2 = jnp.exp(m2 - m1)
    denom = 1.0 + e2
    gate_ref[...] = jnp.concatenate([1.0 / denom, e2 / denom], axis=0)
    eid_ref[...] = jnp.concatenate([i1, i2], axis=0)

    sel1 = sub == i1
    sel2 = sub == i2
    onehot = jnp.where(jnp.logical_or(sel1, sel2), 1.0, 0.0)
    r_io = lax.broadcasted_iota(jnp.int32, (tm, tm), 0)
    c_io = lax.broadcasted_iota(jnp.int32, (tm, tm), 1)
    tri = jnp.where(r_io < c_io, 1.0, 0.0).astype(BF16)
    before = jnp.dot(onehot.astype(BF16), tri, preferred_element_type=F32) + carry_ref[...]
    rank1 = jnp.sum(jnp.where(sel1, before, 0.0), axis=0, keepdims=True)
    rank2 = jnp.sum(jnp.where(sel2, before, 0.0), axis=0, keepdims=True)
    rank_ref[...] = jnp.concatenate([rank1, rank2], axis=0).astype(jnp.int32)
    total = carry_ref[...] + jnp.sum(onehot, axis=1, keepdims=True)
    carry_ref[...] = total
    cnt_ref[...] = total.astype(jnp.int32)


def _router(h2, g, wcat):
    t, d = h2.shape
    tm = min(ROUTE_TM, t)
    pair = lambda dt: jax.ShapeDtypeStruct((TOP_K, t), dt)
    return pl.pallas_call(
        _router_kernel,
        grid=(t // tm,),
        in_specs=[
            pl.BlockSpec((tm, d), lambda i: (i, 0)),
            pl.BlockSpec((1, d), lambda i: (0, 0)),
            pl.BlockSpec((2 * N_EXPERTS, d), lambda i: (0, 0)),
        ],
        out_specs=[
            pl.BlockSpec((TOP_K, tm), lambda i: (0, i)),
            pl.BlockSpec((TOP_K, tm), lambda i: (0, i)),
            pl.BlockSpec((TOP_K, tm), lambda i: (0, i)),
            pl.BlockSpec((N_EXPERTS, 1), lambda i: (0, 0)),
        ],
        out_shape=[pair(jnp.int32), pair(F32), pair(jnp.int32),
                   jax.ShapeDtypeStruct((N_EXPERTS, 1), jnp.int32)],
        scratch_shapes=[pltpu.VMEM((N_EXPERTS, 1), F32)],
        compiler_params=pltpu.CompilerParams(dimension_semantics=("arbitrary",)),
        name="router",
    )(h2, g, wcat)


def _tile_indices(idx, tm):
    k, t = idx.shape
    return idx.reshape(k, t // tm, tm).transpose(1, 0, 2).reshape(t // tm, 1, k * tm)


def _start_row_copies(n_rows, row_copy):
    def body(c, carry):
        for u in range(ISSUE_UNROLL):
            for k in range(TOP_K):
                row_copy(c * ISSUE_UNROLL + u, k).start(priority=k)
        return carry

    lax.fori_loop(0, n_rows // ISSUE_UNROLL, body, 0)


def _dispatch_kernel(zrow_ref, dest_ref, x_ref, g_ref, xs_ref, xn_scr, zero_scr, sem, zsem):
    i = pl.program_id(0)
    tm = x_ref.shape[0]
    slot = i % 2

    @pl.when(i == 0)
    def _():
        zero_scr[...] = jnp.zeros_like(zero_scr)
        zcopies = [
            pltpu.make_async_copy(
                zero_scr,
                xs_ref.at[pl.ds(pl.multiple_of(jnp.maximum(zrow_ref[n], 0) * SUBLANES, MOE_TM),
                                MOE_TM * SUBLANES)],
                zsem)
            for n in range(2 * N_EXPERTS)]
        for n, c in enumerate(zcopies):
            pl.when(zrow_ref[n] >= 0)(c.start)
        for n, c in enumerate(zcopies):
            pl.when(zrow_ref[n] >= 0)(c.wait)

    _store_token_major(xn_scr, (slot,), _rms_norm_rows(x_ref[...], g_ref[...]))

    def row_copy(t, k):
        d = dest_ref[0, 0, k * tm + t]
        return pltpu.make_async_copy(
            xn_scr.at[slot, _token_rows(t)], xs_ref.at[_token_rows(d)], sem.at[slot])

    _start_row_copies(tm, row_copy)

    def wait_slot(s):
        for _ in range(TOP_K):
            pltpu.make_async_copy(
                xn_scr.at[s], xs_ref.at[pl.ds(0, tm * SUBLANES)], sem.at[s]).wait()

    @pl.when(i > 0)
    def _():
        wait_slot(1 - slot)

    @pl.when(i == pl.num_programs(0) - 1)
    def _():
        wait_slot(slot)


def _dispatch(h2, g, dest, zrows, cap):
    t, d = h2.shape
    tm = min(ROW_TM, t)
    dest3 = _tile_indices(dest, tm)
    grid_spec = pltpu.PrefetchScalarGridSpec(
        num_scalar_prefetch=1,
        grid=(t // tm,),
        in_specs=[
            pl.BlockSpec((1, 1, TOP_K * tm), lambda i, z: (i, 0, 0), memory_space=pltpu.SMEM),
            pl.BlockSpec((tm, d), lambda i, z: (i, 0)),
            pl.BlockSpec((1, d), lambda i, z: (0, 0)),
        ],
        out_specs=pl.BlockSpec(memory_space=pl.ANY),
        scratch_shapes=[pltpu.VMEM((2, tm * SUBLANES, LANES), F32),
                        pltpu.VMEM((MOE_TM * SUBLANES, LANES), F32),
                        pltpu.SemaphoreType.DMA((2,)), pltpu.SemaphoreType.DMA],
    )
    return pl.pallas_call(
        _dispatch_kernel,
        grid_spec=grid_spec,
        out_shape=jax.ShapeDtypeStruct((cap * SUBLANES, LANES), F32),
        compiler_params=pltpu.CompilerParams(
            dimension_semantics=("arbitrary",), vmem_limit_bytes=VMEM_LIMIT),
        name="dispatch",
    )(zrows, dest3, h2, g)


def _combine_kernel(pos_ref, posn_ref, h_ref, gate_ref, fg_ref, ys_ref, o_ref, buf, sem, perm_scr,
                    *, final_norm):
    i = pl.program_id(0)
    n = pl.num_programs(0)
    tm = h_ref.shape[0]
    slot = i % 2

    def start_gather(p_ref, s):
        def row_copy(t, k):
            p = p_ref[0, 0, k * tm + t]
            return pltpu.make_async_copy(
                ys_ref.at[_token_rows(p)], buf.at[s, k, _token_rows(t)], sem.at[s])
        _start_row_copies(tm, row_copy)

    @pl.when(i == 0)
    def _():
        start_gather(pos_ref, slot)

    @pl.when(i + 1 < n)
    def _():
        start_gather(posn_ref, 1 - slot)

    for k in range(TOP_K):
        pltpu.make_async_copy(
            ys_ref.at[pl.ds(0, tm * SUBLANES)], buf.at[slot, k], sem.at[slot]).wait()
    gate = gate_ref[...]
    d = h_ref.shape[1]
    out = (h_ref[...] + gate[:, 0:1] * _load_token_major(buf, (slot, 0), tm, d)
           + gate[:, 1:2] * _load_token_major(buf, (slot, 1), tm, d))
    if final_norm:
        _store_time_major(o_ref, _rms_norm_rows(out, fg_ref[...]), perm_scr)
    else:
        o_ref[...] = out


def _combine(h2, gates, pos, ys, final_g, final_norm):
    t, d = h2.shape
    tm = min(ROW_TM, t)
    assert tm == MIX_TS
    nt = t // tm
    pos3 = _tile_indices(pos, tm)
    return pl.pallas_call(
        functools.partial(_combine_kernel, final_norm=final_norm),
        grid=(nt,),
        in_specs=[
            pl.BlockSpec((1, 1, TOP_K * tm), lambda i: (i, 0, 0), memory_space=pltpu.SMEM),
            pl.BlockSpec((1, 1, TOP_K * tm), lambda i: (jnp.minimum(i + 1, nt - 1), 0, 0),
                         memory_space=pltpu.SMEM),
            pl.BlockSpec((tm, d), lambda i: (i, 0)),
            pl.BlockSpec((tm, TOP_K), lambda i: (i, 0)),
            pl.BlockSpec((1, d), lambda i: (0, 0)),
            pl.BlockSpec(memory_space=pl.ANY),
        ],
        out_specs=pl.BlockSpec((tm, d), lambda i: (i, 0)),
        out_shape=jax.ShapeDtypeStruct((t, d), F32),
        scratch_shapes=[pltpu.VMEM((2, TOP_K, tm * SUBLANES, LANES), F32),
                        pltpu.SemaphoreType.DMA((2,)),
                        pltpu.VMEM((d // LANES, tm, LANES), F32)],
        compiler_params=pltpu.CompilerParams(
            dimension_semantics=("arbitrary",), vmem_limit_bytes=VMEM_LIMIT),
        name="combine",
    )(pos3, pos3, h2, gates, final_g, ys)


def _moe(h2, g, wcat, w1, w3, w2, j, final_g, final_norm):
    t, d = h2.shape
    eid, gates, rank, counts = _router(h2, g, wcat)
    n_assign = t * TOP_K
    n_blocks = -(-n_assign // MOE_TM) + N_EXPERTS
    cap = n_blocks * MOE_TM
    counts = counts[:, 0]
    padded = (counts + MOE_TM - 1) // MOE_TM * MOE_TM
    pad_end = jnp.cumsum(padded)
    pad_start = pad_end - padded
    experts = jnp.arange(N_EXPERTS, dtype=jnp.int32)[:, None, None]
    dest = rank + jnp.sum(jnp.where(eid[None] == experts, pad_start[:, None, None], 0), axis=0)
    block_start = jnp.arange(n_blocks, dtype=jnp.int32) * MOE_TM
    block_expert = jnp.minimum(
        jnp.sum(block_start[:, None] >= pad_end[None, :], axis=1), N_EXPERTS - 1
    ).astype(jnp.int32)
    n_used = (pad_end[-1:] // MOE_TM).astype(jnp.int32)
    tail_rows = pad_end[-1] + jnp.arange(N_EXPERTS, dtype=jnp.int32) * MOE_TM
    zrows = jnp.concatenate([
        jnp.where(padded > 0, pad_end - MOE_TM, -1),
        jnp.where(tail_rows < cap, tail_rows, -1),
    ]).astype(jnp.int32)
    xs = _dispatch(h2, g, dest, zrows, cap)
    ys = _expert_ffn(xs, g, block_expert, n_used, w1, w3, w2, j)
    return _combine(h2, gates.T, dest, ys, final_g, final_norm)


def _gate_blocks(w_r, w_i):
    nh, hd, _ = w_r.shape
    per = MXU_DIM // hd
    eye = jnp.eye(per, dtype=w_r.dtype)

    def blocks(w):
        wb = w.reshape(nh // per, per, hd, hd)
        return jnp.einsum("cpde,pq->cpdqe", wb, eye).reshape(nh // per, MXU_DIM, MXU_DIM)

    return jnp.concatenate([blocks(w_r), blocks(w_i)], axis=-1)


def _group_matrices(d_mix):
    grp = jnp.arange(d_mix) // HEAD_DIM
    cols = jnp.arange(LANES)
    g1 = jnp.where(grp[:, None] == cols[None, :], 1.0 / HEAD_DIM, 0.0).astype(BF16)
    e = jnp.where(cols[:, None] == grp[None, :], 1.0, 0.0).astype(BF16)
    return g1, jnp.concatenate([e, e], axis=0)


def kernel(x, norm_mix_g, w_in, conv_a_w, conv_a_b, conv_b_w, conv_b_b, lru_w_r, lru_b_r,
           lru_w_i, lru_b_i, lru_lambda, mix_out_g, w_out, norm_ffn_g, ffn_w1, ffn_w3, ffn_w2,
           router_w, expert_w1, expert_w3, expert_w2, final_g):
    b, s, d = x.shape
    depth = w_in.shape[0]
    assert depth % 2 == 0
    g1, e2 = _group_matrices(D_CONV + D_LRU)
    row = lambda v: v.reshape(1, -1)
    h = x
    for layer in range(depth):
        p = {
            "norm_mix_g": row(norm_mix_g[layer]),
            "w_in": w_in[layer].astype(BF16),
            "conv_a_w": conv_a_w[layer], "conv_a_b": row(conv_a_b[layer]),
            "conv_b_w": conv_b_w[layer], "conv_b_b": row(conv_b_b[layer]),
            "w_gate": _gate_blocks(lru_w_r[layer], lru_w_i[layer]).astype(BF16),
            "b_r": row(lru_b_r[layer]), "b_i": row(lru_b_i[layer]), "lam": row(lru_lambda[layer]),
            "g_mix": row(mix_out_g[layer]), "g1": g1, "e2": e2,
            "w_out": w_out[layer].astype(BF16),
        }
        h = _mixer(h, p, layer == 0)
        h2 = h.reshape(b * s, d)
        gf = row(norm_ffn_g[layer])
        j = layer // 2
        if layer % 2 == 0:
            h2 = _dense_ffn(h2, gf, ffn_w1, ffn_w3, ffn_w2, j)
        else:
            w_hi = router_w[j].astype(BF16)
            w_lo = (router_w[j] - w_hi.astype(F32)).astype(BF16)
            h2 = _moe(h2, gf, jnp.concatenate([w_hi.T, w_lo.T], axis=0),
                      expert_w1, expert_w3, expert_w2, j, row(final_g), layer == depth - 1)
        h = h2.reshape(b, s, d)
    return h
```

```python
import functools

import jax
import jax.numpy as jnp
from jax import lax
from jax.experimental import pallas as pl
from jax.experimental.pallas import tpu as pltpu

EPS = 1e-6
HEAD_DIM = 64
D_CONV = 512
D_LRU = 512
LRU_C = 8.0
N_EXPERTS = 8
TOP_K = 2

SUBLANES = 8
LANES = 128
MXU_DIM = 256

MIX_TS = 1024
FFN_TM = 1024
FFN_TF = 512
MOE_TM = 768
ROUTE_TM = 512
DISPATCH_TM = 1024
COMBINE_TM = 512
ISSUE_UNROLL = 8
VMEM_LIMIT = 56 * 1024 * 1024

F32 = jnp.float32
BF16 = jnp.bfloat16


def _rms_norm_rows(x, g):
    return x * lax.rsqrt(jnp.mean(x * x, axis=-1, keepdims=True) + EPS) * g


def _sigmoid(x):
    return 1.0 / (1.0 + jnp.exp(-x))


def _gelu_tanh(x):
    return 0.5 * x * (1.0 + jnp.tanh(0.7978845608028654 * (x + 0.044715 * (x * x * x))))


def _phases(v, g):
    return [v[r * g:(r + 1) * g] for r in range(SUBLANES)]


def _group_shift(v, first_row):
    rolled = pltpu.roll(v, 1, 0)
    rows = lax.broadcasted_iota(jnp.int32, (SUBLANES, v.shape[1]), 0)
    head = jnp.where(rows == 0, jnp.broadcast_to(first_row, (SUBLANES, v.shape[1])),
                     rolled[0:SUBLANES])
    return jnp.concatenate([head, rolled[SUBLANES:]], axis=0)


def _causal_conv(phases, tail, w_ref, b_ref):
    taps = w_ref.shape[0]
    bias = b_ref[...]
    shifted = {}

    def phase(idx):
        if idx >= 0:
            return phases[idx]
        if idx not in shifted:
            src = idx + SUBLANES
            shifted[idx] = _group_shift(phases[src], tail[src:src + 1])
        return shifted[idx]

    out = []
    for r in range(SUBLANES):
        acc = w_ref[taps - 1:taps, :] * phases[r] + bias
        for k in range(taps - 1):
            acc = acc + w_ref[k:k + 1, :] * phase(r - (taps - 1) + k)
        out.append(acc)
    return out


def _last_group_rows(phases):
    return jnp.concatenate([p[p.shape[0] - 1:p.shape[0]] for p in phases], axis=0)


def _to_phase_major(src_ref, perm_scr):
    ts, d = src_ref.shape
    grp = ts // SUBLANES
    for c in range(d // LANES):
        perm_scr[c] = src_ref[:, c * LANES:(c + 1) * LANES]
    return jnp.concatenate(
        [jnp.concatenate([perm_scr[c, pl.ds(r, grp, stride=SUBLANES), :]
                          for c in range(d // LANES)], axis=-1)
         for r in range(SUBLANES)], axis=0)


def _store_time_major(dst_ref, value, perm_scr):
    ts, d = value.shape
    grp = ts // SUBLANES
    for c in range(d // LANES):
        for ph in range(SUBLANES):
            perm_scr[c, pl.ds(ph, grp, stride=SUBLANES), :] = (
                value[ph * grp:(ph + 1) * grp, c * LANES:(c + 1) * LANES])
    dst_ref[...] = jnp.concatenate([perm_scr[c] for c in range(d // LANES)], axis=-1)


def _mixer_kernel(h_ref, gn_ref, win_ref, caw_ref, cab_ref, cbw_ref, cbb_ref, wg_ref,
                  br_ref, bi_ref, lam_ref, gmix_ref, g1_ref, e2_ref, wout_ref,
                  o_ref, taila_ref, tailb_ref, hstate_ref, perm_scr,
                  *, time_major_in, time_major_out):
    ts = h_ref.shape[0]
    grp = ts // SUBLANES

    @pl.when(pl.program_id(1) == 0)
    def _():
        taila_ref[...] = jnp.zeros_like(taila_ref)
        tailb_ref[...] = jnp.zeros_like(tailb_ref)
        hstate_ref[...] = jnp.zeros_like(hstate_ref)

    x = _to_phase_major(h_ref, perm_scr) if time_major_in else h_ref[...]
    xn = _rms_norm_rows(x, gn_ref[...])
    z = jnp.dot(xn.astype(BF16), win_ref[...], preferred_element_type=F32)
    xa = z[:, 0:D_CONV]
    gb = z[:, D_CONV:2 * D_CONV]
    gc = z[:, 2 * D_CONV:3 * D_CONV]
    xr = z[:, 3 * D_CONV:3 * D_CONV + D_LRU]
    gr = z[:, 3 * D_CONV + D_LRU:]

    ua = _phases(gc * xa, grp)
    ca = jnp.concatenate(_causal_conv(ua, taila_ref[...], caw_ref, cab_ref), axis=0)
    taila_ref[...] = _last_group_rows(ua)
    ya = gb * ca

    xrp = _phases(xr, grp)
    xc = jnp.concatenate(_causal_conv(xrp, tailb_ref[...], cbw_ref, cbb_ref), axis=0)
    tailb_ref[...] = _last_group_rows(xrp)

    xcb = xc.astype(BF16)
    gz = [jnp.dot(xcb[:, c * MXU_DIM:(c + 1) * MXU_DIM], wg_ref[c], preferred_element_type=F32)
          for c in range(D_LRU // MXU_DIM)]
    gz_r = jnp.concatenate([g[:, 0:MXU_DIM] for g in gz], axis=-1)
    gz_i = jnp.concatenate([g[:, MXU_DIM:] for g in gz], axis=-1)
    r = _sigmoid(gz_r + br_ref[...])
    i = _sigmoid(gz_i + bi_ref[...])
    nl = -lam_ref[...]
    softplus = jnp.maximum(nl, 0.0) + jnp.log1p(jnp.exp(-jnp.abs(nl)))
    log_a = (-LRU_C) * r * softplus
    a = jnp.exp(log_a)
    u = jnp.sqrt(1.0 - a * a) * (i * xc)

    ap = _phases(a, grp)
    up = _phases(u, grp)
    hloc = [up[0]]
    ploc = [ap[0]]
    for ph in range(1, SUBLANES):
        hloc.append(ap[ph] * hloc[ph - 1] + up[ph])
        ploc.append(ap[ph] * ploc[ph - 1])
    ps = ploc[SUBLANES - 1]
    hs = hloc[SUBLANES - 1]
    rows = lax.broadcasted_iota(jnp.int32, ps.shape, 0)
    d = 1
    while d < grp:
        m = rows >= d
        hs = jnp.where(m, ps * pltpu.roll(hs, d, 0) + hs, hs)
        ps = jnp.where(m, ps * pltpu.roll(ps, d, 0), ps)
        d *= 2
    c0 = hstate_ref[0:1, :]
    after = ps * c0 + hs
    cin = _group_shift(after, c0)
    hstate_ref[...] = jnp.broadcast_to(after[grp - 1:grp], hstate_ref.shape)
    hh = jnp.concatenate([hloc[ph] + ploc[ph] * cin for ph in range(SUBLANES)], axis=0)
    yr = hh * _gelu_tanh(gr)

    y = jnp.concatenate([ya, yr], axis=-1)
    gm = jnp.dot((y * y).astype(BF16), g1_ref[...], preferred_element_type=F32)
    rs = lax.rsqrt(gm + EPS)
    rs_hi = rs.astype(BF16)
    rs_lo = (rs - rs_hi.astype(F32)).astype(BF16)
    rs_full = jnp.dot(jnp.concatenate([rs_hi, rs_lo], axis=-1), e2_ref[...],
                      preferred_element_type=F32)
    yn = y * rs_full * gmix_ref[...]
    out = x + jnp.dot(yn.astype(BF16), wout_ref[...], preferred_element_type=F32)
    if time_major_out:
        _store_time_major(o_ref, out, perm_scr)
    else:
        o_ref[...] = out


def _mixer(h, p, time_major_in, time_major_out):
    b, s, d = h.shape
    ts = MIX_TS
    assert s % ts == 0 and ts % (SUBLANES * SUBLANES) == 0
    d_in = p["w_in"].shape[1]
    d_mix = D_CONV + D_LRU
    const = lambda shape: pl.BlockSpec(shape, lambda bi, j: (0,) * len(shape))
    return pl.pallas_call(
        functools.partial(_mixer_kernel, time_major_in=time_major_in,
                          time_major_out=time_major_out),
        grid=(b, s // ts),
        in_specs=[
            pl.BlockSpec((None, ts, d), lambda bi, j: (bi, j, 0)),
            const((1, d)), const((d, d_in)),
            const((3, D_CONV)), const((1, D_CONV)), const((4, D_LRU)), const((1, D_LRU)),
            const((D_LRU // MXU_DIM, MXU_DIM, 2 * MXU_DIM)),
            const((1, D_LRU)), const((1, D_LRU)), const((1, D_LRU)),
            const((1, d_mix)), const((d_mix, LANES)), const((2 * LANES, d_mix)), const((d_mix, d)),
        ],
        out_specs=pl.BlockSpec((None, ts, d), lambda bi, j: (bi, j, 0)),
        out_shape=jax.ShapeDtypeStruct((b, s, d), F32),
        scratch_shapes=[
            pltpu.VMEM((SUBLANES, D_CONV), F32), pltpu.VMEM((SUBLANES, D_LRU), F32),
            pltpu.VMEM((SUBLANES, D_LRU), F32),
            pltpu.VMEM((d // LANES, ts, LANES), F32),
        ],
        compiler_params=pltpu.CompilerParams(
            dimension_semantics=("arbitrary", "arbitrary"), vmem_limit_bytes=VMEM_LIMIT),
        name="mixer",
    )(h, p["norm_mix_g"], p["w_in"], p["conv_a_w"], p["conv_a_b"], p["conv_b_w"], p["conv_b_b"],
      p["w_gate"], p["b_r"], p["b_i"], p["lam"], p["g_mix"], p["g1"], p["e2"], p["w_out"])


def _swiglu_chunk(xn, w1_ref, w3_ref, w2_ref):
    h1 = jnp.dot(xn, w1_ref[...], preferred_element_type=F32)
    h3 = jnp.dot(xn, w3_ref[...], preferred_element_type=F32)
    g = (h1 * _sigmoid(h1)) * h3
    return jnp.dot(g.astype(BF16), w2_ref[...], preferred_element_type=F32)


def _swiglu_kernel(be_ref, first_ref, nused_ref, x_ref, g_ref, w1_hbm, w3_hbm, w2_hbm, o_ref,
                   xn_scr, acc_scr, res1, res3, res2, st1, st3, st2, sem,
                   *, nf, tf, layer, routed):
    assert nf >= 2
    i = pl.program_id(0)
    f = pl.program_id(1)
    used = i < nused_ref[0]
    tm, d = xn_scr.shape
    e = be_ref[i]

    def chunk_copies(fc):
        cols = pl.ds(pl.multiple_of(fc * tf, tf), tf)
        return (pltpu.make_async_copy(w1_hbm.at[layer, e, :, cols], st1, sem.at[0]),
                pltpu.make_async_copy(w3_hbm.at[layer, e, :, cols], st3, sem.at[1]),
                pltpu.make_async_copy(w2_hbm.at[layer, e, cols, :], st2, sem.at[2]))

    @pl.when(jnp.logical_and(used, first_ref[i] == 1))
    def _():
        @pl.when(f == 0)
        def _():
            for c in chunk_copies(f):
                c.start()

        for c in chunk_copies(f):
            c.wait()
        res1[f] = st1[...].astype(BF16)
        res3[f] = st3[...].astype(BF16)
        res2[f] = st2[...].astype(BF16)

        @pl.when(f + 1 < nf)
        def _():
            for c in chunk_copies(f + 1):
                c.start()

    w_refs = (res1.at[f], res3.at[f], res2.at[f])

    def load_xn():
        if routed:
            return _load_token_major(x_ref, (), tm, d).astype(BF16)
        return _rms_norm_rows(x_ref[...], g_ref[...]).astype(BF16)

    def finish(y):
        if routed:
            _store_token_major(o_ref, (), y)
        else:
            o_ref[...] = x_ref[...] + y

    @pl.when(jnp.logical_and(used, f == 0))
    def _():
        xn = load_xn()
        xn_scr[...] = xn
        acc_scr[...] = _swiglu_chunk(xn, *w_refs)

    @pl.when(jnp.logical_and(used, jnp.logical_and(f > 0, f < nf - 1)))
    def _():
        acc_scr[...] += _swiglu_chunk(xn_scr[...], *w_refs)

    @pl.when(jnp.logical_and(used, f == nf - 1))
    def _():
        finish(acc_scr[...] + _swiglu_chunk(xn_scr[...], *w_refs))

    @pl.when(jnp.logical_and(jnp.logical_not(used), f == nf - 1))
    def _():
        o_ref[...] = jnp.zeros_like(o_ref)


def _swiglu(x, g, block_expert, first, n_used, w1, w3, w2, layer, tm, routed, name):
    d, ff = w1.shape[2], w1.shape[3]
    tf = FFN_TF
    nf = ff // tf
    rows = x.shape[0] // SUBLANES if routed else x.shape[0]
    blk = (tm * SUBLANES, LANES) if routed else (tm, d)
    grid_spec = pltpu.PrefetchScalarGridSpec(
        num_scalar_prefetch=3,
        grid=(rows // tm, nf),
        in_specs=[
            pl.BlockSpec(blk, lambda i, f, *_: (i, 0)),
            pl.BlockSpec((1, d), lambda i, f, *_: (0, 0)),
            pl.BlockSpec(memory_space=pl.ANY),
            pl.BlockSpec(memory_space=pl.ANY),
            pl.BlockSpec(memory_space=pl.ANY),
        ],
        out_specs=pl.BlockSpec(blk, lambda i, f, *_: (i, 0)),
        scratch_shapes=[
            pltpu.VMEM((tm, d), BF16), pltpu.VMEM((tm, d), F32),
            pltpu.VMEM((nf, d, tf), BF16), pltpu.VMEM((nf, d, tf), BF16),
            pltpu.VMEM((nf, tf, d), BF16),
            pltpu.VMEM((d, tf), F32), pltpu.VMEM((d, tf), F32), pltpu.VMEM((tf, d), F32),
            pltpu.SemaphoreType.DMA((3,)),
        ],
    )
    return pl.pallas_call(
        functools.partial(_swiglu_kernel, nf=nf, tf=tf, layer=layer, routed=routed),
        grid_spec=grid_spec,
        out_shape=jax.ShapeDtypeStruct(x.shape, F32),
        compiler_params=pltpu.CompilerParams(
            dimension_semantics=("arbitrary", "arbitrary"), vmem_limit_bytes=VMEM_LIMIT),
        name=name,
    )(block_expert, first, n_used, x, g, w1, w3, w2)


def _dense_ffn(h2, g, w1, w3, w2, j):
    t = h2.shape[0]
    tm = min(FFN_TM, t)
    nb = t // tm
    block_expert = jnp.zeros((nb,), jnp.int32)
    first = jnp.zeros((nb,), jnp.int32).at[0].set(1)
    n_used = jnp.full((1,), nb, jnp.int32)
    return _swiglu(h2, g, block_expert, first, n_used,
                   w1[:, None], w3[:, None], w2[:, None], j, tm, False, "dense_ffn")


def _token_major_index(idx, c, n_tokens):
    return idx + (pl.ds(c, n_tokens, stride=SUBLANES), slice(None))


def _load_token_major(ref, idx, n_tokens, d):
    assert d == SUBLANES * LANES
    return jnp.concatenate(
        [ref[_token_major_index(idx, c, n_tokens)] for c in range(SUBLANES)], axis=-1)


def _store_token_major(ref, idx, value):
    n_tokens, d = value.shape
    assert d == SUBLANES * LANES
    for c in range(SUBLANES):
        ref[_token_major_index(idx, c, n_tokens)] = value[:, c * LANES:(c + 1) * LANES]


def _token_rows(t):
    return pl.ds(pl.multiple_of(t * SUBLANES, SUBLANES), SUBLANES)


def _expert_ffn(xs, g, block_expert, n_used, w1, w3, w2, j):
    first = jnp.concatenate(
        [jnp.ones((1,), jnp.int32), (block_expert[1:] != block_expert[:-1]).astype(jnp.int32)])
    return _swiglu(xs, g, block_expert, first, n_used, w1, w3, w2, j, MOE_TM, True, "expert_ffn")


def _router_kernel(x_ref, g_ref, wcat_ref, eid_ref, gate_ref, rank_ref, cnt_ref, carry_ref):
    tm = x_ref.shape[0]

    @pl.when(pl.program_id(0) == 0)
    def _():
        carry_ref[...] = jnp.zeros_like(carry_ref)

    xn = _rms_norm_rows(x_ref[...], g_ref[...])
    x_hi = xn.astype(BF16)
    x_lo = (xn - x_hi.astype(F32)).astype(BF16)
    nt_dims = (((1,), (1,)), ((), ()))
    wcat = wcat_ref[...]
    l_hi = lax.dot_general(wcat, x_hi, nt_dims, preferred_element_type=F32)
    l_lo = lax.dot_general(wcat, x_lo, nt_dims, preferred_element_type=F32)
    logits = l_hi[0:N_EXPERTS] + l_hi[N_EXPERTS:] + l_lo[0:N_EXPERTS]

    sub = lax.broadcasted_iota(jnp.int32, logits.shape, 0)
    m1 = jnp.max(logits, axis=0, keepdims=True)
    i1 = jnp.min(jnp.where(logits == m1, sub, N_EXPERTS), axis=0, keepdims=True)
    rest = jnp.where(sub == i1, -jnp.inf, logits)
    m2 = jnp.max(rest, axis=0, keepdims=True)
    i2 = jnp.min(jnp.where(rest == m2, sub, N_EXPERTS), axis=0, keepdims=True)
    e2 = jnp.exp(m2 - m1)
    denom = 1.0 + e2
    gate_ref[...] = jnp.concatenate([1.0 / denom, e2 / denom], axis=0)
    eid_ref[...] = jnp.concatenate([i1, i2], axis=0)

    sel1 = sub == i1
    sel2 = sub == i2
    onehot = jnp.where(jnp.logical_or(sel1, sel2), 1.0, 0.0)
    r_io = lax.broadcasted_iota(jnp.int32, (tm, tm), 0)
    c_io = lax.broadcasted_iota(jnp.int32, (tm, tm), 1)
    tri = jnp.where(r_io < c_io, 1.0, 0.0).astype(BF16)
    before = jnp.dot(onehot.astype(BF16), tri, preferred_element_type=F32) + carry_ref[...]
    rank1 = jnp.sum(jnp.where(sel1, before, 0.0), axis=0, keepdims=True)
    rank2 = jnp.sum(jnp.where(sel2, before, 0.0), axis=0, keepdims=True)
    rank_ref[...] = jnp.concatenate([rank1, rank2], axis=0).astype(jnp.int32)
    total = carry_ref[...] + jnp.sum(onehot, axis=1, keepdims=True)
    carry_ref[...] = total
    cnt_ref[...] = total.astype(jnp.int32)


def _router(h2, g, wcat):
    t, d = h2.shape
    tm = min(ROUTE_TM, t)
    pair = lambda dt: jax.ShapeDtypeStruct((TOP_K, t), dt)
    return pl.pallas_call(
        _router_kernel,
        grid=(t // tm,),
        in_specs=[
            pl.BlockSpec((tm, d), lambda i: (i, 0)),
            pl.BlockSpec((1, d), lambda i: (0, 0)),
            pl.BlockSpec((2 * N_EXPERTS, d), lambda i: (0, 0)),
        ],
        out_specs=[
            pl.BlockSpec((TOP_K, tm), lambda i: (0, i)),
            pl.BlockSpec((TOP_K, tm), lambda i: (0, i)),
            pl.BlockSpec((TOP_K, tm), lambda i: (0, i)),
            pl.BlockSpec((N_EXPERTS, 1), lambda i: (0, 0)),
        ],
        out_shape=[pair(jnp.int32), pair(F32), pair(jnp.int32),
                   jax.ShapeDtypeStruct((N_EXPERTS, 1), jnp.int32)],
        scratch_shapes=[pltpu.VMEM((N_EXPERTS, 1), F32)],
        compiler_params=pltpu.CompilerParams(dimension_semantics=("arbitrary",)),
        name="router",
    )(h2, g, wcat)


def _tile_indices(idx, tm):
    k, t = idx.shape
    return idx.reshape(k, t // tm, tm).transpose(1, 0, 2).reshape(t // tm, 1, k * tm)


def _start_row_copies(n_rows, row_copy):
    def body(c, carry):
        for u in range(ISSUE_UNROLL):
            for k in range(TOP_K):
                row_copy(c * ISSUE_UNROLL + u, k).start(priority=k)
        return carry

    lax.fori_loop(0, n_rows // ISSUE_UNROLL, body, 0)


def _dispatch_kernel(zrow_ref, dest_ref, x_ref, g_ref, xs_ref, xn_scr, zero_scr, sem, zsem):
    i = pl.program_id(0)
    tm = x_ref.shape[0]
    slot = i % 2

    @pl.when(i == 0)
    def _():
        zero_scr[...] = jnp.zeros_like(zero_scr)
        zcopies = [
            pltpu.make_async_copy(
                zero_scr,
                xs_ref.at[pl.ds(pl.multiple_of(jnp.maximum(zrow_ref[n], 0) * SUBLANES, MOE_TM),
                                MOE_TM * SUBLANES)],
                zsem)
            for n in range(2 * N_EXPERTS)]
        for n, c in enumerate(zcopies):
            pl.when(zrow_ref[n] >= 0)(c.start)
        for n, c in enumerate(zcopies):
            pl.when(zrow_ref[n] >= 0)(c.wait)

    _store_token_major(xn_scr, (slot,), _rms_norm_rows(x_ref[...], g_ref[...]))

    def row_copy(t, k):
        d = dest_ref[0, 0, k * tm + t]
        return pltpu.make_async_copy(
            xn_scr.at[slot, _token_rows(t)], xs_ref.at[_token_rows(d)], sem.at[slot])

    _start_row_copies(tm, row_copy)

    def wait_slot(s):
        for _ in range(TOP_K):
            pltpu.make_async_copy(
                xn_scr.at[s], xs_ref.at[pl.ds(0, tm * SUBLANES)], sem.at[s]).wait()

    @pl.when(i > 0)
    def _():
        wait_slot(1 - slot)

    @pl.when(i == pl.num_programs(0) - 1)
    def _():
        wait_slot(slot)


def _dispatch(h2, g, dest, zrows, cap):
    t, d = h2.shape
    tm = min(DISPATCH_TM, t)
    dest3 = _tile_indices(dest, tm)
    grid_spec = pltpu.PrefetchScalarGridSpec(
        num_scalar_prefetch=1,
        grid=(t // tm,),
        in_specs=[
            pl.BlockSpec((1, 1, TOP_K * tm), lambda i, z: (i, 0, 0), memory_space=pltpu.SMEM),
            pl.BlockSpec((tm, d), lambda i, z: (i, 0)),
            pl.BlockSpec((1, d), lambda i, z: (0, 0)),
        ],
        out_specs=pl.BlockSpec(memory_space=pl.ANY),
        scratch_shapes=[pltpu.VMEM((2, tm * SUBLANES, LANES), F32),
                        pltpu.VMEM((MOE_TM * SUBLANES, LANES), F32),
                        pltpu.SemaphoreType.DMA((2,)), pltpu.SemaphoreType.DMA],
    )
    return pl.pallas_call(
        _dispatch_kernel,
        grid_spec=grid_spec,
        out_shape=jax.ShapeDtypeStruct((cap * SUBLANES, LANES), F32),
        compiler_params=pltpu.CompilerParams(
            dimension_semantics=("arbitrary",), vmem_limit_bytes=VMEM_LIMIT),
        name="dispatch",
    )(zrows, dest3, h2, g)


def _combine_kernel(pos_ref, posn_ref, h_ref, gate_ref, fg_ref, ys_ref, o_ref, buf, sem,
                    *, final_norm):
    i = pl.program_id(0)
    n = pl.num_programs(0)
    tm = h_ref.shape[0]
    slot = i % 2

    def start_gather(p_ref, s):
        def row_copy(t, k):
            p = p_ref[0, 0, k * tm + t]
            return pltpu.make_async_copy(
                ys_ref.at[_token_rows(p)], buf.at[s, k, _token_rows(t)], sem.at[s])
        _start_row_copies(tm, row_copy)

    @pl.when(i == 0)
    def _():
        start_gather(pos_ref, slot)

    @pl.when(i + 1 < n)
    def _():
        start_gather(posn_ref, 1 - slot)

    for k in range(TOP_K):
        pltpu.make_async_copy(
            ys_ref.at[pl.ds(0, tm * SUBLANES)], buf.at[slot, k], sem.at[slot]).wait()
    gate = gate_ref[...]
    d = h_ref.shape[1]
    out = (h_ref[...] + gate[:, 0:1] * _load_token_major(buf, (slot, 0), tm, d)
           + gate[:, 1:2] * _load_token_major(buf, (slot, 1), tm, d))
    if final_norm:
        out = _rms_norm_rows(out, fg_ref[...])
    o_ref[...] = out


def _combine(h2, gates, pos, ys, final_g, final_norm):
    t, d = h2.shape
    tm = min(COMBINE_TM, t)
    nt = t // tm
    pos3 = _tile_indices(pos, tm)
    return pl.pallas_call(
        functools.partial(_combine_kernel, final_norm=final_norm),
        grid=(nt,),
        in_specs=[
            pl.BlockSpec((1, 1, TOP_K * tm), lambda i: (i, 0, 0), memory_space=pltpu.SMEM),
            pl.BlockSpec((1, 1, TOP_K * tm), lambda i: (jnp.minimum(i + 1, nt - 1), 0, 0),
                         memory_space=pltpu.SMEM),
            pl.BlockSpec((tm, d), lambda i: (i, 0)),
            pl.BlockSpec((tm, TOP_K), lambda i: (i, 0)),
            pl.BlockSpec((1, d), lambda i: (0, 0)),
            pl.BlockSpec(memory_space=pl.ANY),
        ],
        out_specs=pl.BlockSpec((tm, d), lambda i: (i, 0)),
        out_shape=jax.ShapeDtypeStruct((t, d), F32),
        scratch_shapes=[pltpu.VMEM((2, TOP_K, tm * SUBLANES, LANES), F32),
                        pltpu.SemaphoreType.DMA((2,))],
        compiler_params=pltpu.CompilerParams(
            dimension_semantics=("arbitrary",), vmem_limit_bytes=VMEM_LIMIT),
        name="combine",
    )(pos3, pos3, h2, gates, final_g, ys)


def _moe(h2, g, wcat, w1, w3, w2, j, final_g, final_norm):
    t, d = h2.shape
    eid, gates, rank, counts = _router(h2, g, wcat)
    n_assign = t * TOP_K
    n_blocks = -(-n_assign // MOE_TM) + N_EXPERTS
    cap = n_blocks * MOE_TM
    counts = counts[:, 0]
    padded = (counts + MOE_TM - 1) // MOE_TM * MOE_TM
    pad_end = jnp.cumsum(padded)
    pad_start = pad_end - padded
    experts = jnp.arange(N_EXPERTS, dtype=jnp.int32)[:, None, None]
    dest = rank + jnp.sum(jnp.where(eid[None] == experts, pad_start[:, None, None], 0), axis=0)
    block_start = jnp.arange(n_blocks, dtype=jnp.int32) * MOE_TM
    block_expert = jnp.minimum(
        jnp.sum(block_start[:, None] >= pad_end[None, :], axis=1), N_EXPERTS - 1
    ).astype(jnp.int32)
    n_used = (pad_end[-1:] // MOE_TM).astype(jnp.int32)
    tail_rows = pad_end[-1] + jnp.arange(N_EXPERTS, dtype=jnp.int32) * MOE_TM
    zrows = jnp.concatenate([
        jnp.where(padded > 0, pad_end - MOE_TM, -1),
        jnp.where(tail_rows < cap, tail_rows, -1),
    ]).astype(jnp.int32)
    xs = _dispatch(h2, g, dest, zrows, cap)
    ys = _expert_ffn(xs, g, block_expert, n_used, w1, w3, w2, j)
    return _combine(h2, gates.T, dest, ys, final_g, final_norm)


def _gate_blocks(w_r, w_i):
    nh, hd, _ = w_r.shape
    per = MXU_DIM // hd
    eye = jnp.eye(per, dtype=w_r.dtype)

    def blocks(w):
        wb = w.reshape(nh // per, per, hd, hd)
        return jnp.einsum("cpde,pq->cpdqe", wb, eye).reshape(nh // per, MXU_DIM, MXU_DIM)

    return jnp.concatenate([blocks(w_r), blocks(w_i)], axis=-1)


def _group_matrices(d_mix):
    grp = jnp.arange(d_mix) // HEAD_DIM
    cols = jnp.arange(LANES)
    g1 = jnp.where(grp[:, None] == cols[None, :], 1.0 / HEAD_DIM, 0.0).astype(BF16)
    e = jnp.where(cols[:, None] == grp[None, :], 1.0, 0.0).astype(BF16)
    return g1, jnp.concatenate([e, e], axis=0)


def kernel(x, norm_mix_g, w_in, conv_a_w, conv_a_b, conv_b_w, conv_b_b, lru_w_r, lru_b_r,
           lru_w_i, lru_b_i, lru_lambda, mix_out_g, w_out, norm_ffn_g, ffn_w1, ffn_w3, ffn_w2,
           router_w, expert_w1, expert_w3, expert_w2, final_g):
    b, s, d = x.shape
    depth = w_in.shape[0]
    assert depth % 2 == 0
    g1, e2 = _group_matrices(D_CONV + D_LRU)
    row = lambda v: v.reshape(1, -1)
    h = x
    for layer in range(depth):
        p = {
            "norm_mix_g": row(norm_mix_g[layer]),
            "w_in": w_in[layer].astype(BF16),
            "conv_a_w": conv_a_w[layer], "conv_a_b": row(conv_a_b[layer]),
            "conv_b_w": conv_b_w[layer], "conv_b_b": row(conv_b_b[layer]),
            "w_gate": _gate_blocks(lru_w_r[layer], lru_w_i[layer]).astype(BF16),
            "b_r": row(lru_b_r[layer]), "b_i": row(lru_b_i[layer]), "lam": row(lru_lambda[layer]),
            "g_mix": row(mix_out_g[layer]), "g1": g1, "e2": e2,
            "w_out": w_out[layer].astype(BF16),
        }
        h = _mixer(h, p, layer == 0, layer == depth - 1)
        h2 = h.reshape(b * s, d)
        gf = row(norm_ffn_g[layer])
        j = layer // 2
        if layer % 2 == 0:
            h2 = _dense_ffn(h2, gf, ffn_w1, ffn_w3, ffn_w2, j)
        else:
            w_hi = router_w[j].astype(BF16)
            w_lo = (router_w[j] - w_hi.astype(F32)).astype(BF16)
            h2 = _moe(h2, gf, jnp.concatenate([w_hi.T, w_lo.T], axis=0),
                      expert_w1, expert_w3, expert_w2, j, row(final_g), layer == depth - 1)
        h = h2.reshape(b, s, d)
    return h
```

```python
import functools

import jax
import jax.numpy as jnp
from jax import lax
from jax.experimental import pallas as pl
from jax.experimental.pallas import tpu as pltpu

EPS = 1e-6
HEAD_DIM = 64
D_CONV = 512
D_LRU = 512
LRU_C = 8.0
N_EXPERTS = 8
TOP_K = 2

SUBLANES = 8
LANES = 128
MXU_DIM = 256

MIX_TS = 1024
FFN_TM = 1024
FFN_TF = 512
MOE_TM = 768
MOE_ROW_STEP = 256
ROUTE_TM = 512
DISPATCH_TM = 1024
COMBINE_TM = 512
ISSUE_UNROLL = 8
VMEM_LIMIT = 56 * 1024 * 1024

F32 = jnp.float32
BF16 = jnp.bfloat16


def _rms_norm_rows(x, g):
    return x * lax.rsqrt(jnp.mean(x * x, axis=-1, keepdims=True) + EPS) * g


def _sigmoid(x):
    return 1.0 / (1.0 + jnp.exp(-x))


def _gelu_tanh(x):
    return 0.5 * x * (1.0 + jnp.tanh(0.7978845608028654 * (x + 0.044715 * (x * x * x))))


def _phases(v, g):
    return [v[r * g:(r + 1) * g] for r in range(SUBLANES)]


def _group_shift(v, first_row):
    rolled = pltpu.roll(v, 1, 0)
    rows = lax.broadcasted_iota(jnp.int32, (SUBLANES, v.shape[1]), 0)
    head = jnp.where(rows == 0, jnp.broadcast_to(first_row, (SUBLANES, v.shape[1])),
                     rolled[0:SUBLANES])
    return jnp.concatenate([head, rolled[SUBLANES:]], axis=0)


def _causal_conv(phases, tail, w_ref, b_ref):
    taps = w_ref.shape[0]
    bias = b_ref[...]
    shifted = {}

    def phase(idx):
        if idx >= 0:
            return phases[idx]
        if idx not in shifted:
            src = idx + SUBLANES
            shifted[idx] = _group_shift(phases[src], tail[src:src + 1])
        return shifted[idx]

    out = []
    for r in range(SUBLANES):
        acc = w_ref[taps - 1:taps, :] * phases[r] + bias
        for k in range(taps - 1):
            acc = acc + w_ref[k:k + 1, :] * phase(r - (taps - 1) + k)
        out.append(acc)
    return out


def _last_group_rows(phases):
    return jnp.concatenate([p[p.shape[0] - 1:p.shape[0]] for p in phases], axis=0)


def _to_phase_major(src_ref, perm_scr):
    ts, d = src_ref.shape
    grp = ts // SUBLANES
    for c in range(d // LANES):
        perm_scr[c] = src_ref[:, c * LANES:(c + 1) * LANES]
    return jnp.concatenate(
        [jnp.concatenate([perm_scr[c, pl.ds(r, grp, stride=SUBLANES), :]
                          for c in range(d // LANES)], axis=-1)
         for r in range(SUBLANES)], axis=0)


def _store_time_major(dst_ref, value, perm_scr):
    ts, d = value.shape
    grp = ts // SUBLANES
    for c in range(d // LANES):
        for ph in range(SUBLANES):
            perm_scr[c, pl.ds(ph, grp, stride=SUBLANES), :] = (
                value[ph * grp:(ph + 1) * grp, c * LANES:(c + 1) * LANES])
    dst_ref[...] = jnp.concatenate([perm_scr[c] for c in range(d // LANES)], axis=-1)


def _mixer_kernel(h_ref, gn_ref, win_ref, caw_ref, cab_ref, cbw_ref, cbb_ref, wg_ref,
                  br_ref, bi_ref, lam_ref, gmix_ref, g1_ref, e2_ref, wout_ref,
                  o_ref, taila_ref, tailb_ref, hstate_ref, perm_scr,
                  *, time_major_in, time_major_out):
    ts = h_ref.shape[0]
    grp = ts // SUBLANES

    @pl.when(pl.program_id(1) == 0)
    def _():
        taila_ref[...] = jnp.zeros_like(taila_ref)
        tailb_ref[...] = jnp.zeros_like(tailb_ref)
        hstate_ref[...] = jnp.zeros_like(hstate_ref)

    x = _to_phase_major(h_ref, perm_scr) if time_major_in else h_ref[...]
    xn = _rms_norm_rows(x, gn_ref[...])
    z = jnp.dot(xn.astype(BF16), win_ref[...], preferred_element_type=F32)
    xa = z[:, 0:D_CONV]
    gb = z[:, D_CONV:2 * D_CONV]
    gc = z[:, 2 * D_CONV:3 * D_CONV]
    xr = z[:, 3 * D_CONV:3 * D_CONV + D_LRU]
    gr = z[:, 3 * D_CONV + D_LRU:]

    ua = _phases(gc * xa, grp)
    ca = jnp.concatenate(_causal_conv(ua, taila_ref[...], caw_ref, cab_ref), axis=0)
    taila_ref[...] = _last_group_rows(ua)
    ya = gb * ca

    xrp = _phases(xr, grp)
    xc = jnp.concatenate(_causal_conv(xrp, tailb_ref[...], cbw_ref, cbb_ref), axis=0)
    tailb_ref[...] = _last_group_rows(xrp)

    xcb = xc.astype(BF16)
    gz = [jnp.dot(xcb[:, c * MXU_DIM:(c + 1) * MXU_DIM], wg_ref[c], preferred_element_type=F32)
          for c in range(D_LRU // MXU_DIM)]
    gz_r = jnp.concatenate([g[:, 0:MXU_DIM] for g in gz], axis=-1)
    gz_i = jnp.concatenate([g[:, MXU_DIM:] for g in gz], axis=-1)
    r = _sigmoid(gz_r + br_ref[...])
    i = _sigmoid(gz_i + bi_ref[...])
    nl = -lam_ref[...]
    softplus = jnp.maximum(nl, 0.0) + jnp.log1p(jnp.exp(-jnp.abs(nl)))
    log_a = (-LRU_C) * r * softplus
    a = jnp.exp(log_a)
    u = jnp.sqrt(1.0 - a * a) * (i * xc)

    ap = _phases(a, grp)
    up = _phases(u, grp)
    hloc = [up[0]]
    ploc = [ap[0]]
    for ph in range(1, SUBLANES):
        hloc.append(ap[ph] * hloc[ph - 1] + up[ph])
        ploc.append(ap[ph] * ploc[ph - 1])
    ps = ploc[SUBLANES - 1]
    hs = hloc[SUBLANES - 1]
    rows = lax.broadcasted_iota(jnp.int32, ps.shape, 0)
    d = 1
    while d < grp:
        m = rows >= d
        hs = jnp.where(m, ps * pltpu.roll(hs, d, 0) + hs, hs)
        ps = jnp.where(m, ps * pltpu.roll(ps, d, 0), ps)
        d *= 2
    c0 = hstate_ref[0:1, :]
    after = ps * c0 + hs
    cin = _group_shift(after, c0)
    hstate_ref[...] = jnp.broadcast_to(after[grp - 1:grp], hstate_ref.shape)
    hh = jnp.concatenate([hloc[ph] + ploc[ph] * cin for ph in range(SUBLANES)], axis=0)
    yr = hh * _gelu_tanh(gr)

    y = jnp.concatenate([ya, yr], axis=-1)
    gm = jnp.dot((y * y).astype(BF16), g1_ref[...], preferred_element_type=F32)
    rs = lax.rsqrt(gm + EPS)
    rs_hi = rs.astype(BF16)
    rs_lo = (rs - rs_hi.astype(F32)).astype(BF16)
    rs_full = jnp.dot(jnp.concatenate([rs_hi, rs_lo], axis=-1), e2_ref[...],
                      preferred_element_type=F32)
    yn = y * rs_full * gmix_ref[...]
    out = x + jnp.dot(yn.astype(BF16), wout_ref[...], preferred_element_type=F32)
    if time_major_out:
        _store_time_major(o_ref, out, perm_scr)
    else:
        o_ref[...] = out


def _mixer(h, p, time_major_in, time_major_out):
    b, s, d = h.shape
    ts = MIX_TS
    assert s % ts == 0 and ts % (SUBLANES * SUBLANES) == 0
    d_in = p["w_in"].shape[1]
    d_mix = D_CONV + D_LRU
    const = lambda shape: pl.BlockSpec(shape, lambda bi, j: (0,) * len(shape))
    return pl.pallas_call(
        functools.partial(_mixer_kernel, time_major_in=time_major_in,
                          time_major_out=time_major_out),
        grid=(b, s // ts),
        in_specs=[
            pl.BlockSpec((None, ts, d), lambda bi, j: (bi, j, 0)),
            const((1, d)), const((d, d_in)),
            const((3, D_CONV)), const((1, D_CONV)), const((4, D_LRU)), const((1, D_LRU)),
            const((D_LRU // MXU_DIM, MXU_DIM, 2 * MXU_DIM)),
            const((1, D_LRU)), const((1, D_LRU)), const((1, D_LRU)),
            const((1, d_mix)), const((d_mix, LANES)), const((2 * LANES, d_mix)), const((d_mix, d)),
        ],
        out_specs=pl.BlockSpec((None, ts, d), lambda bi, j: (bi, j, 0)),
        out_shape=jax.ShapeDtypeStruct((b, s, d), F32),
        scratch_shapes=[
            pltpu.VMEM((SUBLANES, D_CONV), F32), pltpu.VMEM((SUBLANES, D_LRU), F32),
            pltpu.VMEM((SUBLANES, D_LRU), F32),
            pltpu.VMEM((d // LANES, ts, LANES), F32),
        ],
        compiler_params=pltpu.CompilerParams(
            dimension_semantics=("arbitrary", "arbitrary"), vmem_limit_bytes=VMEM_LIMIT),
        name="mixer",
    )(h, p["norm_mix_g"], p["w_in"], p["conv_a_w"], p["conv_a_b"], p["conv_b_w"], p["conv_b_b"],
      p["w_gate"], p["b_r"], p["b_i"], p["lam"], p["g_mix"], p["g1"], p["e2"], p["w_out"])


def _swiglu_chunk(xn, w1_ref, w3_ref, w2_ref):
    h1 = jnp.dot(xn, w1_ref[...], preferred_element_type=F32)
    h3 = jnp.dot(xn, w3_ref[...], preferred_element_type=F32)
    g = (h1 * _sigmoid(h1)) * h3
    return jnp.dot(g.astype(BF16), w2_ref[...], preferred_element_type=F32)


def _swiglu_kernel(be_ref, first_ref, valid_ref, x_ref, g_ref, w1_hbm, w3_hbm, w2_hbm, o_ref,
                   xn_scr, acc_scr, res1, res3, res2, st1, st3, st2, sem,
                   *, nf, tf, layer, routed, row_paths):
    assert nf >= 2
    i = pl.program_id(0)
    f = pl.program_id(1)
    valid = valid_ref[i]
    used = valid > 0
    tm, d = xn_scr.shape
    assert row_paths[-1] == tm
    e = be_ref[i]

    def chunk_copies(fc):
        cols = pl.ds(pl.multiple_of(fc * tf, tf), tf)
        return (pltpu.make_async_copy(w1_hbm.at[layer, e, :, cols], st1, sem.at[0]),
                pltpu.make_async_copy(w3_hbm.at[layer, e, :, cols], st3, sem.at[1]),
                pltpu.make_async_copy(w2_hbm.at[layer, e, cols, :], st2, sem.at[2]))

    @pl.when(jnp.logical_and(used, first_ref[i] == 1))
    def _():
        @pl.when(f == 0)
        def _():
            for c in chunk_copies(f):
                c.start()

        for c in chunk_copies(f):
            c.wait()
        res1[f] = st1[...].astype(BF16)
        res3[f] = st3[...].astype(BF16)
        res2[f] = st2[...].astype(BF16)

        @pl.when(f + 1 < nf)
        def _():
            for c in chunk_copies(f + 1):
                c.start()

    w_refs = (res1.at[f], res3.at[f], res2.at[f])

    def load_xn(m):
        if routed:
            return _load_token_major(x_ref, (), m, d).astype(BF16)
        return _rms_norm_rows(x_ref[...], g_ref[...]).astype(BF16)

    def finish(y, m):
        if routed:
            _store_token_major(o_ref, (), y)
            if m < tm:
                o_ref[m * SUBLANES:, :] = jnp.zeros(((tm - m) * SUBLANES, LANES), F32)
        else:
            o_ref[...] = x_ref[...] + y

    def row_path(m, in_path):
        @pl.when(jnp.logical_and(in_path, f == 0))
        def _():
            xn = load_xn(m)
            xn_scr[0:m] = xn
            acc_scr[0:m] = _swiglu_chunk(xn, *w_refs)

        @pl.when(jnp.logical_and(in_path, jnp.logical_and(f > 0, f < nf - 1)))
        def _():
            acc_scr[0:m] += _swiglu_chunk(xn_scr[0:m], *w_refs)

        @pl.when(jnp.logical_and(in_path, f == nf - 1))
        def _():
            finish(acc_scr[0:m] + _swiglu_chunk(xn_scr[0:m], *w_refs), m)

    lo = 0
    for m in row_paths:
        row_path(m, jnp.logical_and(valid > lo, valid <= m))
        lo = m

    @pl.when(jnp.logical_and(jnp.logical_not(used), f == nf - 1))
    def _():
        o_ref[...] = jnp.zeros_like(o_ref)


def _swiglu(x, g, block_expert, first, valid, w1, w3, w2, layer, tm, row_paths, routed, name):
    d, ff = w1.shape[2], w1.shape[3]
    tf = FFN_TF
    nf = ff // tf
    rows = x.shape[0] // SUBLANES if routed else x.shape[0]
    blk = (tm * SUBLANES, LANES) if routed else (tm, d)
    grid_spec = pltpu.PrefetchScalarGridSpec(
        num_scalar_prefetch=3,
        grid=(rows // tm, nf),
        in_specs=[
            pl.BlockSpec(blk, lambda i, f, *_: (i, 0)),
            pl.BlockSpec((1, d), lambda i, f, *_: (0, 0)),
            pl.BlockSpec(memory_space=pl.ANY),
            pl.BlockSpec(memory_space=pl.ANY),
            pl.BlockSpec(memory_space=pl.ANY),
        ],
        out_specs=pl.BlockSpec(blk, lambda i, f, *_: (i, 0)),
        scratch_shapes=[
            pltpu.VMEM((tm, d), BF16), pltpu.VMEM((tm, d), F32),
            pltpu.VMEM((nf, d, tf), BF16), pltpu.VMEM((nf, d, tf), BF16),
            pltpu.VMEM((nf, tf, d), BF16),
            pltpu.VMEM((d, tf), F32), pltpu.VMEM((d, tf), F32), pltpu.VMEM((tf, d), F32),
            pltpu.SemaphoreType.DMA((3,)),
        ],
    )
    return pl.pallas_call(
        functools.partial(_swiglu_kernel, nf=nf, tf=tf, layer=layer, routed=routed,
                          row_paths=row_paths),
        grid_spec=grid_spec,
        out_shape=jax.ShapeDtypeStruct(x.shape, F32),
        compiler_params=pltpu.CompilerParams(
            dimension_semantics=("arbitrary", "arbitrary"), vmem_limit_bytes=VMEM_LIMIT),
        name=name,
    )(block_expert, first, valid, x, g, w1, w3, w2)


def _dense_ffn(h2, g, w1, w3, w2, j):
    t = h2.shape[0]
    tm = min(FFN_TM, t)
    nb = t // tm
    block_expert = jnp.zeros((nb,), jnp.int32)
    first = jnp.zeros((nb,), jnp.int32).at[0].set(1)
    valid = jnp.full((nb,), tm, jnp.int32)
    return _swiglu(h2, g, block_expert, first, valid,
                   w1[:, None], w3[:, None], w2[:, None], j, tm, (tm,), False, "dense_ffn")


def _token_major_index(idx, c, n_tokens):
    return idx + (pl.ds(c, n_tokens, stride=SUBLANES), slice(None))


def _load_token_major(ref, idx, n_tokens, d):
    assert d == SUBLANES * LANES
    return jnp.concatenate(
        [ref[_token_major_index(idx, c, n_tokens)] for c in range(SUBLANES)], axis=-1)


def _store_token_major(ref, idx, value):
    n_tokens, d = value.shape
    assert d == SUBLANES * LANES
    for c in range(SUBLANES):
        ref[_token_major_index(idx, c, n_tokens)] = value[:, c * LANES:(c + 1) * LANES]


def _token_rows(t):
    return pl.ds(pl.multiple_of(t * SUBLANES, SUBLANES), SUBLANES)


def _expert_ffn(xs, g, block_expert, valid, w1, w3, w2, j):
    first = jnp.concatenate(
        [jnp.ones((1,), jnp.int32), (block_expert[1:] != block_expert[:-1]).astype(jnp.int32)])
    row_paths = tuple(range(MOE_ROW_STEP, MOE_TM + 1, MOE_ROW_STEP))
    return _swiglu(xs, g, block_expert, first, valid, w1, w3, w2, j, MOE_TM, row_paths, True,
                   "expert_ffn")


def _router_kernel(x_ref, g_ref, wcat_ref, eid_ref, gate_ref, rank_ref, cnt_ref, carry_ref):
    tm = x_ref.shape[0]

    @pl.when(pl.program_id(0) == 0)
    def _():
        carry_ref[...] = jnp.zeros_like(carry_ref)

    xn = _rms_norm_rows(x_ref[...], g_ref[...])
    x_hi = xn.astype(BF16)
    x_lo = (xn - x_hi.astype(F32)).astype(BF16)
    nt_dims = (((1,), (1,)), ((), ()))
    wcat = wcat_ref[...]
    l_hi = lax.dot_general(wcat, x_hi, nt_dims, preferred_element_type=F32)
    l_lo = lax.dot_general(wcat, x_lo, nt_dims, preferred_element_type=F32)
    logits = l_hi[0:N_EXPERTS] + l_hi[N_EXPERTS:] + l_lo[0:N_EXPERTS]

    sub = lax.broadcasted_iota(jnp.int32, logits.shape, 0)
    m1 = jnp.max(logits, axis=0, keepdims=True)
    i1 = jnp.min(jnp.where(logits == m1, sub, N_EXPERTS), axis=0, keepdims=True)
    rest = jnp.where(sub == i1, -jnp.inf, logits)
    m2 = jnp.max(rest, axis=0, keepdims=True)
    i2 = jnp.min(jnp.where(rest == m2, sub, N_EXPERTS), axis=0, keepdims=True)
    e2 = jnp.exp(m2 - m1)
    denom = 1.0 + e2
    gate_ref[...] = jnp.concatenate([1.0 / denom, e2 / denom], axis=0)
    eid_ref[...] = jnp.concatenate([i1, i2], axis=0)

    sel1 = sub == i1
    sel2 = sub == i2
    onehot = jnp.where(jnp.logical_or(sel1, sel2), 1.0, 0.0)
    r_io = lax.broadcasted_iota(jnp.int32, (tm, tm), 0)
    c_io = lax.broadcasted_iota(jnp.int32, (tm, tm), 1)
    tri = jnp.where(r_io < c_io, 1.0, 0.0).astype(BF16)
    before = jnp.dot(onehot.astype(BF16), tri, preferred_element_type=F32) + carry_ref[...]
    rank1 = jnp.sum(jnp.where(sel1, before, 0.0), axis=0, keepdims=True)
    rank2 = jnp.sum(jnp.where(sel2, before, 0.0), axis=0, keepdims=True)
    rank_ref[...] = jnp.concatenate([rank1, rank2], axis=0).astype(jnp.int32)
    total = carry_ref[...] + jnp.sum(onehot, axis=1, keepdims=True)
    carry_ref[...] = total
    cnt_ref[...] = total.astype(jnp.int32)


def _router(h2, g, wcat):
    t, d = h2.shape
    tm = min(ROUTE_TM, t)
    pair = lambda dt: jax.ShapeDtypeStruct((TOP_K, t), dt)
    return pl.pallas_call(
        _router_kernel,
        grid=(t // tm,),
        in_specs=[
            pl.BlockSpec((tm, d), lambda i: (i, 0)),
            pl.BlockSpec((1, d), lambda i: (0, 0)),
            pl.BlockSpec((2 * N_EXPERTS, d), lambda i: (0, 0)),
        ],
        out_specs=[
            pl.BlockSpec((TOP_K, tm), lambda i: (0, i)),
            pl.BlockSpec((TOP_K, tm), lambda i: (0, i)),
            pl.BlockSpec((TOP_K, tm), lambda i: (0, i)),
            pl.BlockSpec((N_EXPERTS, 1), lambda i: (0, 0)),
        ],
        out_shape=[pair(jnp.int32), pair(F32), pair(jnp.int32),
                   jax.ShapeDtypeStruct((N_EXPERTS, 1), jnp.int32)],
        scratch_shapes=[pltpu.VMEM((N_EXPERTS, 1), F32)],
        compiler_params=pltpu.CompilerParams(dimension_semantics=("arbitrary",)),
        name="router",
    )(h2, g, wcat)


def _tile_indices(idx, tm):
    k, t = idx.shape
    return idx.reshape(k, t // tm, tm).transpose(1, 0, 2).reshape(t // tm, 1, k * tm)


def _start_row_copies(n_rows, row_copy):
    def body(c, carry):
        for u in range(ISSUE_UNROLL):
            for k in range(TOP_K):
                row_copy(c * ISSUE_UNROLL + u, k).start(priority=k)
        return carry

    lax.fori_loop(0, n_rows // ISSUE_UNROLL, body, 0)


def _dispatch_kernel(zrow_ref, dest_ref, x_ref, g_ref, xs_ref, xn_scr, zero_scr, sem, zsem):
    i = pl.program_id(0)
    tm = x_ref.shape[0]
    slot = i % 2

    @pl.when(i == 0)
    def _():
        zero_scr[...] = jnp.zeros_like(zero_scr)
        zcopies = [
            pltpu.make_async_copy(
                zero_scr,
                xs_ref.at[pl.ds(pl.multiple_of(jnp.maximum(zrow_ref[n], 0) * SUBLANES, MOE_TM),
                                MOE_TM * SUBLANES)],
                zsem)
            for n in range(2 * N_EXPERTS)]
        for n, c in enumerate(zcopies):
            pl.when(zrow_ref[n] >= 0)(c.start)
        for n, c in enumerate(zcopies):
            pl.when(zrow_ref[n] >= 0)(c.wait)

    _store_token_major(xn_scr, (slot,), _rms_norm_rows(x_ref[...], g_ref[...]))

    def row_copy(t, k):
        d = dest_ref[0, 0, k * tm + t]
        return pltpu.make_async_copy(
            xn_scr.at[slot, _token_rows(t)], xs_ref.at[_token_rows(d)], sem.at[slot])

    _start_row_copies(tm, row_copy)

    def wait_slot(s):
        for _ in range(TOP_K):
            pltpu.make_async_copy(
                xn_scr.at[s], xs_ref.at[pl.ds(0, tm * SUBLANES)], sem.at[s]).wait()

    @pl.when(i > 0)
    def _():
        wait_slot(1 - slot)

    @pl.when(i == pl.num_programs(0) - 1)
    def _():
        wait_slot(slot)


def _dispatch(h2, g, dest, zrows, cap):
    t, d = h2.shape
    tm = min(DISPATCH_TM, t)
    dest3 = _tile_indices(dest, tm)
    grid_spec = pltpu.PrefetchScalarGridSpec(
        num_scalar_prefetch=1,
        grid=(t // tm,),
        in_specs=[
            pl.BlockSpec((1, 1, TOP_K * tm), lambda i, z: (i, 0, 0), memory_space=pltpu.SMEM),
            pl.BlockSpec((tm, d), lambda i, z: (i, 0)),
            pl.BlockSpec((1, d), lambda i, z: (0, 0)),
        ],
        out_specs=pl.BlockSpec(memory_space=pl.ANY),
        scratch_shapes=[pltpu.VMEM((2, tm * SUBLANES, LANES), F32),
                        pltpu.VMEM((MOE_TM * SUBLANES, LANES), F32),
                        pltpu.SemaphoreType.DMA((2,)), pltpu.SemaphoreType.DMA],
    )
    return pl.pallas_call(
        _dispatch_kernel,
        grid_spec=grid_spec,
        out_shape=jax.ShapeDtypeStruct((cap * SUBLANES, LANES), F32),
        compiler_params=pltpu.CompilerParams(
            dimension_semantics=("arbitrary",), vmem_limit_bytes=VMEM_LIMIT),
        name="dispatch",
    )(zrows, dest3, h2, g)


def _combine_kernel(pos_ref, posn_ref, h_ref, gate_ref, fg_ref, ys_ref, o_ref, buf, sem,
                    *, final_norm):
    i = pl.program_id(0)
    n = pl.num_programs(0)
    tm = h_ref.shape[0]
    slot = i % 2

    def start_gather(p_ref, s):
        def row_copy(t, k):
            p = p_ref[0, 0, k * tm + t]
            return pltpu.make_async_copy(
                ys_ref.at[_token_rows(p)], buf.at[s, k, _token_rows(t)], sem.at[s])
        _start_row_copies(tm, row_copy)

    @pl.when(i == 0)
    def _():
        start_gather(pos_ref, slot)

    @pl.when(i + 1 < n)
    def _():
        start_gather(posn_ref, 1 - slot)

    for k in range(TOP_K):
        pltpu.make_async_copy(
            ys_ref.at[pl.ds(0, tm * SUBLANES)], buf.at[slot, k], sem.at[slot]).wait()
    gate = gate_ref[...]
    d = h_ref.shape[1]
    out = (h_ref[...] + gate[:, 0:1] * _load_token_major(buf, (slot, 0), tm, d)
           + gate[:, 1:2] * _load_token_major(buf, (slot, 1), tm, d))
    if final_norm:
        out = _rms_norm_rows(out, fg_ref[...])
    o_ref[...] = out


def _combine(h2, gates, pos, ys, final_g, final_norm):
    t, d = h2.shape
    tm = min(COMBINE_TM, t)
    nt = t // tm
    pos3 = _tile_indices(pos, tm)
    return pl.pallas_call(
        functools.partial(_combine_kernel, final_norm=final_norm),
        grid=(nt,),
        in_specs=[
            pl.BlockSpec((1, 1, TOP_K * tm), lambda i: (i, 0, 0), memory_space=pltpu.SMEM),
            pl.BlockSpec((1, 1, TOP_K * tm), lambda i: (jnp.minimum(i + 1, nt - 1), 0, 0),
                         memory_space=pltpu.SMEM),
            pl.BlockSpec((tm, d), lambda i: (i, 0)),
            pl.BlockSpec((tm, TOP_K), lambda i: (i, 0)),
            pl.BlockSpec((1, d), lambda i: (0, 0)),
            pl.BlockSpec(memory_space=pl.ANY),
        ],
        out_specs=pl.BlockSpec((tm, d), lambda i: (i, 0)),
        out_shape=jax.ShapeDtypeStruct((t, d), F32),
        scratch_shapes=[pltpu.VMEM((2, TOP_K, tm * SUBLANES, LANES), F32),
                        pltpu.SemaphoreType.DMA((2,))],
        compiler_params=pltpu.CompilerParams(
            dimension_semantics=("arbitrary",), vmem_limit_bytes=VMEM_LIMIT),
        name="combine",
    )(pos3, pos3, h2, gates, final_g, ys)


def _moe(h2, g, wcat, w1, w3, w2, j, final_g, final_norm):
    t, d = h2.shape
    eid, gates, rank, counts = _router(h2, g, wcat)
    n_assign = t * TOP_K
    n_blocks = -(-n_assign // MOE_TM) + N_EXPERTS
    cap = n_blocks * MOE_TM
    counts = counts[:, 0]
    padded = (counts + MOE_TM - 1) // MOE_TM * MOE_TM
    pad_end = jnp.cumsum(padded)
    pad_start = pad_end - padded
    experts = jnp.arange(N_EXPERTS, dtype=jnp.int32)[:, None, None]
    dest = rank + jnp.sum(jnp.where(eid[None] == experts, pad_start[:, None, None], 0), axis=0)
    block_start = jnp.arange(n_blocks, dtype=jnp.int32) * MOE_TM
    block_expert = jnp.minimum(
        jnp.sum(block_start[:, None] >= pad_end[None, :], axis=1), N_EXPERTS - 1
    ).astype(jnp.int32)
    valid = jnp.clip(counts[block_expert] - (block_start - pad_start[block_expert]),
                     0, MOE_TM).astype(jnp.int32)
    tail_rows = pad_end[-1] + jnp.arange(N_EXPERTS, dtype=jnp.int32) * MOE_TM
    zrows = jnp.concatenate([
        jnp.where(padded > 0, pad_end - MOE_TM, -1),
        jnp.where(tail_rows < cap, tail_rows, -1),
    ]).astype(jnp.int32)
    xs = _dispatch(h2, g, dest, zrows, cap)
    ys = _expert_ffn(xs, g, block_expert, valid, w1, w3, w2, j)
    return _combine(h2, gates.T, dest, ys, final_g, final_norm)


def _gate_blocks(w_r, w_i):
    nh, hd, _ = w_r.shape
    per = MXU_DIM // hd
    eye = jnp.eye(per, dtype=w_r.dtype)

    def blocks(w):
        wb = w.reshape(nh // per, per, hd, hd)
        return jnp.einsum("cpde,pq->cpdqe", wb, eye).reshape(nh // per, MXU_DIM, MXU_DIM)

    return jnp.concatenate([blocks(w_r), blocks(w_i)], axis=-1)


def _group_matrices(d_mix):
    grp = jnp.arange(d_mix) // HEAD_DIM
    cols = jnp.arange(LANES)
    g1 = jnp.where(grp[:, None] == cols[None, :], 1.0 / HEAD_DIM, 0.0).astype(BF16)
    e = jnp.where(cols[:, None] == grp[None, :], 1.0, 0.0).astype(BF16)
    return g1, jnp.concatenate([e, e], axis=0)


def kernel(x, norm_mix_g, w_in, conv_a_w, conv_a_b, conv_b_w, conv_b_b, lru_w_r, lru_b_r,
           lru_w_i, lru_b_i, lru_lambda, mix_out_g, w_out, norm_ffn_g, ffn_w1, ffn_w3, ffn_w2,
           router_w, expert_w1, expert_w3, expert_w2, final_g):
    b, s, d = x.shape
    depth = w_in.shape[0]
    assert depth % 2 == 0
    g1, e2 = _group_matrices(D_CONV + D_LRU)
    row = lambda v: v.reshape(1, -1)
    h = x
    for layer in range(depth):
        p = {
            "norm_mix_g": row(norm_mix_g[layer]),
            "w_in": w_in[layer].astype(BF16),
            "conv_a_w": conv_a_w[layer], "conv_a_b": row(conv_a_b[layer]),
            "conv_b_w": conv_b_w[layer], "conv_b_b": row(conv_b_b[layer]),
            "w_gate": _gate_blocks(lru_w_r[layer], lru_w_i[layer]).astype(BF16),
            "b_r": row(lru_b_r[layer]), "b_i": row(lru_b_i[layer]), "lam": row(lru_lambda[layer]),
            "g_mix": row(mix_out_g[layer]), "g1": g1, "e2": e2,
            "w_out": w_out[layer].astype(BF16),
        }
        h = _mixer(h, p, layer == 0, layer == depth - 1)
        h2 = h.reshape(b * s, d)
        gf = row(norm_ffn_g[layer])
        j = layer // 2
        if layer % 2 == 0:
            h2 = _dense_ffn(h2, gf, ffn_w1, ffn_w3, ffn_w2, j)
        else:
            w_hi = router_w[j].astype(BF16)
            w_lo = (router_w[j] - w_hi.astype(F32)).astype(BF16)
            h2 = _moe(h2, gf, jnp.concatenate([w_hi.T, w_lo.T], axis=0),
                      expert_w1, expert_w3, expert_w2, j, row(final_g), layer == depth - 1)
        h = h2.reshape(b, s, d)
    return h
```

```python
import functools

import jax
import jax.numpy as jnp
from jax import lax
from jax.experimental import pallas as pl
from jax.experimental.pallas import tpu as pltpu

EPS = 1e-6
HEAD_DIM = 64
D_CONV = 512
D_LRU = 512
LRU_C = 8.0
N_EXPERTS = 8
TOP_K = 2

SUBLANES = 8
LANES = 128
MXU_DIM = 256

MIX_TS = 1024
FFN_TM = 1024
FFN_TF = 512
MOE_TM = 1024
MOE_ROW_STEP = 256
ROUTE_TM = 512
DISPATCH_TM = 1024
COMBINE_TM = 512
ISSUE_UNROLL = 8
VMEM_LIMIT = 56 * 1024 * 1024

F32 = jnp.float32
BF16 = jnp.bfloat16


def _rms_norm_rows(x, g):
    return x * lax.rsqrt(jnp.mean(x * x, axis=-1, keepdims=True) + EPS) * g


def _sigmoid(x):
    return 1.0 / (1.0 + jnp.exp(-x))


def _gelu_tanh(x):
    return 0.5 * x * (1.0 + jnp.tanh(0.7978845608028654 * (x + 0.044715 * (x * x * x))))


def _phases(v, g):
    return [v[r * g:(r + 1) * g] for r in range(SUBLANES)]


def _group_shift(v, first_row):
    rolled = pltpu.roll(v, 1, 0)
    rows = lax.broadcasted_iota(jnp.int32, (SUBLANES, v.shape[1]), 0)
    head = jnp.where(rows == 0, jnp.broadcast_to(first_row, (SUBLANES, v.shape[1])),
                     rolled[0:SUBLANES])
    return jnp.concatenate([head, rolled[SUBLANES:]], axis=0)


def _causal_conv(phases, tail, w_ref, b_ref):
    taps = w_ref.shape[0]
    bias = b_ref[...]
    shifted = {}

    def phase(idx):
        if idx >= 0:
            return phases[idx]
        if idx not in shifted:
            src = idx + SUBLANES
            shifted[idx] = _group_shift(phases[src], tail[src:src + 1])
        return shifted[idx]

    out = []
    for r in range(SUBLANES):
        acc = w_ref[taps - 1:taps, :] * phases[r] + bias
        for k in range(taps - 1):
            acc = acc + w_ref[k:k + 1, :] * phase(r - (taps - 1) + k)
        out.append(acc)
    return out


def _last_group_rows(phases):
    return jnp.concatenate([p[p.shape[0] - 1:p.shape[0]] for p in phases], axis=0)


def _to_phase_major(src_ref, perm_scr):
    ts, d = src_ref.shape
    grp = ts // SUBLANES
    for c in range(d // LANES):
        perm_scr[c] = src_ref[:, c * LANES:(c + 1) * LANES]
    return jnp.concatenate(
        [jnp.concatenate([perm_scr[c, pl.ds(r, grp, stride=SUBLANES), :]
                          for c in range(d // LANES)], axis=-1)
         for r in range(SUBLANES)], axis=0)


def _store_time_major(dst_ref, value, perm_scr):
    ts, d = value.shape
    grp = ts // SUBLANES
    for c in range(d // LANES):
        for ph in range(SUBLANES):
            perm_scr[c, pl.ds(ph, grp, stride=SUBLANES), :] = (
                value[ph * grp:(ph + 1) * grp, c * LANES:(c + 1) * LANES])
    dst_ref[...] = jnp.concatenate([perm_scr[c] for c in range(d // LANES)], axis=-1)


def _mixer_kernel(h_ref, gn_ref, win_ref, caw_ref, cab_ref, cbw_ref, cbb_ref, wg_ref,
                  br_ref, bi_ref, lam_ref, gmix_ref, g1_ref, e2_ref, wout_ref,
                  o_ref, taila_ref, tailb_ref, hstate_ref, perm_scr,
                  *, time_major_in, time_major_out):
    ts = h_ref.shape[0]
    grp = ts // SUBLANES

    @pl.when(pl.program_id(1) == 0)
    def _():
        taila_ref[...] = jnp.zeros_like(taila_ref)
        tailb_ref[...] = jnp.zeros_like(tailb_ref)
        hstate_ref[...] = jnp.zeros_like(hstate_ref)

    x = _to_phase_major(h_ref, perm_scr) if time_major_in else h_ref[...]
    xn = _rms_norm_rows(x, gn_ref[...])
    z = jnp.dot(xn.astype(BF16), win_ref[...], preferred_element_type=F32)
    xa = z[:, 0:D_CONV]
    gb = z[:, D_CONV:2 * D_CONV]
    gc = z[:, 2 * D_CONV:3 * D_CONV]
    xr = z[:, 3 * D_CONV:3 * D_CONV + D_LRU]
    gr = z[:, 3 * D_CONV + D_LRU:]

    ua = _phases(gc * xa, grp)
    ca = jnp.concatenate(_causal_conv(ua, taila_ref[...], caw_ref, cab_ref), axis=0)
    taila_ref[...] = _last_group_rows(ua)
    ya = gb * ca

    xrp = _phases(xr, grp)
    xc = jnp.concatenate(_causal_conv(xrp, tailb_ref[...], cbw_ref, cbb_ref), axis=0)
    tailb_ref[...] = _last_group_rows(xrp)

    xcb = xc.astype(BF16)
    gz = [jnp.dot(xcb[:, c * MXU_DIM:(c + 1) * MXU_DIM], wg_ref[c], preferred_element_type=F32)
          for c in range(D_LRU // MXU_DIM)]
    gz_r = jnp.concatenate([g[:, 0:MXU_DIM] for g in gz], axis=-1)
    gz_i = jnp.concatenate([g[:, MXU_DIM:] for g in gz], axis=-1)
    r = _sigmoid(gz_r + br_ref[...])
    i = _sigmoid(gz_i + bi_ref[...])
    nl = -lam_ref[...]
    softplus = jnp.maximum(nl, 0.0) + jnp.log1p(jnp.exp(-jnp.abs(nl)))
    log_a = (-LRU_C) * r * softplus
    a = jnp.exp(log_a)
    u = jnp.sqrt(1.0 - a * a) * (i * xc)

    ap = _phases(a, grp)
    up = _phases(u, grp)
    hloc = [up[0]]
    ploc = [ap[0]]
    for ph in range(1, SUBLANES):
        hloc.append(ap[ph] * hloc[ph - 1] + up[ph])
        ploc.append(ap[ph] * ploc[ph - 1])
    ps = ploc[SUBLANES - 1]
    hs = hloc[SUBLANES - 1]
    rows = lax.broadcasted_iota(jnp.int32, ps.shape, 0)
    d = 1
    while d < grp:
        m = rows >= d
        hs = jnp.where(m, ps * pltpu.roll(hs, d, 0) + hs, hs)
        ps = jnp.where(m, ps * pltpu.roll(ps, d, 0), ps)
        d *= 2
    c0 = hstate_ref[0:1, :]
    after = ps * c0 + hs
    cin = _group_shift(after, c0)
    hstate_ref[...] = jnp.broadcast_to(after[grp - 1:grp], hstate_ref.shape)
    hh = jnp.concatenate([hloc[ph] + ploc[ph] * cin for ph in range(SUBLANES)], axis=0)
    yr = hh * _gelu_tanh(gr)

    y = jnp.concatenate([ya, yr], axis=-1)
    gm = jnp.dot((y * y).astype(BF16), g1_ref[...], preferred_element_type=F32)
    rs = lax.rsqrt(gm + EPS)
    rs_hi = rs.astype(BF16)
    rs_lo = (rs - rs_hi.astype(F32)).astype(BF16)
    rs_full = jnp.dot(jnp.concatenate([rs_hi, rs_lo], axis=-1), e2_ref[...],
                      preferred_element_type=F32)
    yn = y * rs_full * gmix_ref[...]
    out = x + jnp.dot(yn.astype(BF16), wout_ref[...], preferred_element_type=F32)
    if time_major_out:
        _store_time_major(o_ref, out, perm_scr)
    else:
        o_ref[...] = out


def _mixer(h, p, time_major_in, time_major_out):
    b, s, d = h.shape
    ts = MIX_TS
    assert s % ts == 0 and ts % (SUBLANES * SUBLANES) == 0
    d_in = p["w_in"].shape[1]
    d_mix = D_CONV + D_LRU
    const = lambda shape: pl.BlockSpec(shape, lambda bi, j: (0,) * len(shape))
    return pl.pallas_call(
        functools.partial(_mixer_kernel, time_major_in=time_major_in,
                          time_major_out=time_major_out),
        grid=(b, s // ts),
        in_specs=[
            pl.BlockSpec((None, ts, d), lambda bi, j: (bi, j, 0)),
            const((1, d)), const((d, d_in)),
            const((3, D_CONV)), const((1, D_CONV)), const((4, D_LRU)), const((1, D_LRU)),
            const((D_LRU // MXU_DIM, MXU_DIM, 2 * MXU_DIM)),
            const((1, D_LRU)), const((1, D_LRU)), const((1, D_LRU)),
            const((1, d_mix)), const((d_mix, LANES)), const((2 * LANES, d_mix)), const((d_mix, d)),
        ],
        out_specs=pl.BlockSpec((None, ts, d), lambda bi, j: (bi, j, 0)),
        out_shape=jax.ShapeDtypeStruct((b, s, d), F32),
        scratch_shapes=[
            pltpu.VMEM((SUBLANES, D_CONV), F32), pltpu.VMEM((SUBLANES, D_LRU), F32),
            pltpu.VMEM((SUBLANES, D_LRU), F32),
            pltpu.VMEM((d // LANES, ts, LANES), F32),
        ],
        compiler_params=pltpu.CompilerParams(
            dimension_semantics=("arbitrary", "arbitrary"), vmem_limit_bytes=VMEM_LIMIT),
        name="mixer",
    )(h, p["norm_mix_g"], p["w_in"], p["conv_a_w"], p["conv_a_b"], p["conv_b_w"], p["conv_b_b"],
      p["w_gate"], p["b_r"], p["b_i"], p["lam"], p["g_mix"], p["g1"], p["e2"], p["w_out"])


def _swiglu_chunk(xn, w1_ref, w3_ref, w2_ref):
    h1 = jnp.dot(xn, w1_ref[...], preferred_element_type=F32)
    h3 = jnp.dot(xn, w3_ref[...], preferred_element_type=F32)
    g = (h1 * _sigmoid(h1)) * h3
    return jnp.dot(g.astype(BF16), w2_ref[...], preferred_element_type=F32)


def _swiglu_kernel(be_ref, first_ref, valid_ref, x_ref, g_ref, w1_hbm, w3_hbm, w2_hbm, o_ref,
                   xn_scr, acc_scr, res1, res3, res2, st1, st3, st2, sem,
                   *, nf, tf, layer, routed, row_paths):
    assert nf >= 2
    i = pl.program_id(0)
    f = pl.program_id(1)
    valid = valid_ref[i]
    used = valid > 0
    tm, d = xn_scr.shape
    assert row_paths[-1] == tm
    e = be_ref[i]

    def chunk_copies(fc):
        cols = pl.ds(pl.multiple_of(fc * tf, tf), tf)
        return (pltpu.make_async_copy(w1_hbm.at[layer, e, :, cols], st1, sem.at[0]),
                pltpu.make_async_copy(w3_hbm.at[layer, e, :, cols], st3, sem.at[1]),
                pltpu.make_async_copy(w2_hbm.at[layer, e, cols, :], st2, sem.at[2]))

    @pl.when(jnp.logical_and(used, first_ref[i] == 1))
    def _():
        @pl.when(f == 0)
        def _():
            for c in chunk_copies(f):
                c.start()

        for c in chunk_copies(f):
            c.wait()
        res1[f] = st1[...].astype(BF16)
        res3[f] = st3[...].astype(BF16)
        res2[f] = st2[...].astype(BF16)

        @pl.when(f + 1 < nf)
        def _():
            for c in chunk_copies(f + 1):
                c.start()

    w_refs = (res1.at[f], res3.at[f], res2.at[f])

    def load_xn(m):
        if routed:
            return _load_token_major(x_ref, (), m, d).astype(BF16)
        return _rms_norm_rows(x_ref[...], g_ref[...]).astype(BF16)

    def finish(y, m):
        if routed:
            _store_token_major(o_ref, (), y)
            if m < tm:
                o_ref[m * SUBLANES:, :] = jnp.zeros(((tm - m) * SUBLANES, LANES), F32)
        else:
            o_ref[...] = x_ref[...] + y

    def row_path(m, in_path):
        @pl.when(jnp.logical_and(in_path, f == 0))
        def _():
            xn = load_xn(m)
            xn_scr[0:m] = xn
            acc_scr[0:m] = _swiglu_chunk(xn, *w_refs)

        @pl.when(jnp.logical_and(in_path, jnp.logical_and(f > 0, f < nf - 1)))
        def _():
            acc_scr[0:m] += _swiglu_chunk(xn_scr[0:m], *w_refs)

        @pl.when(jnp.logical_and(in_path, f == nf - 1))
        def _():
            finish(acc_scr[0:m] + _swiglu_chunk(xn_scr[0:m], *w_refs), m)

    lo = 0
    for m in row_paths:
        row_path(m, jnp.logical_and(valid > lo, valid <= m))
        lo = m

    @pl.when(jnp.logical_and(jnp.logical_not(used), f == nf - 1))
    def _():
        o_ref[...] = jnp.zeros_like(o_ref)


def _swiglu(x, g, block_expert, first, valid, w1, w3, w2, layer, tm, row_paths, routed, name):
    d, ff = w1.shape[2], w1.shape[3]
    tf = FFN_TF
    nf = ff // tf
    rows = x.shape[0] // SUBLANES if routed else x.shape[0]
    blk = (tm * SUBLANES, LANES) if routed else (tm, d)
    grid_spec = pltpu.PrefetchScalarGridSpec(
        num_scalar_prefetch=3,
        grid=(rows // tm, nf),
        in_specs=[
            pl.BlockSpec(blk, lambda i, f, *_: (i, 0)),
            pl.BlockSpec((1, d), lambda i, f, *_: (0, 0)),
            pl.BlockSpec(memory_space=pl.ANY),
            pl.BlockSpec(memory_space=pl.ANY),
            pl.BlockSpec(memory_space=pl.ANY),
        ],
        out_specs=pl.BlockSpec(blk, lambda i, f, *_: (i, 0)),
        scratch_shapes=[
            pltpu.VMEM((tm, d), BF16), pltpu.VMEM((tm, d), F32),
            pltpu.VMEM((nf, d, tf), BF16), pltpu.VMEM((nf, d, tf), BF16),
            pltpu.VMEM((nf, tf, d), BF16),
            pltpu.VMEM((d, tf), F32), pltpu.VMEM((d, tf), F32), pltpu.VMEM((tf, d), F32),
            pltpu.SemaphoreType.DMA((3,)),
        ],
    )
    return pl.pallas_call(
        functools.partial(_swiglu_kernel, nf=nf, tf=tf, layer=layer, routed=routed,
                          row_paths=row_paths),
        grid_spec=grid_spec,
        out_shape=jax.ShapeDtypeStruct(x.shape, F32),
        compiler_params=pltpu.CompilerParams(
            dimension_semantics=("arbitrary", "arbitrary"), vmem_limit_bytes=VMEM_LIMIT),
        name=name,
    )(block_expert, first, valid, x, g, w1, w3, w2)


def _dense_ffn(h2, g, w1, w3, w2, j):
    t = h2.shape[0]
    tm = min(FFN_TM, t)
    nb = t // tm
    block_expert = jnp.zeros((nb,), jnp.int32)
    first = jnp.zeros((nb,), jnp.int32).at[0].set(1)
    valid = jnp.full((nb,), tm, jnp.int32)
    return _swiglu(h2, g, block_expert, first, valid,
                   w1[:, None], w3[:, None], w2[:, None], j, tm, (tm,), False, "dense_ffn")


def _token_major_index(idx, c, n_tokens):
    return idx + (pl.ds(c, n_tokens, stride=SUBLANES), slice(None))


def _load_token_major(ref, idx, n_tokens, d):
    assert d == SUBLANES * LANES
    return jnp.concatenate(
        [ref[_token_major_index(idx, c, n_tokens)] for c in range(SUBLANES)], axis=-1)


def _store_token_major(ref, idx, value):
    n_tokens, d = value.shape
    assert d == SUBLANES * LANES
    for c in range(SUBLANES):
        ref[_token_major_index(idx, c, n_tokens)] = value[:, c * LANES:(c + 1) * LANES]


def _token_rows(t):
    return pl.ds(pl.multiple_of(t * SUBLANES, SUBLANES), SUBLANES)


def _expert_ffn(xs, g, block_expert, valid, w1, w3, w2, j):
    first = jnp.concatenate(
        [jnp.ones((1,), jnp.int32), (block_expert[1:] != block_expert[:-1]).astype(jnp.int32)])
    row_paths = tuple(range(MOE_ROW_STEP, MOE_TM + 1, MOE_ROW_STEP))
    return _swiglu(xs, g, block_expert, first, valid, w1, w3, w2, j, MOE_TM, row_paths, True,
                   "expert_ffn")


def _router_kernel(x_ref, g_ref, wcat_ref, eid_ref, gate_ref, rank_ref, cnt_ref, carry_ref):
    tm = x_ref.shape[0]

    @pl.when(pl.program_id(0) == 0)
    def _():
        carry_ref[...] = jnp.zeros_like(carry_ref)

    xn = _rms_norm_rows(x_ref[...], g_ref[...])
    x_hi = xn.astype(BF16)
    x_lo = (xn - x_hi.astype(F32)).astype(BF16)
    nt_dims = (((1,), (1,)), ((), ()))
    wcat = wcat_ref[...]
    l_hi = lax.dot_general(wcat, x_hi, nt_dims, preferred_element_type=F32)
    l_lo = lax.dot_general(wcat, x_lo, nt_dims, preferred_element_type=F32)
    logits = l_hi[0:N_EXPERTS] + l_hi[N_EXPERTS:] + l_lo[0:N_EXPERTS]

    sub = lax.broadcasted_iota(jnp.int32, logits.shape, 0)
    m1 = jnp.max(logits, axis=0, keepdims=True)
    i1 = jnp.min(jnp.where(logits == m1, sub, N_EXPERTS), axis=0, keepdims=True)
    rest = jnp.where(sub == i1, -jnp.inf, logits)
    m2 = jnp.max(rest, axis=0, keepdims=True)
    i2 = jnp.min(jnp.where(rest == m2, sub, N_EXPERTS), axis=0, keepdims=True)
    e2 = jnp.exp(m2 - m1)
    denom = 1.0 + e2
    gate_ref[...] = jnp.concatenate([1.0 / denom, e2 / denom], axis=0)
    eid_ref[...] = jnp.concatenate([i1, i2], axis=0)

    sel1 = sub == i1
    sel2 = sub == i2
    onehot = jnp.where(jnp.logical_or(sel1, sel2), 1.0, 0.0)
    r_io = lax.broadcasted_iota(jnp.int32, (tm, tm), 0)
    c_io = lax.broadcasted_iota(jnp.int32, (tm, tm), 1)
    tri = jnp.where(r_io < c_io, 1.0, 0.0).astype(BF16)
    before = jnp.dot(onehot.astype(BF16), tri, preferred_element_type=F32) + carry_ref[...]
    rank1 = jnp.sum(jnp.where(sel1, before, 0.0), axis=0, keepdims=True)
    rank2 = jnp.sum(jnp.where(sel2, before, 0.0), axis=0, keepdims=True)
    rank_ref[...] = jnp.concatenate([rank1, rank2], axis=0).astype(jnp.int32)
    total = carry_ref[...] + jnp.sum(onehot, axis=1, keepdims=True)
    carry_ref[...] = total
    cnt_ref[...] = total.astype(jnp.int32)


def _router(h2, g, wcat):
    t, d = h2.shape
    tm = min(ROUTE_TM, t)
    pair = lambda dt: jax.ShapeDtypeStruct((TOP_K, t), dt)
    return pl.pallas_call(
        _router_kernel,
        grid=(t // tm,),
        in_specs=[
            pl.BlockSpec((tm, d), lambda i: (i, 0)),
            pl.BlockSpec((1, d), lambda i: (0, 0)),
            pl.BlockSpec((2 * N_EXPERTS, d), lambda i: (0, 0)),
        ],
        out_specs=[
            pl.BlockSpec((TOP_K, tm), lambda i: (0, i)),
            pl.BlockSpec((TOP_K, tm), lambda i: (0, i)),
            pl.BlockSpec((TOP_K, tm), lambda i: (0, i)),
            pl.BlockSpec((N_EXPERTS, 1), lambda i: (0, 0)),
        ],
        out_shape=[pair(jnp.int32), pair(F32), pair(jnp.int32),
                   jax.ShapeDtypeStruct((N_EXPERTS, 1), jnp.int32)],
        scratch_shapes=[pltpu.VMEM((N_EXPERTS, 1), F32)],
        compiler_params=pltpu.CompilerParams(dimension_semantics=("arbitrary",)),
        name="router",
    )(h2, g, wcat)


def _tile_indices(idx, tm):
    k, t = idx.shape
    return idx.reshape(k, t // tm, tm).transpose(1, 0, 2).reshape(t // tm, 1, k * tm)


def _start_row_copies(n_rows, row_copy):
    def body(c, carry):
        for u in range(ISSUE_UNROLL):
            for k in range(TOP_K):
                row_copy(c * ISSUE_UNROLL + u, k).start(priority=k)
        return carry

    lax.fori_loop(0, n_rows // ISSUE_UNROLL, body, 0)


def _dispatch_kernel(zrow_ref, dest_ref, x_ref, g_ref, xs_ref, xn_scr, zero_scr, sem, zsem):
    i = pl.program_id(0)
    tm = x_ref.shape[0]
    slot = i % 2

    @pl.when(i == 0)
    def _():
        zero_scr[...] = jnp.zeros_like(zero_scr)
        zcopies = [
            pltpu.make_async_copy(
                zero_scr,
                xs_ref.at[pl.ds(pl.multiple_of(jnp.maximum(zrow_ref[n], 0) * SUBLANES, MOE_TM),
                                MOE_TM * SUBLANES)],
                zsem)
            for n in range(2 * N_EXPERTS)]
        for n, c in enumerate(zcopies):
            pl.when(zrow_ref[n] >= 0)(c.start)
        for n, c in enumerate(zcopies):
            pl.when(zrow_ref[n] >= 0)(c.wait)

    _store_token_major(xn_scr, (slot,), _rms_norm_rows(x_ref[...], g_ref[...]))

    def row_copy(t, k):
        d = dest_ref[0, 0, k * tm + t]
        return pltpu.make_async_copy(
            xn_scr.at[slot, _token_rows(t)], xs_ref.at[_token_rows(d)], sem.at[slot])

    _start_row_copies(tm, row_copy)

    def wait_slot(s):
        for _ in range(TOP_K):
            pltpu.make_async_copy(
                xn_scr.at[s], xs_ref.at[pl.ds(0, tm * SUBLANES)], sem.at[s]).wait()

    @pl.when(i > 0)
    def _():
        wait_slot(1 - slot)

    @pl.when(i == pl.num_programs(0) - 1)
    def _():
        wait_slot(slot)


def _dispatch(h2, g, dest, zrows, cap):
    t, d = h2.shape
    tm = min(DISPATCH_TM, t)
    dest3 = _tile_indices(dest, tm)
    grid_spec = pltpu.PrefetchScalarGridSpec(
        num_scalar_prefetch=1,
        grid=(t // tm,),
        in_specs=[
            pl.BlockSpec((1, 1, TOP_K * tm), lambda i, z: (i, 0, 0), memory_space=pltpu.SMEM),
            pl.BlockSpec((tm, d), lambda i, z: (i, 0)),
            pl.BlockSpec((1, d), lambda i, z: (0, 0)),
        ],
        out_specs=pl.BlockSpec(memory_space=pl.ANY),
        scratch_shapes=[pltpu.VMEM((2, tm * SUBLANES, LANES), F32),
                        pltpu.VMEM((MOE_TM * SUBLANES, LANES), F32),
                        pltpu.SemaphoreType.DMA((2,)), pltpu.SemaphoreType.DMA],
    )
    return pl.pallas_call(
        _dispatch_kernel,
        grid_spec=grid_spec,
        out_shape=jax.ShapeDtypeStruct((cap * SUBLANES, LANES), F32),
        compiler_params=pltpu.CompilerParams(
            dimension_semantics=("arbitrary",), vmem_limit_bytes=VMEM_LIMIT),
        name="dispatch",
    )(zrows, dest3, h2, g)


def _combine_kernel(pos_ref, posn_ref, h_ref, gate_ref, fg_ref, ys_ref, o_ref, buf, sem,
                    *, final_norm):
    i = pl.program_id(0)
    n = pl.num_programs(0)
    tm = h_ref.shape[0]
    slot = i % 2

    def start_gather(p_ref, s):
        def row_copy(t, k):
            p = p_ref[0, 0, k * tm + t]
            return pltpu.make_async_copy(
                ys_ref.at[_token_rows(p)], buf.at[s, k, _token_rows(t)], sem.at[s])
        _start_row_copies(tm, row_copy)

    @pl.when(i == 0)
    def _():
        start_gather(pos_ref, slot)

    @pl.when(i + 1 < n)
    def _():
        start_gather(posn_ref, 1 - slot)

    for k in range(TOP_K):
        pltpu.make_async_copy(
            ys_ref.at[pl.ds(0, tm * SUBLANES)], buf.at[slot, k], sem.at[slot]).wait()
    gate = gate_ref[...]
    d = h_ref.shape[1]
    out = (h_ref[...] + gate[:, 0:1] * _load_token_major(buf, (slot, 0), tm, d)
           + gate[:, 1:2] * _load_token_major(buf, (slot, 1), tm, d))
    if final_norm:
        out = _rms_norm_rows(out, fg_ref[...])
    o_ref[...] = out


def _combine(h2, gates, pos, ys, final_g, final_norm):
    t, d = h2.shape
    tm = min(COMBINE_TM, t)
    nt = t // tm
    pos3 = _tile_indices(pos, tm)
    return pl.pallas_call(
        functools.partial(_combine_kernel, final_norm=final_norm),
        grid=(nt,),
        in_specs=[
            pl.BlockSpec((1, 1, TOP_K * tm), lambda i: (i, 0, 0), memory_space=pltpu.SMEM),
            pl.BlockSpec((1, 1, TOP_K * tm), lambda i: (jnp.minimum(i + 1, nt - 1), 0, 0),
                         memory_space=pltpu.SMEM),
            pl.BlockSpec((tm, d), lambda i: (i, 0)),
            pl.BlockSpec((tm, TOP_K), lambda i: (i, 0)),
            pl.BlockSpec((1, d), lambda i: (0, 0)),
            pl.BlockSpec(memory_space=pl.ANY),
        ],
        out_specs=pl.BlockSpec((tm, d), lambda i: (i, 0)),
        out_shape=jax.ShapeDtypeStruct((t, d), F32),
        scratch_shapes=[pltpu.VMEM((2, TOP_K, tm * SUBLANES, LANES), F32),
                        pltpu.SemaphoreType.DMA((2,))],
        compiler_params=pltpu.CompilerParams(
            dimension_semantics=("arbitrary",), vmem_limit_bytes=VMEM_LIMIT),
        name="combine",
    )(pos3, pos3, h2, gates, final_g, ys)


def _moe(h2, g, wcat, w1, w3, w2, j, final_g, final_norm):
    t, d = h2.shape
    eid, gates, rank, counts = _router(h2, g, wcat)
    n_assign = t * TOP_K
    n_blocks = -(-n_assign // MOE_TM) + N_EXPERTS
    cap = n_blocks * MOE_TM
    counts = counts[:, 0]
    padded = (counts + MOE_TM - 1) // MOE_TM * MOE_TM
    pad_end = jnp.cumsum(padded)
    pad_start = pad_end - padded
    experts = jnp.arange(N_EXPERTS, dtype=jnp.int32)[:, None, None]
    dest = rank + jnp.sum(jnp.where(eid[None] == experts, pad_start[:, None, None], 0), axis=0)
    block_start = jnp.arange(n_blocks, dtype=jnp.int32) * MOE_TM
    block_expert = jnp.minimum(
        jnp.sum(block_start[:, None] >= pad_end[None, :], axis=1), N_EXPERTS - 1
    ).astype(jnp.int32)
    valid = jnp.clip(counts[block_expert] - (block_start - pad_start[block_expert]),
                     0, MOE_TM).astype(jnp.int32)
    tail_rows = pad_end[-1] + jnp.arange(N_EXPERTS, dtype=jnp.int32) * MOE_TM
    zrows = jnp.concatenate([
        jnp.where(padded > 0, pad_end - MOE_TM, -1),
        jnp.where(tail_rows < cap, tail_rows, -1),
    ]).astype(jnp.int32)
    xs = _dispatch(h2, g, dest, zrows, cap)
    ys = _expert_ffn(xs, g, block_expert, valid, w1, w3, w2, j)
    return _combine(h2, gates.T, dest, ys, final_g, final_norm)


def _gate_blocks(w_r, w_i):
    nh, hd, _ = w_r.shape
    per = MXU_DIM // hd
    eye = jnp.eye(per, dtype=w_r.dtype)

    def blocks(w):
        wb = w.reshape(nh // per, per, hd, hd)
        return jnp.einsum("cpde,pq->cpdqe", wb, eye).reshape(nh // per, MXU_DIM, MXU_DIM)

    return jnp.concatenate([blocks(w_r), blocks(w_i)], axis=-1)


def _group_matrices(d_mix):
    grp = jnp.arange(d_mix) // HEAD_DIM
    cols = jnp.arange(LANES)
    g1 = jnp.where(grp[:, None] == cols[None, :], 1.0 / HEAD_DIM, 0.0).astype(BF16)
    e = jnp.where(cols[:, None] == grp[None, :], 1.0, 0.0).astype(BF16)
    return g1, jnp.concatenate([e, e], axis=0)


def kernel(x, norm_mix_g, w_in, conv_a_w, conv_a_b, conv_b_w, conv_b_b, lru_w_r, lru_b_r,
           lru_w_i, lru_b_i, lru_lambda, mix_out_g, w_out, norm_ffn_g, ffn_w1, ffn_w3, ffn_w2,
           router_w, expert_w1, expert_w3, expert_w2, final_g):
    b, s, d = x.shape
    depth = w_in.shape[0]
    assert depth % 2 == 0
    g1, e2 = _group_matrices(D_CONV + D_LRU)
    row = lambda v: v.reshape(1, -1)
    h = x
    for layer in range(depth):
        p = {
            "norm_mix_g": row(norm_mix_g[layer]),
            "w_in": w_in[layer].astype(BF16),
            "conv_a_w": conv_a_w[layer], "conv_a_b": row(conv_a_b[layer]),
            "conv_b_w": conv_b_w[layer], "conv_b_b": row(conv_b_b[layer]),
            "w_gate": _gate_blocks(lru_w_r[layer], lru_w_i[layer]).astype(BF16),
            "b_r": row(lru_b_r[layer]), "b_i": row(lru_b_i[layer]), "lam": row(lru_lambda[layer]),
            "g_mix": row(mix_out_g[layer]), "g1": g1, "e2": e2,
            "w_out": w_out[layer].astype(BF16),
        }
        h = _mixer(h, p, layer == 0, layer == depth - 1)
        h2 = h.reshape(b * s, d)
        gf = row(norm_ffn_g[layer])
        j = layer // 2
        if layer % 2 == 0:
            h2 = _dense_ffn(h2, gf, ffn_w1, ffn_w3, ffn_w2, j)
        else:
            w_hi = router_w[j].astype(BF16)
            w_lo = (router_w[j] - w_hi.astype(F32)).astype(BF16)
            h2 = _moe(h2, gf, jnp.concatenate([w_hi.T, w_lo.T], axis=0),
                      expert_w1, expert_w3, expert_w2, j, row(final_g), layer == depth - 1)
        h = h2.reshape(b, s, d)
    return h
```

```python
import functools

import jax
import jax.numpy as jnp
from jax import lax
from jax.experimental import pallas as pl
from jax.experimental.pallas import tpu as pltpu

EPS = 1e-6
HEAD_DIM = 64
D_CONV = 512
D_LRU = 512
LRU_C = 8.0
N_EXPERTS = 8
TOP_K = 2

SUBLANES = 8
LANES = 128
MXU_DIM = 256

MIX_TS = 1024
MIX_SPLIT = 2
FFN_TM = 1024
FFN_TF = 512
MOE_TM = 1024
MOE_ROW_STEP = 256
ROUTE_TM = 512
DISPATCH_TM = 1024
COMBINE_TM = 512
ISSUE_UNROLL = 8
VMEM_LIMIT = 56 * 1024 * 1024

F32 = jnp.float32
BF16 = jnp.bfloat16


def _rms_norm_rows(x, g):
    return x * lax.rsqrt(jnp.mean(x * x, axis=-1, keepdims=True) + EPS) * g


def _sigmoid(x):
    return 1.0 / (1.0 + jnp.exp(-x))


def _gelu_tanh(x):
    return 0.5 * x * (1.0 + jnp.tanh(0.7978845608028654 * (x + 0.044715 * (x * x * x))))


def _phases(v, g):
    return [v[r * g:(r + 1) * g] for r in range(SUBLANES)]


def _group_shift(v, first_row):
    rolled = pltpu.roll(v, 1, 0)
    rows = lax.broadcasted_iota(jnp.int32, (SUBLANES, v.shape[1]), 0)
    head = jnp.where(rows == 0, jnp.broadcast_to(first_row, (SUBLANES, v.shape[1])),
                     rolled[0:SUBLANES])
    return jnp.concatenate([head, rolled[SUBLANES:]], axis=0)


def _causal_conv(phases, tail, w_ref, b_ref):
    taps = w_ref.shape[0]
    bias = b_ref[...]
    shifted = {}

    def phase(idx):
        if idx >= 0:
            return phases[idx]
        if idx not in shifted:
            src = idx + SUBLANES
            shifted[idx] = _group_shift(phases[src], tail[src:src + 1])
        return shifted[idx]

    out = []
    for r in range(SUBLANES):
        acc = w_ref[taps - 1:taps, :] * phases[r] + bias
        for k in range(taps - 1):
            acc = acc + w_ref[k:k + 1, :] * phase(r - (taps - 1) + k)
        out.append(acc)
    return out


def _last_group_rows(phases):
    return jnp.concatenate([p[p.shape[0] - 1:p.shape[0]] for p in phases], axis=0)


def _to_phase_major(src_ref, perm_scr):
    ts, d = src_ref.shape
    grp = ts // SUBLANES
    for c in range(d // LANES):
        perm_scr[c] = src_ref[:, c * LANES:(c + 1) * LANES]
    return jnp.concatenate(
        [jnp.concatenate([perm_scr[c, pl.ds(r, grp, stride=SUBLANES), :]
                          for c in range(d // LANES)], axis=-1)
         for r in range(SUBLANES)], axis=0)


def _store_time_major(dst_ref, value, perm_scr):
    ts, d = value.shape
    grp = ts // SUBLANES
    for c in range(d // LANES):
        for ph in range(SUBLANES):
            perm_scr[c, pl.ds(ph, grp, stride=SUBLANES), :] = (
                value[ph * grp:(ph + 1) * grp, c * LANES:(c + 1) * LANES])
    dst_ref[...] = jnp.concatenate([perm_scr[c] for c in range(d // LANES)], axis=-1)


def _mixer_kernel(h_ref, gn_ref, win_ref, caw_ref, cab_ref, cbw_ref, cbb_ref, wg_ref,
                  br_ref, bi_ref, lam_ref, gmix_ref, g1_ref, e2_ref, wout_ref,
                  o_ref, taila_ref, tailb_ref, hstate_ref, perm_scr,
                  *, time_major_in, time_major_out):
    ts = h_ref.shape[0]
    grp = ts // SUBLANES

    @pl.when(pl.program_id(1) == 0)
    def _():
        taila_ref[...] = jnp.zeros_like(taila_ref)
        tailb_ref[...] = jnp.zeros_like(tailb_ref)
        hstate_ref[...] = jnp.zeros_like(hstate_ref)

    def mix_rows(x, taila, tailb, c0):
        grp = x.shape[0] // SUBLANES
        xn = _rms_norm_rows(x, gn_ref[...])
        z = jnp.dot(xn.astype(BF16), win_ref[...], preferred_element_type=F32)
        xa = z[:, 0:D_CONV]
        gb = z[:, D_CONV:2 * D_CONV]
        gc = z[:, 2 * D_CONV:3 * D_CONV]
        xr = z[:, 3 * D_CONV:3 * D_CONV + D_LRU]
        gr = z[:, 3 * D_CONV + D_LRU:]

        ua = _phases(gc * xa, grp)
        ca = jnp.concatenate(_causal_conv(ua, taila, caw_ref, cab_ref), axis=0)
        taila_new = _last_group_rows(ua)
        ya = gb * ca

        xrp = _phases(xr, grp)
        xc = jnp.concatenate(_causal_conv(xrp, tailb, cbw_ref, cbb_ref), axis=0)
        tailb_new = _last_group_rows(xrp)

        xcb = xc.astype(BF16)
        gz = [jnp.dot(xcb[:, c * MXU_DIM:(c + 1) * MXU_DIM], wg_ref[c], preferred_element_type=F32)
              for c in range(D_LRU // MXU_DIM)]
        gz_r = jnp.concatenate([g[:, 0:MXU_DIM] for g in gz], axis=-1)
        gz_i = jnp.concatenate([g[:, MXU_DIM:] for g in gz], axis=-1)
        r = _sigmoid(gz_r + br_ref[...])
        i = _sigmoid(gz_i + bi_ref[...])
        nl = -lam_ref[...]
        softplus = jnp.maximum(nl, 0.0) + jnp.log1p(jnp.exp(-jnp.abs(nl)))
        log_a = (-LRU_C) * r * softplus
        a = jnp.exp(log_a)
        u = jnp.sqrt(1.0 - a * a) * (i * xc)

        ap = _phases(a, grp)
        up = _phases(u, grp)
        hloc = [up[0]]
        ploc = [ap[0]]
        for ph in range(1, SUBLANES):
            hloc.append(ap[ph] * hloc[ph - 1] + up[ph])
            ploc.append(ap[ph] * ploc[ph - 1])
        ps = ploc[SUBLANES - 1]
        hs = hloc[SUBLANES - 1]
        rows = lax.broadcasted_iota(jnp.int32, ps.shape, 0)
        d = 1
        while d < grp:
            m = rows >= d
            hs = jnp.where(m, ps * pltpu.roll(hs, d, 0) + hs, hs)
            ps = jnp.where(m, ps * pltpu.roll(ps, d, 0), ps)
            d *= 2
        after = ps * c0 + hs
        cin = _group_shift(after, c0)
        c_new = after[grp - 1:grp]
        hh = jnp.concatenate([hloc[ph] + ploc[ph] * cin for ph in range(SUBLANES)], axis=0)
        yr = hh * _gelu_tanh(gr)

        y = jnp.concatenate([ya, yr], axis=-1)
        gm = jnp.dot((y * y).astype(BF16), g1_ref[...], preferred_element_type=F32)
        rs = lax.rsqrt(gm + EPS)
        rs_hi = rs.astype(BF16)
        rs_lo = (rs - rs_hi.astype(F32)).astype(BF16)
        rs_full = jnp.dot(jnp.concatenate([rs_hi, rs_lo], axis=-1), e2_ref[...],
                          preferred_element_type=F32)
        yn = y * rs_full * gmix_ref[...]
        out = x + jnp.dot(yn.astype(BF16), wout_ref[...], preferred_element_type=F32)
        return out, taila_new, tailb_new, c_new

    x_tile = _to_phase_major(h_ref, perm_scr) if time_major_in else h_ref[...]
    taila, tailb, c0 = taila_ref[...], tailb_ref[...], hstate_ref[0:1, :]
    sub = grp // MIX_SPLIT
    outs = []
    for part in range(MIX_SPLIT):
        xp = jnp.concatenate([x_tile[r * grp + part * sub:r * grp + (part + 1) * sub]
                              for r in range(SUBLANES)], axis=0)
        op, taila, tailb, c0 = mix_rows(xp, taila, tailb, c0)
        outs.append(op)
    taila_ref[...] = taila
    tailb_ref[...] = tailb
    hstate_ref[...] = jnp.broadcast_to(c0, hstate_ref.shape)
    out = jnp.concatenate([outs[part][r * sub:(r + 1) * sub]
                           for r in range(SUBLANES) for part in range(MIX_SPLIT)], axis=0)
    if time_major_out:
        _store_time_major(o_ref, out, perm_scr)
    else:
        o_ref[...] = out


def _mixer(h, p, time_major_in, time_major_out):
    b, s, d = h.shape
    ts = MIX_TS
    assert s % ts == 0 and ts % (SUBLANES * SUBLANES) == 0
    d_in = p["w_in"].shape[1]
    d_mix = D_CONV + D_LRU
    const = lambda shape: pl.BlockSpec(shape, lambda bi, j: (0,) * len(shape))
    return pl.pallas_call(
        functools.partial(_mixer_kernel, time_major_in=time_major_in,
                          time_major_out=time_major_out),
        grid=(b, s // ts),
        in_specs=[
            pl.BlockSpec((None, ts, d), lambda bi, j: (bi, j, 0)),
            const((1, d)), const((d, d_in)),
            const((3, D_CONV)), const((1, D_CONV)), const((4, D_LRU)), const((1, D_LRU)),
            const((D_LRU // MXU_DIM, MXU_DIM, 2 * MXU_DIM)),
            const((1, D_LRU)), const((1, D_LRU)), const((1, D_LRU)),
            const((1, d_mix)), const((d_mix, LANES)), const((2 * LANES, d_mix)), const((d_mix, d)),
        ],
        out_specs=pl.BlockSpec((None, ts, d), lambda bi, j: (bi, j, 0)),
        out_shape=jax.ShapeDtypeStruct((b, s, d), F32),
        scratch_shapes=[
            pltpu.VMEM((SUBLANES, D_CONV), F32), pltpu.VMEM((SUBLANES, D_LRU), F32),
            pltpu.VMEM((SUBLANES, D_LRU), F32),
            pltpu.VMEM((d // LANES, ts, LANES), F32),
        ],
        compiler_params=pltpu.CompilerParams(
            dimension_semantics=("arbitrary", "arbitrary"), vmem_limit_bytes=VMEM_LIMIT),
        name="mixer",
    )(h, p["norm_mix_g"], p["w_in"], p["conv_a_w"], p["conv_a_b"], p["conv_b_w"], p["conv_b_b"],
      p["w_gate"], p["b_r"], p["b_i"], p["lam"], p["g_mix"], p["g1"], p["e2"], p["w_out"])


def _swiglu_chunk(xn, w1_ref, w3_ref, w2_ref):
    h1 = jnp.dot(xn, w1_ref[...], preferred_element_type=F32)
    h3 = jnp.dot(xn, w3_ref[...], preferred_element_type=F32)
    g = (h1 * _sigmoid(h1)) * h3
    return jnp.dot(g.astype(BF16), w2_ref[...], preferred_element_type=F32)


def _swiglu_kernel(be_ref, first_ref, valid_ref, x_ref, g_ref, w1_hbm, w3_hbm, w2_hbm, o_ref,
                   xn_scr, acc_scr, res1, res3, res2, st1, st3, st2, sem,
                   *, nf, tf, layer, routed, row_paths):
    assert nf >= 2
    i = pl.program_id(0)
    f = pl.program_id(1)
    valid = valid_ref[i]
    used = valid > 0
    tm, d = xn_scr.shape
    assert row_paths[-1] == tm
    e = be_ref[i]

    def chunk_copies(fc):
        cols = pl.ds(pl.multiple_of(fc * tf, tf), tf)
        return (pltpu.make_async_copy(w1_hbm.at[layer, e, :, cols], st1, sem.at[0]),
                pltpu.make_async_copy(w3_hbm.at[layer, e, :, cols], st3, sem.at[1]),
                pltpu.make_async_copy(w2_hbm.at[layer, e, cols, :], st2, sem.at[2]))

    @pl.when(jnp.logical_and(used, first_ref[i] == 1))
    def _():
        @pl.when(f == 0)
        def _():
            for c in chunk_copies(f):
                c.start()

        for c in chunk_copies(f):
            c.wait()
        res1[f] = st1[...].astype(BF16)
        res3[f] = st3[...].astype(BF16)
        res2[f] = st2[...].astype(BF16)

        @pl.when(f + 1 < nf)
        def _():
            for c in chunk_copies(f + 1):
                c.start()

    w_refs = (res1.at[f], res3.at[f], res2.at[f])

    def load_xn(m):
        if routed:
            return _load_token_major(x_ref, (), m, d).astype(BF16)
        return _rms_norm_rows(x_ref[...], g_ref[...]).astype(BF16)

    def finish(y, m):
        if routed:
            _store_token_major(o_ref, (), y)
            if m < tm:
                o_ref[m * SUBLANES:, :] = jnp.zeros(((tm - m) * SUBLANES, LANES), F32)
        else:
            o_ref[...] = x_ref[...] + y

    def row_path(m, in_path):
        @pl.when(jnp.logical_and(in_path, f == 0))
        def _():
            xn = load_xn(m)
            xn_scr[0:m] = xn
            acc_scr[0:m] = _swiglu_chunk(xn, *w_refs)

        @pl.when(jnp.logical_and(in_path, jnp.logical_and(f > 0, f < nf - 1)))
        def _():
            acc_scr[0:m] += _swiglu_chunk(xn_scr[0:m], *w_refs)

        @pl.when(jnp.logical_and(in_path, f == nf - 1))
        def _():
            finish(acc_scr[0:m] + _swiglu_chunk(xn_scr[0:m], *w_refs), m)

    lo = 0
    for m in row_paths:
        row_path(m, jnp.logical_and(valid > lo, valid <= m))
        lo = m

    @pl.when(jnp.logical_and(jnp.logical_not(used), f == nf - 1))
    def _():
        o_ref[...] = jnp.zeros_like(o_ref)


def _swiglu(x, g, block_expert, first, valid, w1, w3, w2, layer, tm, row_paths, routed, name):
    d, ff = w1.shape[2], w1.shape[3]
    tf = FFN_TF
    nf = ff // tf
    rows = x.shape[0] // SUBLANES if routed else x.shape[0]
    blk = (tm * SUBLANES, LANES) if routed else (tm, d)
    grid_spec = pltpu.PrefetchScalarGridSpec(
        num_scalar_prefetch=3,
        grid=(rows // tm, nf),
        in_specs=[
            pl.BlockSpec(blk, lambda i, f, *_: (i, 0)),
            pl.BlockSpec((1, d), lambda i, f, *_: (0, 0)),
            pl.BlockSpec(memory_space=pl.ANY),
            pl.BlockSpec(memory_space=pl.ANY),
            pl.BlockSpec(memory_space=pl.ANY),
        ],
        out_specs=pl.BlockSpec(blk, lambda i, f, *_: (i, 0)),
        scratch_shapes=[
            pltpu.VMEM((tm, d), BF16), pltpu.VMEM((tm, d), F32),
            pltpu.VMEM((nf, d, tf), BF16), pltpu.VMEM((nf, d, tf), BF16),
            pltpu.VMEM((nf, tf, d), BF16),
            pltpu.VMEM((d, tf), F32), pltpu.VMEM((d, tf), F32), pltpu.VMEM((tf, d), F32),
            pltpu.SemaphoreType.DMA((3,)),
        ],
    )
    return pl.pallas_call(
        functools.partial(_swiglu_kernel, nf=nf, tf=tf, layer=layer, routed=routed,
                          row_paths=row_paths),
        grid_spec=grid_spec,
        out_shape=jax.ShapeDtypeStruct(x.shape, F32),
        compiler_params=pltpu.CompilerParams(
            dimension_semantics=("arbitrary", "arbitrary"), vmem_limit_bytes=VMEM_LIMIT),
        name=name,
    )(block_expert, first, valid, x, g, w1, w3, w2)


def _dense_ffn(h2, g, w1, w3, w2, j):
    t = h2.shape[0]
    tm = min(FFN_TM, t)
    nb = t // tm
    block_expert = jnp.zeros((nb,), jnp.int32)
    first = jnp.zeros((nb,), jnp.int32).at[0].set(1)
    valid = jnp.full((nb,), tm, jnp.int32)
    return _swiglu(h2, g, block_expert, first, valid,
                   w1[:, None], w3[:, None], w2[:, None], j, tm, (tm,), False, "dense_ffn")


def _token_major_index(idx, c, n_tokens):
    return idx + (pl.ds(c, n_tokens, stride=SUBLANES), slice(None))


def _load_token_major(ref, idx, n_tokens, d):
    assert d == SUBLANES * LANES
    return jnp.concatenate(
        [ref[_token_major_index(idx, c, n_tokens)] for c in range(SUBLANES)], axis=-1)


def _store_token_major(ref, idx, value):
    n_tokens, d = value.shape
    assert d == SUBLANES * LANES
    for c in range(SUBLANES):
        ref[_token_major_index(idx, c, n_tokens)] = value[:, c * LANES:(c + 1) * LANES]


def _token_rows(t):
    return pl.ds(pl.multiple_of(t * SUBLANES, SUBLANES), SUBLANES)


def _expert_ffn(xs, g, block_expert, valid, w1, w3, w2, j):
    first = jnp.concatenate(
        [jnp.ones((1,), jnp.int32), (block_expert[1:] != block_expert[:-1]).astype(jnp.int32)])
    row_paths = tuple(range(MOE_ROW_STEP, MOE_TM + 1, MOE_ROW_STEP))
    return _swiglu(xs, g, block_expert, first, valid, w1, w3, w2, j, MOE_TM, row_paths, True,
                   "expert_ffn")


def _router_kernel(x_ref, g_ref, wcat_ref, eid_ref, gate_ref, rank_ref, cnt_ref, carry_ref):
    tm = x_ref.shape[0]

    @pl.when(pl.program_id(0) == 0)
    def _():
        carry_ref[...] = jnp.zeros_like(carry_ref)

    xn = _rms_norm_rows(x_ref[...], g_ref[...])
    x_hi = xn.astype(BF16)
    x_lo = (xn - x_hi.astype(F32)).astype(BF16)
    nt_dims = (((1,), (1,)), ((), ()))
    wcat = wcat_ref[...]
    l_hi = lax.dot_general(wcat, x_hi, nt_dims, preferred_element_type=F32)
    l_lo = lax.dot_general(wcat, x_lo, nt_dims, preferred_element_type=F32)
    logits = l_hi[0:N_EXPERTS] + l_hi[N_EXPERTS:] + l_lo[0:N_EXPERTS]

    sub = lax.broadcasted_iota(jnp.int32, logits.shape, 0)
    m1 = jnp.max(logits, axis=0, keepdims=True)
    i1 = jnp.min(jnp.where(logits == m1, sub, N_EXPERTS), axis=0, keepdims=True)
    rest = jnp.where(sub == i1, -jnp.inf, logits)
    m2 = jnp.max(rest, axis=0, keepdims=True)
    i2 = jnp.min(jnp.where(rest == m2, sub, N_EXPERTS), axis=0, keepdims=True)
    e2 = jnp.exp(m2 - m1)
    denom = 1.0 + e2
    gate_ref[...] = jnp.concatenate([1.0 / denom, e2 / denom], axis=0)
    eid_ref[...] = jnp.concatenate([i1, i2], axis=0)

    sel1 = sub == i1
    sel2 = sub == i2
    onehot = jnp.where(jnp.logical_or(sel1, sel2), 1.0, 0.0)
    r_io = lax.broadcasted_iota(jnp.int32, (tm, tm), 0)
    c_io = lax.broadcasted_iota(jnp.int32, (tm, tm), 1)
    tri = jnp.where(r_io < c_io, 1.0, 0.0).astype(BF16)
    before = jnp.dot(onehot.astype(BF16), tri, preferred_element_type=F32) + carry_ref[...]
    rank1 = jnp.sum(jnp.where(sel1, before, 0.0), axis=0, keepdims=True)
    rank2 = jnp.sum(jnp.where(sel2, before, 0.0), axis=0, keepdims=True)
    rank_ref[...] = jnp.concatenate([rank1, rank2], axis=0).astype(jnp.int32)
    total = carry_ref[...] + jnp.sum(onehot, axis=1, keepdims=True)
    carry_ref[...] = total
    cnt_ref[...] = total.astype(jnp.int32)


def _router(h2, g, wcat):
    t, d = h2.shape
    tm = min(ROUTE_TM, t)
    pair = lambda dt: jax.ShapeDtypeStruct((TOP_K, t), dt)
    return pl.pallas_call(
        _router_kernel,
        grid=(t // tm,),
        in_specs=[
            pl.BlockSpec((tm, d), lambda i: (i, 0)),
            pl.BlockSpec((1, d), lambda i: (0, 0)),
            pl.BlockSpec((2 * N_EXPERTS, d), lambda i: (0, 0)),
        ],
        out_specs=[
            pl.BlockSpec((TOP_K, tm), lambda i: (0, i)),
            pl.BlockSpec((TOP_K, tm), lambda i: (0, i)),
            pl.BlockSpec((TOP_K, tm), lambda i: (0, i)),
            pl.BlockSpec((N_EXPERTS, 1), lambda i: (0, 0)),
        ],
        out_shape=[pair(jnp.int32), pair(F32), pair(jnp.int32),
                   jax.ShapeDtypeStruct((N_EXPERTS, 1), jnp.int32)],
        scratch_shapes=[pltpu.VMEM((N_EXPERTS, 1), F32)],
        compiler_params=pltpu.CompilerParams(dimension_semantics=("arbitrary",)),
        name="router",
    )(h2, g, wcat)


def _tile_indices(idx, tm):
    k, t = idx.shape
    return idx.reshape(k, t // tm, tm).transpose(1, 0, 2).reshape(t // tm, 1, k * tm)


def _start_row_copies(n_rows, row_copy):
    def body(c, carry):
        for u in range(ISSUE_UNROLL):
            for k in range(TOP_K):
                row_copy(c * ISSUE_UNROLL + u, k).start(priority=k)
        return carry

    lax.fori_loop(0, n_rows // ISSUE_UNROLL, body, 0)


def _dispatch_kernel(zrow_ref, dest_ref, x_ref, g_ref, xs_ref, xn_scr, zero_scr, sem, zsem):
    i = pl.program_id(0)
    tm = x_ref.shape[0]
    slot = i % 2

    @pl.when(i == 0)
    def _():
        zero_scr[...] = jnp.zeros_like(zero_scr)
        zcopies = [
            pltpu.make_async_copy(
                zero_scr,
                xs_ref.at[pl.ds(pl.multiple_of(jnp.maximum(zrow_ref[n], 0) * SUBLANES, MOE_TM),
                                MOE_TM * SUBLANES)],
                zsem)
            for n in range(2 * N_EXPERTS)]
        for n, c in enumerate(zcopies):
            pl.when(zrow_ref[n] >= 0)(c.start)
        for n, c in enumerate(zcopies):
            pl.when(zrow_ref[n] >= 0)(c.wait)

    _store_token_major(xn_scr, (slot,), _rms_norm_rows(x_ref[...], g_ref[...]))

    def row_copy(t, k):
        d = dest_ref[0, 0, k * tm + t]
        return pltpu.make_async_copy(
            xn_scr.at[slot, _token_rows(t)], xs_ref.at[_token_rows(d)], sem.at[slot])

    _start_row_copies(tm, row_copy)

    def wait_slot(s):
        for _ in range(TOP_K):
            pltpu.make_async_copy(
                xn_scr.at[s], xs_ref.at[pl.ds(0, tm * SUBLANES)], sem.at[s]).wait()

    @pl.when(i > 0)
    def _():
        wait_slot(1 - slot)

    @pl.when(i == pl.num_programs(0) - 1)
    def _():
        wait_slot(slot)


def _dispatch(h2, g, dest, zrows, cap):
    t, d = h2.shape
    tm = min(DISPATCH_TM, t)
    dest3 = _tile_indices(dest, tm)
    grid_spec = pltpu.PrefetchScalarGridSpec(
        num_scalar_prefetch=1,
        grid=(t // tm,),
        in_specs=[
            pl.BlockSpec((1, 1, TOP_K * tm), lambda i, z: (i, 0, 0), memory_space=pltpu.SMEM),
            pl.BlockSpec((tm, d), lambda i, z: (i, 0)),
            pl.BlockSpec((1, d), lambda i, z: (0, 0)),
        ],
        out_specs=pl.BlockSpec(memory_space=pl.ANY),
        scratch_shapes=[pltpu.VMEM((2, tm * SUBLANES, LANES), F32),
                        pltpu.VMEM((MOE_TM * SUBLANES, LANES), F32),
                        pltpu.SemaphoreType.DMA((2,)), pltpu.SemaphoreType.DMA],
    )
    return pl.pallas_call(
        _dispatch_kernel,
        grid_spec=grid_spec,
        out_shape=jax.ShapeDtypeStruct((cap * SUBLANES, LANES), F32),
        compiler_params=pltpu.CompilerParams(
            dimension_semantics=("arbitrary",), vmem_limit_bytes=VMEM_LIMIT),
        name="dispatch",
    )(zrows, dest3, h2, g)


def _combine_kernel(pos_ref, posn_ref, h_ref, gate_ref, fg_ref, ys_ref, o_ref, buf, sem,
                    *, final_norm):
    i = pl.program_id(0)
    n = pl.num_programs(0)
    tm = h_ref.shape[0]
    slot = i % 2

    def start_gather(p_ref, s):
        def row_copy(t, k):
            p = p_ref[0, 0, k * tm + t]
            return pltpu.make_async_copy(
                ys_ref.at[_token_rows(p)], buf.at[s, k, _token_rows(t)], sem.at[s])
        _start_row_copies(tm, row_copy)

    @pl.when(i == 0)
    def _():
        start_gather(pos_ref, slot)

    @pl.when(i + 1 < n)
    def _():
        start_gather(posn_ref, 1 - slot)

    for k in range(TOP_K):
        pltpu.make_async_copy(
            ys_ref.at[pl.ds(0, tm * SUBLANES)], buf.at[slot, k], sem.at[slot]).wait()
    gate = gate_ref[...]
    d = h_ref.shape[1]
    out = (h_ref[...] + gate[:, 0:1] * _load_token_major(buf, (slot, 0), tm, d)
           + gate[:, 1:2] * _load_token_major(buf, (slot, 1), tm, d))
    if final_norm:
        out = _rms_norm_rows(out, fg_ref[...])
    o_ref[...] = out


def _combine(h2, gates, pos, ys, final_g, final_norm):
    t, d = h2.shape
    tm = min(COMBINE_TM, t)
    nt = t // tm
    pos3 = _tile_indices(pos, tm)
    return pl.pallas_call(
        functools.partial(_combine_kernel, final_norm=final_norm),
        grid=(nt,),
        in_specs=[
            pl.BlockSpec((1, 1, TOP_K * tm), lambda i: (i, 0, 0), memory_space=pltpu.SMEM),
            pl.BlockSpec((1, 1, TOP_K * tm), lambda i: (jnp.minimum(i + 1, nt - 1), 0, 0),
                         memory_space=pltpu.SMEM),
            pl.BlockSpec((tm, d), lambda i: (i, 0)),
            pl.BlockSpec((tm, TOP_K), lambda i: (i, 0)),
            pl.BlockSpec((1, d), lambda i: (0, 0)),
            pl.BlockSpec(memory_space=pl.ANY),
        ],
        out_specs=pl.BlockSpec((tm, d), lambda i: (i, 0)),
        out_shape=jax.ShapeDtypeStruct((t, d), F32),
        scratch_shapes=[pltpu.VMEM((2, TOP_K, tm * SUBLANES, LANES), F32),
                        pltpu.SemaphoreType.DMA((2,))],
        compiler_params=pltpu.CompilerParams(
            dimension_semantics=("arbitrary",), vmem_limit_bytes=VMEM_LIMIT),
        name="combine",
    )(pos3, pos3, h2, gates, final_g, ys)


def _moe(h2, g, wcat, w1, w3, w2, j, final_g, final_norm):
    t, d = h2.shape
    eid, gates, rank, counts = _router(h2, g, wcat)
    n_assign = t * TOP_K
    n_blocks = -(-n_assign // MOE_TM) + N_EXPERTS
    cap = n_blocks * MOE_TM
    counts = counts[:, 0]
    padded = (counts + MOE_TM - 1) // MOE_TM * MOE_TM
    pad_end = jnp.cumsum(padded)
    pad_start = pad_end - padded
    experts = jnp.arange(N_EXPERTS, dtype=jnp.int32)[:, None, None]
    dest = rank + jnp.sum(jnp.where(eid[None] == experts, pad_start[:, None, None], 0), axis=0)
    block_start = jnp.arange(n_blocks, dtype=jnp.int32) * MOE_TM
    block_expert = jnp.minimum(
        jnp.sum(block_start[:, None] >= pad_end[None, :], axis=1), N_EXPERTS - 1
    ).astype(jnp.int32)
    valid = jnp.clip(counts[block_expert] - (block_start - pad_start[block_expert]),
                     0, MOE_TM).astype(jnp.int32)
    tail_rows = pad_end[-1] + jnp.arange(N_EXPERTS, dtype=jnp.int32) * MOE_TM
    zrows = jnp.concatenate([
        jnp.where(padded > 0, pad_end - MOE_TM, -1),
        jnp.where(tail_rows < cap, tail_rows, -1),
    ]).astype(jnp.int32)
    xs = _dispatch(h2, g, dest, zrows, cap)
    ys = _expert_ffn(xs, g, block_expert, valid, w1, w3, w2, j)
    return _combine(h2, gates.T, dest, ys, final_g, final_norm)


def _gate_blocks(w_r, w_i):
    nh, hd, _ = w_r.shape
    per = MXU_DIM // hd
    eye = jnp.eye(per, dtype=w_r.dtype)

    def blocks(w):
        wb = w.reshape(nh // per, per, hd, hd)
        return jnp.einsum("cpde,pq->cpdqe", wb, eye).reshape(nh // per, MXU_DIM, MXU_DIM)

    return jnp.concatenate([blocks(w_r), blocks(w_i)], axis=-1)


def _group_matrices(d_mix):
    grp = jnp.arange(d_mix) // HEAD_DIM
    cols = jnp.arange(LANES)
    g1 = jnp.where(grp[:, None] == cols[None, :], 1.0 / HEAD_DIM, 0.0).astype(BF16)
    e = jnp.where(cols[:, None] == grp[None, :], 1.0, 0.0).astype(BF16)
    return g1, jnp.concatenate([e, e], axis=0)


def kernel(x, norm_mix_g, w_in, conv_a_w, conv_a_b, conv_b_w, conv_b_b, lru_w_r, lru_b_r,
           lru_w_i, lru_b_i, lru_lambda, mix_out_g, w_out, norm_ffn_g, ffn_w1, ffn_w3, ffn_w2,
           router_w, expert_w1, expert_w3, expert_w2, final_g):
    b, s, d = x.shape
    depth = w_in.shape[0]
    assert depth % 2 == 0
    g1, e2 = _group_matrices(D_CONV + D_LRU)
    row = lambda v: v.reshape(1, -1)
    h = x
    for layer in range(depth):
        p = {
            "norm_mix_g": row(norm_mix_g[layer]),
            "w_in": w_in[layer].astype(BF16),
            "conv_a_w": conv_a_w[layer], "conv_a_b": row(conv_a_b[layer]),
            "conv_b_w": conv_b_w[layer], "conv_b_b": row(conv_b_b[layer]),
            "w_gate": _gate_blocks(lru_w_r[layer], lru_w_i[layer]).astype(BF16),
            "b_r": row(lru_b_r[layer]), "b_i": row(lru_b_i[layer]), "lam": row(lru_lambda[layer]),
            "g_mix": row(mix_out_g[layer]), "g1": g1, "e2": e2,
            "w_out": w_out[layer].astype(BF16),
        }
        h = _mixer(h, p, layer == 0, layer == depth - 1)
        h2 = h.reshape(b * s, d)
        gf = row(norm_ffn_g[layer])
        j = layer // 2
        if layer % 2 == 0:
            h2 = _dense_ffn(h2, gf, ffn_w1, ffn_w3, ffn_w2, j)
        else:
            w_hi = router_w[j].astype(BF16)
            w_lo = (router_w[j] - w_hi.astype(F32)).astype(BF16)
            h2 = _moe(h2, gf, jnp.concatenate([w_hi.T, w_lo.T], axis=0),
                      expert_w1, expert_w3, expert_w2, j, row(final_g), layer == depth - 1)
        h = h2.reshape(b, s, d)
    return h
```

```python
import functools

import jax
import jax.numpy as jnp
from jax import lax
from jax.experimental import pallas as pl
from jax.experimental.pallas import tpu as pltpu

EPS = 1e-6
HEAD_DIM = 64
D_CONV = 512
D_LRU = 512
LRU_C = 8.0
N_EXPERTS = 8
TOP_K = 2

SUBLANES = 8
LANES = 128
MXU_DIM = 256

MIX_TS = 1024
MIX_SPLIT = 2
FFN_TM = 1024
FFN_TF = 512
MOE_TM = 1024
MOE_ROW_STEP = 256
ROUTE_TM = 512
DISPATCH_TM = 1024
COMBINE_TM = 512
ISSUE_UNROLL = 8
VMEM_LIMIT = 56 * 1024 * 1024

F32 = jnp.float32
BF16 = jnp.bfloat16


def _rms_norm_rows(x, g):
    return x * lax.rsqrt(jnp.mean(x * x, axis=-1, keepdims=True) + EPS) * g


def _sigmoid(x):
    return 1.0 / (1.0 + jnp.exp(-x))


def _gelu_tanh(x):
    return 0.5 * x * (1.0 + jnp.tanh(0.7978845608028654 * (x + 0.044715 * (x * x * x))))


def _phases(v, g):
    return [v[r * g:(r + 1) * g] for r in range(SUBLANES)]


def _group_shift(v, first_row):
    rolled = pltpu.roll(v, 1, 0)
    rows = lax.broadcasted_iota(jnp.int32, (SUBLANES, v.shape[1]), 0)
    head = jnp.where(rows == 0, jnp.broadcast_to(first_row, (SUBLANES, v.shape[1])),
                     rolled[0:SUBLANES])
    return jnp.concatenate([head, rolled[SUBLANES:]], axis=0)


def _causal_conv(phases, tail, w_ref, b_ref):
    taps = w_ref.shape[0]
    bias = b_ref[...]
    shifted = {}

    def phase(idx):
        if idx >= 0:
            return phases[idx]
        if idx not in shifted:
            src = idx + SUBLANES
            shifted[idx] = _group_shift(phases[src], tail[src:src + 1])
        return shifted[idx]

    out = []
    for r in range(SUBLANES):
        acc = w_ref[taps - 1:taps, :] * phases[r] + bias
        for k in range(taps - 1):
            acc = acc + w_ref[k:k + 1, :] * phase(r - (taps - 1) + k)
        out.append(acc)
    return out


def _last_group_rows(phases):
    return jnp.concatenate([p[p.shape[0] - 1:p.shape[0]] for p in phases], axis=0)


def _to_phase_major(src_ref, perm_scr):
    ts, d = src_ref.shape
    grp = ts // SUBLANES
    for c in range(d // LANES):
        perm_scr[c] = src_ref[:, c * LANES:(c + 1) * LANES]
    return jnp.concatenate(
        [jnp.concatenate([perm_scr[c, pl.ds(r, grp, stride=SUBLANES), :]
                          for c in range(d // LANES)], axis=-1)
         for r in range(SUBLANES)], axis=0)


def _store_time_major(dst_ref, value, perm_scr):
    ts, d = value.shape
    grp = ts // SUBLANES
    for c in range(d // LANES):
        for ph in range(SUBLANES):
            perm_scr[c, pl.ds(ph, grp, stride=SUBLANES), :] = (
                value[ph * grp:(ph + 1) * grp, c * LANES:(c + 1) * LANES])
    dst_ref[...] = jnp.concatenate([perm_scr[c] for c in range(d // LANES)], axis=-1)


def _mixer_kernel(h_ref, gn_ref, win_ref, caw_ref, cab_ref, cbw_ref, cbb_ref, wg_ref,
                  br_ref, bi_ref, lam_ref, gmix_ref, g1_ref, e2_ref, wout_ref,
                  o_ref, taila_ref, tailb_ref, hstate_ref, perm_scr,
                  *, time_major_in, time_major_out):
    ts = h_ref.shape[0]
    grp = ts // SUBLANES

    @pl.when(pl.program_id(1) == 0)
    def _():
        taila_ref[...] = jnp.zeros_like(taila_ref)
        tailb_ref[...] = jnp.zeros_like(tailb_ref)
        hstate_ref[...] = jnp.zeros_like(hstate_ref)

    def project(x):
        xn = _rms_norm_rows(x, gn_ref[...])
        return jnp.dot(xn.astype(BF16), win_ref[...], preferred_element_type=F32)

    def mix_rows(x, z, taila, tailb, c0):
        grp = x.shape[0] // SUBLANES
        xa = z[:, 0:D_CONV]
        gb = z[:, D_CONV:2 * D_CONV]
        gc = z[:, 2 * D_CONV:3 * D_CONV]
        xr = z[:, 3 * D_CONV:3 * D_CONV + D_LRU]
        gr = z[:, 3 * D_CONV + D_LRU:]

        ua = _phases(gc * xa, grp)
        ca = jnp.concatenate(_causal_conv(ua, taila, caw_ref, cab_ref), axis=0)
        taila_new = _last_group_rows(ua)
        ya = gb * ca

        xrp = _phases(xr, grp)
        xc = jnp.concatenate(_causal_conv(xrp, tailb, cbw_ref, cbb_ref), axis=0)
        tailb_new = _last_group_rows(xrp)

        xcb = xc.astype(BF16)
        gz = [jnp.dot(xcb[:, c * MXU_DIM:(c + 1) * MXU_DIM], wg_ref[c], preferred_element_type=F32)
              for c in range(D_LRU // MXU_DIM)]
        gz_r = jnp.concatenate([g[:, 0:MXU_DIM] for g in gz], axis=-1)
        gz_i = jnp.concatenate([g[:, MXU_DIM:] for g in gz], axis=-1)
        r = _sigmoid(gz_r + br_ref[...])
        i = _sigmoid(gz_i + bi_ref[...])
        nl = -lam_ref[...]
        softplus = jnp.maximum(nl, 0.0) + jnp.log1p(jnp.exp(-jnp.abs(nl)))
        log_a = (-LRU_C) * r * softplus
        a = jnp.exp(log_a)
        u = jnp.sqrt(1.0 - a * a) * (i * xc)

        ap = _phases(a, grp)
        up = _phases(u, grp)
        hloc = [up[0]]
        ploc = [ap[0]]
        for ph in range(1, SUBLANES):
            hloc.append(ap[ph] * hloc[ph - 1] + up[ph])
            ploc.append(ap[ph] * ploc[ph - 1])
        ps = ploc[SUBLANES - 1]
        hs = hloc[SUBLANES - 1]
        rows = lax.broadcasted_iota(jnp.int32, ps.shape, 0)
        d = 1
        while d < grp:
            m = rows >= d
            hs = jnp.where(m, ps * pltpu.roll(hs, d, 0) + hs, hs)
            ps = jnp.where(m, ps * pltpu.roll(ps, d, 0), ps)
            d *= 2
        after = ps * c0 + hs
        cin = _group_shift(after, c0)
        c_new = after[grp - 1:grp]
        hh = jnp.concatenate([hloc[ph] + ploc[ph] * cin for ph in range(SUBLANES)], axis=0)
        yr = hh * _gelu_tanh(gr)

        y = jnp.concatenate([ya, yr], axis=-1)
        gm = jnp.dot((y * y).astype(BF16), g1_ref[...], preferred_element_type=F32)
        rs = lax.rsqrt(gm + EPS)
        rs_hi = rs.astype(BF16)
        rs_lo = (rs - rs_hi.astype(F32)).astype(BF16)
        rs_full = jnp.dot(jnp.concatenate([rs_hi, rs_lo], axis=-1), e2_ref[...],
                          preferred_element_type=F32)
        yn = y * rs_full * gmix_ref[...]
        out = x + jnp.dot(yn.astype(BF16), wout_ref[...], preferred_element_type=F32)
        return out, taila_new, tailb_new, c_new

    x_tile = _to_phase_major(h_ref, perm_scr) if time_major_in else h_ref[...]
    taila, tailb, c0 = taila_ref[...], tailb_ref[...], hstate_ref[0:1, :]
    sub = grp // MIX_SPLIT
    xps = [jnp.concatenate([x_tile[r * grp + part * sub:r * grp + (part + 1) * sub]
                            for r in range(SUBLANES)], axis=0) for part in range(MIX_SPLIT)]
    zs = [project(xp) for xp in xps]
    outs = []
    for part in range(MIX_SPLIT):
        op, taila, tailb, c0 = mix_rows(xps[part], zs[part], taila, tailb, c0)
        outs.append(op)
    taila_ref[...] = taila
    tailb_ref[...] = tailb
    hstate_ref[...] = jnp.broadcast_to(c0, hstate_ref.shape)
    out = jnp.concatenate([outs[part][r * sub:(r + 1) * sub]
                           for r in range(SUBLANES) for part in range(MIX_SPLIT)], axis=0)
    if time_major_out:
        _store_time_major(o_ref, out, perm_scr)
    else:
        o_ref[...] = out


def _mixer(h, p, time_major_in, time_major_out):
    b, s, d = h.shape
    ts = MIX_TS
    assert s % ts == 0 and ts % (SUBLANES * SUBLANES) == 0
    d_in = p["w_in"].shape[1]
    d_mix = D_CONV + D_LRU
    const = lambda shape: pl.BlockSpec(shape, lambda bi, j: (0,) * len(shape))
    return pl.pallas_call(
        functools.partial(_mixer_kernel, time_major_in=time_major_in,
                          time_major_out=time_major_out),
        grid=(b, s // ts),
        in_specs=[
            pl.BlockSpec((None, ts, d), lambda bi, j: (bi, j, 0)),
            const((1, d)), const((d, d_in)),
            const((3, D_CONV)), const((1, D_CONV)), const((4, D_LRU)), const((1, D_LRU)),
            const((D_LRU // MXU_DIM, MXU_DIM, 2 * MXU_DIM)),
            const((1, D_LRU)), const((1, D_LRU)), const((1, D_LRU)),
            const((1, d_mix)), const((d_mix, LANES)), const((2 * LANES, d_mix)), const((d_mix, d)),
        ],
        out_specs=pl.BlockSpec((None, ts, d), lambda bi, j: (bi, j, 0)),
        out_shape=jax.ShapeDtypeStruct((b, s, d), F32),
        scratch_shapes=[
            pltpu.VMEM((SUBLANES, D_CONV), F32), pltpu.VMEM((SUBLANES, D_LRU), F32),
            pltpu.VMEM((SUBLANES, D_LRU), F32),
            pltpu.VMEM((d // LANES, ts, LANES), F32),
        ],
        compiler_params=pltpu.CompilerParams(
            dimension_semantics=("arbitrary", "arbitrary"), vmem_limit_bytes=VMEM_LIMIT),
        name="mixer",
    )(h, p["norm_mix_g"], p["w_in"], p["conv_a_w"], p["conv_a_b"], p["conv_b_w"], p["conv_b_b"],
      p["w_gate"], p["b_r"], p["b_i"], p["lam"], p["g_mix"], p["g1"], p["e2"], p["w_out"])


def _swiglu_chunk(xn, w1_ref, w3_ref, w2_ref):
    h1 = jnp.dot(xn, w1_ref[...], preferred_element_type=F32)
    h3 = jnp.dot(xn, w3_ref[...], preferred_element_type=F32)
    g = (h1 * _sigmoid(h1)) * h3
    return jnp.dot(g.astype(BF16), w2_ref[...], preferred_element_type=F32)


def _swiglu_kernel(be_ref, first_ref, valid_ref, x_ref, g_ref, w1_hbm, w3_hbm, w2_hbm, o_ref,
                   xn_scr, acc_scr, res1, res3, res2, st1, st3, st2, sem,
                   *, nf, tf, layer, routed, row_paths):
    assert nf >= 2
    i = pl.program_id(0)
    f = pl.program_id(1)
    valid = valid_ref[i]
    used = valid > 0
    tm, d = xn_scr.shape
    assert row_paths[-1] == tm
    e = be_ref[i]

    def chunk_copies(fc):
        cols = pl.ds(pl.multiple_of(fc * tf, tf), tf)
        return (pltpu.make_async_copy(w1_hbm.at[layer, e, :, cols], st1, sem.at[0]),
                pltpu.make_async_copy(w3_hbm.at[layer, e, :, cols], st3, sem.at[1]),
                pltpu.make_async_copy(w2_hbm.at[layer, e, cols, :], st2, sem.at[2]))

    @pl.when(jnp.logical_and(used, first_ref[i] == 1))
    def _():
        @pl.when(f == 0)
        def _():
            for c in chunk_copies(f):
                c.start()

        for c in chunk_copies(f):
            c.wait()
        res1[f] = st1[...].astype(BF16)
        res3[f] = st3[...].astype(BF16)
        res2[f] = st2[...].astype(BF16)

        @pl.when(f + 1 < nf)
        def _():
            for c in chunk_copies(f + 1):
                c.start()

    w_refs = (res1.at[f], res3.at[f], res2.at[f])

    def load_xn(m):
        if routed:
            return _load_token_major(x_ref, (), m, d).astype(BF16)
        return _rms_norm_rows(x_ref[...], g_ref[...]).astype(BF16)

    def finish(y, m):
        if routed:
            _store_token_major(o_ref, (), y)
            if m < tm:
                o_ref[m * SUBLANES:, :] = jnp.zeros(((tm - m) * SUBLANES, LANES), F32)
        else:
            o_ref[...] = x_ref[...] + y

    def row_path(m, in_path):
        @pl.when(jnp.logical_and(in_path, f == 0))
        def _():
            xn = load_xn(m)
            xn_scr[0:m] = xn
            acc_scr[0:m] = _swiglu_chunk(xn, *w_refs)

        @pl.when(jnp.logical_and(in_path, jnp.logical_and(f > 0, f < nf - 1)))
        def _():
            acc_scr[0:m] += _swiglu_chunk(xn_scr[0:m], *w_refs)

        @pl.when(jnp.logical_and(in_path, f == nf - 1))
        def _():
            finish(acc_scr[0:m] + _swiglu_chunk(xn_scr[0:m], *w_refs), m)

    lo = 0
    for m in row_paths:
        row_path(m, jnp.logical_and(valid > lo, valid <= m))
        lo = m

    @pl.when(jnp.logical_and(jnp.logical_not(used), f == nf - 1))
    def _():
        o_ref[...] = jnp.zeros_like(o_ref)


def _swiglu(x, g, block_expert, first, valid, w1, w3, w2, layer, tm, row_paths, routed, name):
    d, ff = w1.shape[2], w1.shape[3]
    tf = FFN_TF
    nf = ff // tf
    rows = x.shape[0] // SUBLANES if routed else x.shape[0]
    blk = (tm * SUBLANES, LANES) if routed else (tm, d)
    grid_spec = pltpu.PrefetchScalarGridSpec(
        num_scalar_prefetch=3,
        grid=(rows // tm, nf),
        in_specs=[
            pl.BlockSpec(blk, lambda i, f, *_: (i, 0)),
            pl.BlockSpec((1, d), lambda i, f, *_: (0, 0)),
            pl.BlockSpec(memory_space=pl.ANY),
            pl.BlockSpec(memory_space=pl.ANY),
            pl.BlockSpec(memory_space=pl.ANY),
        ],
        out_specs=pl.BlockSpec(blk, lambda i, f, *_: (i, 0)),
        scratch_shapes=[
            pltpu.VMEM((tm, d), BF16), pltpu.VMEM((tm, d), F32),
            pltpu.VMEM((nf, d, tf), BF16), pltpu.VMEM((nf, d, tf), BF16),
            pltpu.VMEM((nf, tf, d), BF16),
            pltpu.VMEM((d, tf), F32), pltpu.VMEM((d, tf), F32), pltpu.VMEM((tf, d), F32),
            pltpu.SemaphoreType.DMA((3,)),
        ],
    )
    return pl.pallas_call(
        functools.partial(_swiglu_kernel, nf=nf, tf=tf, layer=layer, routed=routed,
                          row_paths=row_paths),
        grid_spec=grid_spec,
        out_shape=jax.ShapeDtypeStruct(x.shape, F32),
        compiler_params=pltpu.CompilerParams(
            dimension_semantics=("arbitrary", "arbitrary"), vmem_limit_bytes=VMEM_LIMIT),
        name=name,
    )(block_expert, first, valid, x, g, w1, w3, w2)


def _dense_ffn(h2, g, w1, w3, w2, j):
    t = h2.shape[0]
    tm = min(FFN_TM, t)
    nb = t // tm
    block_expert = jnp.zeros((nb,), jnp.int32)
    first = jnp.zeros((nb,), jnp.int32).at[0].set(1)
    valid = jnp.full((nb,), tm, jnp.int32)
    return _swiglu(h2, g, block_expert, first, valid,
                   w1[:, None], w3[:, None], w2[:, None], j, tm, (tm,), False, "dense_ffn")


def _token_major_index(idx, c, n_tokens):
    return idx + (pl.ds(c, n_tokens, stride=SUBLANES), slice(None))


def _load_token_major(ref, idx, n_tokens, d):
    assert d == SUBLANES * LANES
    return jnp.concatenate(
        [ref[_token_major_index(idx, c, n_tokens)] for c in range(SUBLANES)], axis=-1)


def _store_token_major(ref, idx, value):
    n_tokens, d = value.shape
    assert d == SUBLANES * LANES
    for c in range(SUBLANES):
        ref[_token_major_index(idx, c, n_tokens)] = value[:, c * LANES:(c + 1) * LANES]


def _token_rows(t):
    return pl.ds(pl.multiple_of(t * SUBLANES, SUBLANES), SUBLANES)


def _expert_ffn(xs, g, block_expert, valid, w1, w3, w2, j):
    first = jnp.concatenate(
        [jnp.ones((1,), jnp.int32), (block_expert[1:] != block_expert[:-1]).astype(jnp.int32)])
    row_paths = tuple(range(MOE_ROW_STEP, MOE_TM + 1, MOE_ROW_STEP))
    return _swiglu(xs, g, block_expert, first, valid, w1, w3, w2, j, MOE_TM, row_paths, True,
                   "expert_ffn")


def _router_kernel(x_ref, g_ref, wcat_ref, eid_ref, gate_ref, rank_ref, cnt_ref, carry_ref):
    tm = x_ref.shape[0]

    @pl.when(pl.program_id(0) == 0)
    def _():
        carry_ref[...] = jnp.zeros_like(carry_ref)

    xn = _rms_norm_rows(x_ref[...], g_ref[...])
    x_hi = xn.astype(BF16)
    x_lo = (xn - x_hi.astype(F32)).astype(BF16)
    nt_dims = (((1,), (1,)), ((), ()))
    wcat = wcat_ref[...]
    l_hi = lax.dot_general(wcat, x_hi, nt_dims, preferred_element_type=F32)
    l_lo = lax.dot_general(wcat, x_lo, nt_dims, preferred_element_type=F32)
    logits = l_hi[0:N_EXPERTS] + l_hi[N_EXPERTS:] + l_lo[0:N_EXPERTS]

    sub = lax.broadcasted_iota(jnp.int32, logits.shape, 0)
    m1 = jnp.max(logits, axis=0, keepdims=True)
    i1 = jnp.min(jnp.where(logits == m1, sub, N_EXPERTS), axis=0, keepdims=True)
    rest = jnp.where(sub == i1, -jnp.inf, logits)
    m2 = jnp.max(rest, axis=0, keepdims=True)
    i2 = jnp.min(jnp.where(rest == m2, sub, N_EXPERTS), axis=0, keepdims=True)
    e2 = jnp.exp(m2 - m1)
    denom = 1.0 + e2
    gate_ref[...] = jnp.concatenate([1.0 / denom, e2 / denom], axis=0)
    eid_ref[...] = jnp.concatenate([i1, i2], axis=0)

    sel1 = sub == i1
    sel2 = sub == i2
    onehot = jnp.where(jnp.logical_or(sel1, sel2), 1.0, 0.0)
    r_io = lax.broadcasted_iota(jnp.int32, (tm, tm), 0)
    c_io = lax.broadcasted_iota(jnp.int32, (tm, tm), 1)
    tri = jnp.where(r_io < c_io, 1.0, 0.0).astype(BF16)
    before = jnp.dot(onehot.astype(BF16), tri, preferred_element_type=F32) + carry_ref[...]
    rank1 = jnp.sum(jnp.where(sel1, before, 0.0), axis=0, keepdims=True)
    rank2 = jnp.sum(jnp.where(sel2, before, 0.0), axis=0, keepdims=True)
    rank_ref[...] = jnp.concatenate([rank1, rank2], axis=0).astype(jnp.int32)
    total = carry_ref[...] + jnp.sum(onehot, axis=1, keepdims=True)
    carry_ref[...] = total
    cnt_ref[...] = total.astype(jnp.int32)


def _router(h2, g, wcat):
    t, d = h2.shape
    tm = min(ROUTE_TM, t)
    pair = lambda dt: jax.ShapeDtypeStruct((TOP_K, t), dt)
    return pl.pallas_call(
        _router_kernel,
        grid=(t // tm,),
        in_specs=[
            pl.BlockSpec((tm, d), lambda i: (i, 0)),
            pl.BlockSpec((1, d), lambda i: (0, 0)),
            pl.BlockSpec((2 * N_EXPERTS, d), lambda i: (0, 0)),
        ],
        out_specs=[
            pl.BlockSpec((TOP_K, tm), lambda i: (0, i)),
            pl.BlockSpec((TOP_K, tm), lambda i: (0, i)),
            pl.BlockSpec((TOP_K, tm), lambda i: (0, i)),
            pl.BlockSpec((N_EXPERTS, 1), lambda i: (0, 0)),
        ],
        out_shape=[pair(jnp.int32), pair(F32), pair(jnp.int32),
                   jax.ShapeDtypeStruct((N_EXPERTS, 1), jnp.int32)],
        scratch_shapes=[pltpu.VMEM((N_EXPERTS, 1), F32)],
        compiler_params=pltpu.CompilerParams(dimension_semantics=("arbitrary",)),
        name="router",
    )(h2, g, wcat)


def _tile_indices(idx, tm):
    k, t = idx.shape
    return idx.reshape(k, t // tm, tm).transpose(1, 0, 2).reshape(t // tm, 1, k * tm)


def _start_row_copies(n_rows, row_copy):
    def body(c, carry):
        for u in range(ISSUE_UNROLL):
            for k in range(TOP_K):
                row_copy(c * ISSUE_UNROLL + u, k).start(priority=k)
        return carry

    lax.fori_loop(0, n_rows // ISSUE_UNROLL, body, 0)


def _dispatch_kernel(zrow_ref, dest_ref, x_ref, g_ref, xs_ref, xn_scr, zero_scr, sem, zsem):
    i = pl.program_id(0)
    tm = x_ref.shape[0]
    slot = i % 2

    @pl.when(i == 0)
    def _():
        zero_scr[...] = jnp.zeros_like(zero_scr)
        zcopies = [
            pltpu.make_async_copy(
                zero_scr,
                xs_ref.at[pl.ds(pl.multiple_of(jnp.maximum(zrow_ref[n], 0) * SUBLANES, MOE_TM),
                                MOE_TM * SUBLANES)],
                zsem)
            for n in range(2 * N_EXPERTS)]
        for n, c in enumerate(zcopies):
            pl.when(zrow_ref[n] >= 0)(c.start)
        for n, c in enumerate(zcopies):
            pl.when(zrow_ref[n] >= 0)(c.wait)

    _store_token_major(xn_scr, (slot,), _rms_norm_rows(x_ref[...], g_ref[...]))

    def row_copy(t, k):
        d = dest_ref[0, 0, k * tm + t]
        return pltpu.make_async_copy(
            xn_scr.at[slot, _token_rows(t)], xs_ref.at[_token_rows(d)], sem.at[slot])

    _start_row_copies(tm, row_copy)

    def wait_slot(s):
        for _ in range(TOP_K):
            pltpu.make_async_copy(
                xn_scr.at[s], xs_ref.at[pl.ds(0, tm * SUBLANES)], sem.at[s]).wait()

    @pl.when(i > 0)
    def _():
        wait_slot(1 - slot)

    @pl.when(i == pl.num_programs(0) - 1)
    def _():
        wait_slot(slot)


def _dispatch(h2, g, dest, zrows, cap):
    t, d = h2.shape
    tm = min(DISPATCH_TM, t)
    dest3 = _tile_indices(dest, tm)
    grid_spec = pltpu.PrefetchScalarGridSpec(
        num_scalar_prefetch=1,
        grid=(t // tm,),
        in_specs=[
            pl.BlockSpec((1, 1, TOP_K * tm), lambda i, z: (i, 0, 0), memory_space=pltpu.SMEM),
            pl.BlockSpec((tm, d), lambda i, z: (i, 0)),
            pl.BlockSpec((1, d), lambda i, z: (0, 0)),
        ],
        out_specs=pl.BlockSpec(memory_space=pl.ANY),
        scratch_shapes=[pltpu.VMEM((2, tm * SUBLANES, LANES), F32),
                        pltpu.VMEM((MOE_TM * SUBLANES, LANES), F32),
                        pltpu.SemaphoreType.DMA((2,)), pltpu.SemaphoreType.DMA],
    )
    return pl.pallas_call(
        _dispatch_kernel,
        grid_spec=grid_spec,
        out_shape=jax.ShapeDtypeStruct((cap * SUBLANES, LANES), F32),
        compiler_params=pltpu.CompilerParams(
            dimension_semantics=("arbitrary",), vmem_limit_bytes=VMEM_LIMIT),
        name="dispatch",
    )(zrows, dest3, h2, g)


def _combine_kernel(pos_ref, posn_ref, h_ref, gate_ref, fg_ref, ys_ref, o_ref, buf, sem,
                    *, final_norm):
    i = pl.program_id(0)
    n = pl.num_programs(0)
    tm = h_ref.shape[0]
    slot = i % 2

    def start_gather(p_ref, s):
        def row_copy(t, k):
            p = p_ref[0, 0, k * tm + t]
            return pltpu.make_async_copy(
                ys_ref.at[_token_rows(p)], buf.at[s, k, _token_rows(t)], sem.at[s])
        _start_row_copies(tm, row_copy)

    @pl.when(i == 0)
    def _():
        start_gather(pos_ref, slot)

    @pl.when(i + 1 < n)
    def _():
        start_gather(posn_ref, 1 - slot)

    for k in range(TOP_K):
        pltpu.make_async_copy(
            ys_ref.at[pl.ds(0, tm * SUBLANES)], buf.at[slot, k], sem.at[slot]).wait()
    gate = gate_ref[...]
    d = h_ref.shape[1]
    out = (h_ref[...] + gate[:, 0:1] * _load_token_major(buf, (slot, 0), tm, d)
           + gate[:, 1:2] * _load_token_major(buf, (slot, 1), tm, d))
    if final_norm:
        out = _rms_norm_rows(out, fg_ref[...])
    o_ref[...] = out


def _combine(h2, gates, pos, ys, final_g, final_norm):
    t, d = h2.shape
    tm = min(COMBINE_TM, t)
    nt = t // tm
    pos3 = _tile_indices(pos, tm)
    return pl.pallas_call(
        functools.partial(_combine_kernel, final_norm=final_norm),
        grid=(nt,),
        in_specs=[
            pl.BlockSpec((1, 1, TOP_K * tm), lambda i: (i, 0, 0), memory_space=pltpu.SMEM),
            pl.BlockSpec((1, 1, TOP_K * tm), lambda i: (jnp.minimum(i + 1, nt - 1), 0, 0),
                         memory_space=pltpu.SMEM),
            pl.BlockSpec((tm, d), lambda i: (i, 0)),
            pl.BlockSpec((tm, TOP_K), lambda i: (i, 0)),
            pl.BlockSpec((1, d), lambda i: (0, 0)),
            pl.BlockSpec(memory_space=pl.ANY),
        ],
        out_specs=pl.BlockSpec((tm, d), lambda i: (i, 0)),
        out_shape=jax.ShapeDtypeStruct((t, d), F32),
        scratch_shapes=[pltpu.VMEM((2, TOP_K, tm * SUBLANES, LANES), F32),
                        pltpu.SemaphoreType.DMA((2,))],
        compiler_params=pltpu.CompilerParams(
            dimension_semantics=("arbitrary",), vmem_limit_bytes=VMEM_LIMIT),
        name="combine",
    )(pos3, pos3, h2, gates, final_g, ys)


def _moe(h2, g, wcat, w1, w3, w2, j, final_g, final_norm):
    t, d = h2.shape
    eid, gates, rank, counts = _router(h2, g, wcat)
    n_assign = t * TOP_K
    n_blocks = -(-n_assign // MOE_TM) + N_EXPERTS
    cap = n_blocks * MOE_TM
    counts = counts[:, 0]
    padded = (counts + MOE_TM - 1) // MOE_TM * MOE_TM
    pad_end = jnp.cumsum(padded)
    pad_start = pad_end - padded
    experts = jnp.arange(N_EXPERTS, dtype=jnp.int32)[:, None, None]
    dest = rank + jnp.sum(jnp.where(eid[None] == experts, pad_start[:, None, None], 0), axis=0)
    block_start = jnp.arange(n_blocks, dtype=jnp.int32) * MOE_TM
    block_expert = jnp.minimum(
        jnp.sum(block_start[:, None] >= pad_end[None, :], axis=1), N_EXPERTS - 1
    ).astype(jnp.int32)
    valid = jnp.clip(counts[block_expert] - (block_start - pad_start[block_expert]),
                     0, MOE_TM).astype(jnp.int32)
    tail_rows = pad_end[-1] + jnp.arange(N_EXPERTS, dtype=jnp.int32) * MOE_TM
    zrows = jnp.concatenate([
        jnp.where(padded > 0, pad_end - MOE_TM, -1),
        jnp.where(tail_rows < cap, tail_rows, -1),
    ]).astype(jnp.int32)
    xs = _dispatch(h2, g, dest, zrows, cap)
    ys = _expert_ffn(xs, g, block_expert, valid, w1, w3, w2, j)
    return _combine(h2, gates.T, dest, ys, final_g, final_norm)


def _gate_blocks(w_r, w_i):
    nh, hd, _ = w_r.shape
    per = MXU_DIM // hd
    eye = jnp.eye(per, dtype=w_r.dtype)

    def blocks(w):
        wb = w.reshape(nh // per, per, hd, hd)
        return jnp.einsum("cpde,pq->cpdqe", wb, eye).reshape(nh // per, MXU_DIM, MXU_DIM)

    return jnp.concatenate([blocks(w_r), blocks(w_i)], axis=-1)


def _group_matrices(d_mix):
    grp = jnp.arange(d_mix) // HEAD_DIM
    cols = jnp.arange(LANES)
    g1 = jnp.where(grp[:, None] == cols[None, :], 1.0 / HEAD_DIM, 0.0).astype(BF16)
    e = jnp.where(cols[:, None] == grp[None, :], 1.0, 0.0).astype(BF16)
    return g1, jnp.concatenate([e, e], axis=0)


def kernel(x, norm_mix_g, w_in, conv_a_w, conv_a_b, conv_b_w, conv_b_b, lru_w_r, lru_b_r,
           lru_w_i, lru_b_i, lru_lambda, mix_out_g, w_out, norm_ffn_g, ffn_w1, ffn_w3, ffn_w2,
           router_w, expert_w1, expert_w3, expert_w2, final_g):
    b, s, d = x.shape
    depth = w_in.shape[0]
    assert depth % 2 == 0
    g1, e2 = _group_matrices(D_CONV + D_LRU)
    row = lambda v: v.reshape(1, -1)
    h = x
    for layer in range(depth):
        p = {
            "norm_mix_g": row(norm_mix_g[layer]),
            "w_in": w_in[layer].astype(BF16),
            "conv_a_w": conv_a_w[layer], "conv_a_b": row(conv_a_b[layer]),
            "conv_b_w": conv_b_w[layer], "conv_b_b": row(conv_b_b[layer]),
            "w_gate": _gate_blocks(lru_w_r[layer], lru_w_i[layer]).astype(BF16),
            "b_r": row(lru_b_r[layer]), "b_i": row(lru_b_i[layer]), "lam": row(lru_lambda[layer]),
            "g_mix": row(mix_out_g[layer]), "g1": g1, "e2": e2,
            "w_out": w_out[layer].astype(BF16),
        }
        h = _mixer(h, p, layer == 0, layer == depth - 1)
        h2 = h.reshape(b * s, d)
        gf = row(norm_ffn_g[layer])
        j = layer // 2
        if layer % 2 == 0:
            h2 = _dense_ffn(h2, gf, ffn_w1, ffn_w3, ffn_w2, j)
        else:
            w_hi = router_w[j].astype(BF16)
            w_lo = (router_w[j] - w_hi.astype(F32)).astype(BF16)
            h2 = _moe(h2, gf, jnp.concatenate([w_hi.T, w_lo.T], axis=0),
                      expert_w1, expert_w3, expert_w2, j, row(final_g), layer == depth - 1)
        h = h2.reshape(b, s, d)
    return h
```

```python
import functools

import jax
import jax.numpy as jnp
from jax import lax
from jax.experimental import pallas as pl
from jax.experimental.pallas import tpu as pltpu

EPS = 1e-6
HEAD_DIM = 64
D_CONV = 512
D_LRU = 512
LRU_C = 8.0
N_EXPERTS = 8
TOP_K = 2

SUBLANES = 8
LANES = 128
MXU_DIM = 256

MIX_TS = 1024
MIX_SPLIT = 2
FFN_TM = 1024
FFN_TF = 512
FFN_RUNS = 2
MOE_TM = 1024
MOE_ROW_STEP = 256
ROUTE_TM = 512
DISPATCH_TM = 1024
COMBINE_TM = 512
ISSUE_UNROLL = 8
VMEM_LIMIT = 56 * 1024 * 1024

F32 = jnp.float32
BF16 = jnp.bfloat16


def _rms_norm_rows(x, g):
    return x * lax.rsqrt(jnp.mean(x * x, axis=-1, keepdims=True) + EPS) * g


def _sigmoid(x):
    return 1.0 / (1.0 + jnp.exp(-x))


def _gelu_tanh(x):
    return 0.5 * x * (1.0 + jnp.tanh(0.7978845608028654 * (x + 0.044715 * (x * x * x))))


def _phases(v, g):
    return [v[r * g:(r + 1) * g] for r in range(SUBLANES)]


def _group_shift(v, first_row):
    rolled = pltpu.roll(v, 1, 0)
    rows = lax.broadcasted_iota(jnp.int32, (SUBLANES, v.shape[1]), 0)
    head = jnp.where(rows == 0, jnp.broadcast_to(first_row, (SUBLANES, v.shape[1])),
                     rolled[0:SUBLANES])
    return jnp.concatenate([head, rolled[SUBLANES:]], axis=0)


def _causal_conv(phases, tail, w_ref, b_ref):
    taps = w_ref.shape[0]
    bias = b_ref[...]
    shifted = {}

    def phase(idx):
        if idx >= 0:
            return phases[idx]
        if idx not in shifted:
            src = idx + SUBLANES
            shifted[idx] = _group_shift(phases[src], tail[src:src + 1])
        return shifted[idx]

    out = []
    for r in range(SUBLANES):
        acc = w_ref[taps - 1:taps, :] * phases[r] + bias
        for k in range(taps - 1):
            acc = acc + w_ref[k:k + 1, :] * phase(r - (taps - 1) + k)
        out.append(acc)
    return out


def _last_group_rows(phases):
    return jnp.concatenate([p[p.shape[0] - 1:p.shape[0]] for p in phases], axis=0)


def _to_phase_major(src_ref, perm_scr):
    ts, d = src_ref.shape
    grp = ts // SUBLANES
    for c in range(d // LANES):
        perm_scr[c] = src_ref[:, c * LANES:(c + 1) * LANES]
    return jnp.concatenate(
        [jnp.concatenate([perm_scr[c, pl.ds(r, grp, stride=SUBLANES), :]
                          for c in range(d // LANES)], axis=-1)
         for r in range(SUBLANES)], axis=0)


def _store_time_major(dst_ref, value, perm_scr):
    ts, d = value.shape
    grp = ts // SUBLANES
    for c in range(d // LANES):
        for ph in range(SUBLANES):
            perm_scr[c, pl.ds(ph, grp, stride=SUBLANES), :] = (
                value[ph * grp:(ph + 1) * grp, c * LANES:(c + 1) * LANES])
    dst_ref[...] = jnp.concatenate([perm_scr[c] for c in range(d // LANES)], axis=-1)


def _mixer_kernel(h_ref, gn_ref, win_ref, caw_ref, cab_ref, cbw_ref, cbb_ref, wg_ref,
                  br_ref, bi_ref, lam_ref, gmix_ref, g1_ref, e2_ref, wout_ref,
                  o_ref, taila_ref, tailb_ref, hstate_ref, perm_scr,
                  *, time_major_in, time_major_out):
    ts = h_ref.shape[0]
    grp = ts // SUBLANES

    @pl.when(pl.program_id(1) == 0)
    def _():
        taila_ref[...] = jnp.zeros_like(taila_ref)
        tailb_ref[...] = jnp.zeros_like(tailb_ref)
        hstate_ref[...] = jnp.zeros_like(hstate_ref)

    def project(x):
        xn = _rms_norm_rows(x, gn_ref[...])
        return jnp.dot(xn.astype(BF16), win_ref[...], preferred_element_type=F32)

    def mix_rows(x, z, taila, tailb, c0):
        grp = x.shape[0] // SUBLANES
        xa = z[:, 0:D_CONV]
        gb = z[:, D_CONV:2 * D_CONV]
        gc = z[:, 2 * D_CONV:3 * D_CONV]
        xr = z[:, 3 * D_CONV:3 * D_CONV + D_LRU]
        gr = z[:, 3 * D_CONV + D_LRU:]

        ua = _phases(gc * xa, grp)
        ca = jnp.concatenate(_causal_conv(ua, taila, caw_ref, cab_ref), axis=0)
        taila_new = _last_group_rows(ua)
        ya = gb * ca

        xrp = _phases(xr, grp)
        xc = jnp.concatenate(_causal_conv(xrp, tailb, cbw_ref, cbb_ref), axis=0)
        tailb_new = _last_group_rows(xrp)

        xcb = xc.astype(BF16)
        gz = [jnp.dot(xcb[:, c * MXU_DIM:(c + 1) * MXU_DIM], wg_ref[c], preferred_element_type=F32)
              for c in range(D_LRU // MXU_DIM)]
        gz_r = jnp.concatenate([g[:, 0:MXU_DIM] for g in gz], axis=-1)
        gz_i = jnp.concatenate([g[:, MXU_DIM:] for g in gz], axis=-1)
        r = _sigmoid(gz_r + br_ref[...])
        i = _sigmoid(gz_i + bi_ref[...])
        nl = -lam_ref[...]
        softplus = jnp.maximum(nl, 0.0) + jnp.log1p(jnp.exp(-jnp.abs(nl)))
        log_a = (-LRU_C) * r * softplus
        a = jnp.exp(log_a)
        u = jnp.sqrt(1.0 - a * a) * (i * xc)

        ap = _phases(a, grp)
        up = _phases(u, grp)
        hloc = [up[0]]
        ploc = [ap[0]]
        for ph in range(1, SUBLANES):
            hloc.append(ap[ph] * hloc[ph - 1] + up[ph])
            ploc.append(ap[ph] * ploc[ph - 1])
        ps = ploc[SUBLANES - 1]
        hs = hloc[SUBLANES - 1]
        rows = lax.broadcasted_iota(jnp.int32, ps.shape, 0)
        d = 1
        while d < grp:
            m = rows >= d
            hs = jnp.where(m, ps * pltpu.roll(hs, d, 0) + hs, hs)
            ps = jnp.where(m, ps * pltpu.roll(ps, d, 0), ps)
            d *= 2
        after = ps * c0 + hs
        cin = _group_shift(after, c0)
        c_new = after[grp - 1:grp]
        hh = jnp.concatenate([hloc[ph] + ploc[ph] * cin for ph in range(SUBLANES)], axis=0)
        yr = hh * _gelu_tanh(gr)

        y = jnp.concatenate([ya, yr], axis=-1)
        gm = jnp.dot((y * y).astype(BF16), g1_ref[...], preferred_element_type=F32)
        rs = lax.rsqrt(gm + EPS)
        rs_hi = rs.astype(BF16)
        rs_lo = (rs - rs_hi.astype(F32)).astype(BF16)
        rs_full = jnp.dot(jnp.concatenate([rs_hi, rs_lo], axis=-1), e2_ref[...],
                          preferred_element_type=F32)
        yn = y * rs_full * gmix_ref[...]
        out = x + jnp.dot(yn.astype(BF16), wout_ref[...], preferred_element_type=F32)
        return out, taila_new, tailb_new, c_new

    x_tile = _to_phase_major(h_ref, perm_scr) if time_major_in else h_ref[...]
    taila, tailb, c0 = taila_ref[...], tailb_ref[...], hstate_ref[0:1, :]
    sub = grp // MIX_SPLIT
    xps = [jnp.concatenate([x_tile[r * grp + part * sub:r * grp + (part + 1) * sub]
                            for r in range(SUBLANES)], axis=0) for part in range(MIX_SPLIT)]
    zs = [project(xp) for xp in xps]
    outs = []
    for part in range(MIX_SPLIT):
        op, taila, tailb, c0 = mix_rows(xps[part], zs[part], taila, tailb, c0)
        outs.append(op)
    taila_ref[...] = taila
    tailb_ref[...] = tailb
    hstate_ref[...] = jnp.broadcast_to(c0, hstate_ref.shape)
    out = jnp.concatenate([outs[part][r * sub:(r + 1) * sub]
                           for r in range(SUBLANES) for part in range(MIX_SPLIT)], axis=0)
    if time_major_out:
        _store_time_major(o_ref, out, perm_scr)
    else:
        o_ref[...] = out


def _mixer(h, p, time_major_in, time_major_out):
    b, s, d = h.shape
    ts = MIX_TS
    assert s % ts == 0 and ts % (SUBLANES * SUBLANES) == 0
    d_in = p["w_in"].shape[1]
    d_mix = D_CONV + D_LRU
    const = lambda shape: pl.BlockSpec(shape, lambda bi, j: (0,) * len(shape))
    return pl.pallas_call(
        functools.partial(_mixer_kernel, time_major_in=time_major_in,
                          time_major_out=time_major_out),
        grid=(b, s // ts),
        in_specs=[
            pl.BlockSpec((None, ts, d), lambda bi, j: (bi, j, 0)),
            const((1, d)), const((d, d_in)),
            const((3, D_CONV)), const((1, D_CONV)), const((4, D_LRU)), const((1, D_LRU)),
            const((D_LRU // MXU_DIM, MXU_DIM, 2 * MXU_DIM)),
            const((1, D_LRU)), const((1, D_LRU)), const((1, D_LRU)),
            const((1, d_mix)), const((d_mix, LANES)), const((2 * LANES, d_mix)), const((d_mix, d)),
        ],
        out_specs=pl.BlockSpec((None, ts, d), lambda bi, j: (bi, j, 0)),
        out_shape=jax.ShapeDtypeStruct((b, s, d), F32),
        scratch_shapes=[
            pltpu.VMEM((SUBLANES, D_CONV), F32), pltpu.VMEM((SUBLANES, D_LRU), F32),
            pltpu.VMEM((SUBLANES, D_LRU), F32),
            pltpu.VMEM((d // LANES, ts, LANES), F32),
        ],
        compiler_params=pltpu.CompilerParams(
            dimension_semantics=("arbitrary", "arbitrary"), vmem_limit_bytes=VMEM_LIMIT),
        name="mixer",
    )(h, p["norm_mix_g"], p["w_in"], p["conv_a_w"], p["conv_a_b"], p["conv_b_w"], p["conv_b_b"],
      p["w_gate"], p["b_r"], p["b_i"], p["lam"], p["g_mix"], p["g1"], p["e2"], p["w_out"])


def _swiglu_chunk(xn, w1_ref, w3_ref, w2_ref):
    m = xn.shape[0]
    runs = FFN_RUNS if m % (FFN_RUNS * MXU_DIM) == 0 else 1
    parts = [xn[r * (m // runs):(r + 1) * (m // runs)] for r in range(runs)]
    w1, w3, w2 = w1_ref[...], w3_ref[...], w2_ref[...]
    h1 = [jnp.dot(p, w1, preferred_element_type=F32) for p in parts]
    h3 = [jnp.dot(p, w3, preferred_element_type=F32) for p in parts]
    out = [jnp.dot(((a * _sigmoid(a)) * b).astype(BF16), w2, preferred_element_type=F32)
           for a, b in zip(h1, h3)]
    return out[0] if runs == 1 else jnp.concatenate(out, axis=0)


def _swiglu_kernel(be_ref, first_ref, valid_ref, x_ref, g_ref, w1_hbm, w3_hbm, w2_hbm, o_ref,
                   xn_scr, acc_scr, res1, res3, res2, st1, st3, st2, sem,
                   *, nf, tf, layer, routed, row_paths):
    assert nf >= 2
    i = pl.program_id(0)
    f = pl.program_id(1)
    valid = valid_ref[i]
    used = valid > 0
    tm, d = xn_scr.shape
    assert row_paths[-1] == tm
    e = be_ref[i]

    def chunk_copies(fc):
        cols = pl.ds(pl.multiple_of(fc * tf, tf), tf)
        return (pltpu.make_async_copy(w1_hbm.at[layer, e, :, cols], st1, sem.at[0]),
                pltpu.make_async_copy(w3_hbm.at[layer, e, :, cols], st3, sem.at[1]),
                pltpu.make_async_copy(w2_hbm.at[layer, e, cols, :], st2, sem.at[2]))

    @pl.when(jnp.logical_and(used, first_ref[i] == 1))
    def _():
        @pl.when(f == 0)
        def _():
            for c in chunk_copies(f):
                c.start()

        for c in chunk_copies(f):
            c.wait()
        res1[f] = st1[...].astype(BF16)
        res3[f] = st3[...].astype(BF16)
        res2[f] = st2[...].astype(BF16)

        @pl.when(f + 1 < nf)
        def _():
            for c in chunk_copies(f + 1):
                c.start()

    w_refs = (res1.at[f], res3.at[f], res2.at[f])

    def load_xn(m):
        if routed:
            return _load_token_major(x_ref, (), m, d).astype(BF16)
        return _rms_norm_rows(x_ref[...], g_ref[...]).astype(BF16)

    def finish(y, m):
        if routed:
            _store_token_major(o_ref, (), y)
            if m < tm:
                o_ref[m * SUBLANES:, :] = jnp.zeros(((tm - m) * SUBLANES, LANES), F32)
        else:
            o_ref[...] = x_ref[...] + y

    def row_path(m, in_path):
        @pl.when(jnp.logical_and(in_path, f == 0))
        def _():
            xn = load_xn(m)
            xn_scr[0:m] = xn
            acc_scr[0:m] = _swiglu_chunk(xn, *w_refs)

        @pl.when(jnp.logical_and(in_path, jnp.logical_and(f > 0, f < nf - 1)))
        def _():
            acc_scr[0:m] += _swiglu_chunk(xn_scr[0:m], *w_refs)

        @pl.when(jnp.logical_and(in_path, f == nf - 1))
        def _():
            finish(acc_scr[0:m] + _swiglu_chunk(xn_scr[0:m], *w_refs), m)

    lo = 0
    for m in row_paths:
        row_path(m, jnp.logical_and(valid > lo, valid <= m))
        lo = m

    @pl.when(jnp.logical_and(jnp.logical_not(used), f == nf - 1))
    def _():
        o_ref[...] = jnp.zeros_like(o_ref)


def _swiglu(x, g, block_expert, first, valid, w1, w3, w2, layer, tm, row_paths, routed, name):
    d, ff = w1.shape[2], w1.shape[3]
    tf = FFN_TF
    nf = ff // tf
    rows = x.shape[0] // SUBLANES if routed else x.shape[0]
    blk = (tm * SUBLANES, LANES) if routed else (tm, d)
    grid_spec = pltpu.PrefetchScalarGridSpec(
        num_scalar_prefetch=3,
        grid=(rows // tm, nf),
        in_specs=[
            pl.BlockSpec(blk, lambda i, f, *_: (i, 0)),
            pl.BlockSpec((1, d), lambda i, f, *_: (0, 0)),
            pl.BlockSpec(memory_space=pl.ANY),
            pl.BlockSpec(memory_space=pl.ANY),
            pl.BlockSpec(memory_space=pl.ANY),
        ],
        out_specs=pl.BlockSpec(blk, lambda i, f, *_: (i, 0)),
        scratch_shapes=[
            pltpu.VMEM((tm, d), BF16), pltpu.VMEM((tm, d), F32),
            pltpu.VMEM((nf, d, tf), BF16), pltpu.VMEM((nf, d, tf), BF16),
            pltpu.VMEM((nf, tf, d), BF16),
            pltpu.VMEM((d, tf), F32), pltpu.VMEM((d, tf), F32), pltpu.VMEM((tf, d), F32),
            pltpu.SemaphoreType.DMA((3,)),
        ],
    )
    return pl.pallas_call(
        functools.partial(_swiglu_kernel, nf=nf, tf=tf, layer=layer, routed=routed,
                          row_paths=row_paths),
        grid_spec=grid_spec,
        out_shape=jax.ShapeDtypeStruct(x.shape, F32),
        compiler_params=pltpu.CompilerParams(
            dimension_semantics=("arbitrary", "arbitrary"), vmem_limit_bytes=VMEM_LIMIT),
        name=name,
    )(block_expert, first, valid, x, g, w1, w3, w2)


def _dense_ffn(h2, g, w1, w3, w2, j):
    t = h2.shape[0]
    tm = min(FFN_TM, t)
    nb = t // tm
    block_expert = jnp.zeros((nb,), jnp.int32)
    first = jnp.zeros((nb,), jnp.int32).at[0].set(1)
    valid = jnp.full((nb,), tm, jnp.int32)
    return _swiglu(h2, g, block_expert, first, valid,
                   w1[:, None], w3[:, None], w2[:, None], j, tm, (tm,), False, "dense_ffn")


def _token_major_index(idx, c, n_tokens):
    return idx + (pl.ds(c, n_tokens, stride=SUBLANES), slice(None))


def _load_token_major(ref, idx, n_tokens, d):
    assert d == SUBLANES * LANES
    return jnp.concatenate(
        [ref[_token_major_index(idx, c, n_tokens)] for c in range(SUBLANES)], axis=-1)


def _store_token_major(ref, idx, value):
    n_tokens, d = value.shape
    assert d == SUBLANES * LANES
    for c in range(SUBLANES):
        ref[_token_major_index(idx, c, n_tokens)] = value[:, c * LANES:(c + 1) * LANES]


def _token_rows(t):
    return pl.ds(pl.multiple_of(t * SUBLANES, SUBLANES), SUBLANES)


def _expert_ffn(xs, g, block_expert, valid, w1, w3, w2, j):
    first = jnp.concatenate(
        [jnp.ones((1,), jnp.int32), (block_expert[1:] != block_expert[:-1]).astype(jnp.int32)])
    row_paths = tuple(range(MOE_ROW_STEP, MOE_TM + 1, MOE_ROW_STEP))
    return _swiglu(xs, g, block_expert, first, valid, w1, w3, w2, j, MOE_TM, row_paths, True,
                   "expert_ffn")


def _router_kernel(x_ref, g_ref, wcat_ref, eid_ref, gate_ref, rank_ref, cnt_ref, carry_ref):
    tm = x_ref.shape[0]

    @pl.when(pl.program_id(0) == 0)
    def _():
        carry_ref[...] = jnp.zeros_like(carry_ref)

    xn = _rms_norm_rows(x_ref[...], g_ref[...])
    x_hi = xn.astype(BF16)
    x_lo = (xn - x_hi.astype(F32)).astype(BF16)
    nt_dims = (((1,), (1,)), ((), ()))
    wcat = wcat_ref[...]
    l_hi = lax.dot_general(wcat, x_hi, nt_dims, preferred_element_type=F32)
    l_lo = lax.dot_general(wcat, x_lo, nt_dims, preferred_element_type=F32)
    logits = l_hi[0:N_EXPERTS] + l_hi[N_EXPERTS:] + l_lo[0:N_EXPERTS]

    sub = lax.broadcasted_iota(jnp.int32, logits.shape, 0)
    m1 = jnp.max(logits, axis=0, keepdims=True)
    i1 = jnp.min(jnp.where(logits == m1, sub, N_EXPERTS), axis=0, keepdims=True)
    rest = jnp.where(sub == i1, -jnp.inf, logits)
    m2 = jnp.max(rest, axis=0, keepdims=True)
    i2 = jnp.min(jnp.where(rest == m2, sub, N_EXPERTS), axis=0, keepdims=True)
    e2 = jnp.exp(m2 - m1)
    denom = 1.0 + e2
    gate_ref[...] = jnp.concatenate([1.0 / denom, e2 / denom], axis=0)
    eid_ref[...] = jnp.concatenate([i1, i2], axis=0)

    sel1 = sub == i1
    sel2 = sub == i2
    onehot = jnp.where(jnp.logical_or(sel1, sel2), 1.0, 0.0)
    r_io = lax.broadcasted_iota(jnp.int32, (tm, tm), 0)
    c_io = lax.broadcasted_iota(jnp.int32, (tm, tm), 1)
    tri = jnp.where(r_io < c_io, 1.0, 0.0).astype(BF16)
    before = jnp.dot(onehot.astype(BF16), tri, preferred_element_type=F32) + carry_ref[...]
    rank1 = jnp.sum(jnp.where(sel1, before, 0.0), axis=0, keepdims=True)
    rank2 = jnp.sum(jnp.where(sel2, before, 0.0), axis=0, keepdims=True)
    rank_ref[...] = jnp.concatenate([rank1, rank2], axis=0).astype(jnp.int32)
    total = carry_ref[...] + jnp.sum(onehot, axis=1, keepdims=True)
    carry_ref[...] = total
    cnt_ref[...] = total.astype(jnp.int32)


def _router(h2, g, wcat):
    t, d = h2.shape
    tm = min(ROUTE_TM, t)
    pair = lambda dt: jax.ShapeDtypeStruct((TOP_K, t), dt)
    return pl.pallas_call(
        _router_kernel,
        grid=(t // tm,),
        in_specs=[
            pl.BlockSpec((tm, d), lambda i: (i, 0)),
            pl.BlockSpec((1, d), lambda i: (0, 0)),
            pl.BlockSpec((2 * N_EXPERTS, d), lambda i: (0, 0)),
        ],
        out_specs=[
            pl.BlockSpec((TOP_K, tm), lambda i: (0, i)),
            pl.BlockSpec((TOP_K, tm), lambda i: (0, i)),
            pl.BlockSpec((TOP_K, tm), lambda i: (0, i)),
            pl.BlockSpec((N_EXPERTS, 1), lambda i: (0, 0)),
        ],
        out_shape=[pair(jnp.int32), pair(F32), pair(jnp.int32),
                   jax.ShapeDtypeStruct((N_EXPERTS, 1), jnp.int32)],
        scratch_shapes=[pltpu.VMEM((N_EXPERTS, 1), F32)],
        compiler_params=pltpu.CompilerParams(dimension_semantics=("arbitrary",)),
        name="router",
    )(h2, g, wcat)


def _tile_indices(idx, tm):
    k, t = idx.shape
    return idx.reshape(k, t // tm, tm).transpose(1, 0, 2).reshape(t // tm, 1, k * tm)


def _start_row_copies(n_rows, row_copy):
    def body(c, carry):
        for u in range(ISSUE_UNROLL):
            for k in range(TOP_K):
                row_copy(c * ISSUE_UNROLL + u, k).start(priority=k)
        return carry

    lax.fori_loop(0, n_rows // ISSUE_UNROLL, body, 0)


def _dispatch_kernel(zrow_ref, dest_ref, x_ref, g_ref, xs_ref, xn_scr, zero_scr, sem, zsem):
    i = pl.program_id(0)
    tm = x_ref.shape[0]
    slot = i % 2

    @pl.when(i == 0)
    def _():
        zero_scr[...] = jnp.zeros_like(zero_scr)
        zcopies = [
            pltpu.make_async_copy(
                zero_scr,
                xs_ref.at[pl.ds(pl.multiple_of(jnp.maximum(zrow_ref[n], 0) * SUBLANES, MOE_TM),
                                MOE_TM * SUBLANES)],
                zsem)
            for n in range(2 * N_EXPERTS)]
        for n, c in enumerate(zcopies):
            pl.when(zrow_ref[n] >= 0)(c.start)
        for n, c in enumerate(zcopies):
            pl.when(zrow_ref[n] >= 0)(c.wait)

    _store_token_major(xn_scr, (slot,), _rms_norm_rows(x_ref[...], g_ref[...]))

    def row_copy(t, k):
        d = dest_ref[0, 0, k * tm + t]
        return pltpu.make_async_copy(
            xn_scr.at[slot, _token_rows(t)], xs_ref.at[_token_rows(d)], sem.at[slot])

    _start_row_copies(tm, row_copy)

    def wait_slot(s):
        for _ in range(TOP_K):
            pltpu.make_async_copy(
                xn_scr.at[s], xs_ref.at[pl.ds(0, tm * SUBLANES)], sem.at[s]).wait()

    @pl.when(i > 0)
    def _():
        wait_slot(1 - slot)

    @pl.when(i == pl.num_programs(0) - 1)
    def _():
        wait_slot(slot)


def _dispatch(h2, g, dest, zrows, cap):
    t, d = h2.shape
    tm = min(DISPATCH_TM, t)
    dest3 = _tile_indices(dest, tm)
    grid_spec = pltpu.PrefetchScalarGridSpec(
        num_scalar_prefetch=1,
        grid=(t // tm,),
        in_specs=[
            pl.BlockSpec((1, 1, TOP_K * tm), lambda i, z: (i, 0, 0), memory_space=pltpu.SMEM),
            pl.BlockSpec((tm, d), lambda i, z: (i, 0)),
            pl.BlockSpec((1, d), lambda i, z: (0, 0)),
        ],
        out_specs=pl.BlockSpec(memory_space=pl.ANY),
        scratch_shapes=[pltpu.VMEM((2, tm * SUBLANES, LANES), F32),
                        pltpu.VMEM((MOE_TM * SUBLANES, LANES), F32),
                        pltpu.SemaphoreType.DMA((2,)), pltpu.SemaphoreType.DMA],
    )
    return pl.pallas_call(
        _dispatch_kernel,
        grid_spec=grid_spec,
        out_shape=jax.ShapeDtypeStruct((cap * SUBLANES, LANES), F32),
        compiler_params=pltpu.CompilerParams(
            dimension_semantics=("arbitrary",), vmem_limit_bytes=VMEM_LIMIT),
        name="dispatch",
    )(zrows, dest3, h2, g)


def _combine_kernel(pos_ref, posn_ref, h_ref, gate_ref, fg_ref, ys_ref, o_ref, buf, sem,
                    *, final_norm):
    i = pl.program_id(0)
    n = pl.num_programs(0)
    tm = h_ref.shape[0]
    slot = i % 2

    def start_gather(p_ref, s):
        def row_copy(t, k):
            p = p_ref[0, 0, k * tm + t]
            return pltpu.make_async_copy(
                ys_ref.at[_token_rows(p)], buf.at[s, k, _token_rows(t)], sem.at[s])
        _start_row_copies(tm, row_copy)

    @pl.when(i == 0)
    def _():
        start_gather(pos_ref, slot)

    @pl.when(i + 1 < n)
    def _():
        start_gather(posn_ref, 1 - slot)

    for k in range(TOP_K):
        pltpu.make_async_copy(
            ys_ref.at[pl.ds(0, tm * SUBLANES)], buf.at[slot, k], sem.at[slot]).wait()
    gate = gate_ref[...]
    d = h_ref.shape[1]
    out = (h_ref[...] + gate[:, 0:1] * _load_token_major(buf, (slot, 0), tm, d)
           + gate[:, 1:2] * _load_token_major(buf, (slot, 1), tm, d))
    if final_norm:
        out = _rms_norm_rows(out, fg_ref[...])
    o_ref[...] = out


def _combine(h2, gates, pos, ys, final_g, final_norm):
    t, d = h2.shape
    tm = min(COMBINE_TM, t)
    nt = t // tm
    pos3 = _tile_indices(pos, tm)
    return pl.pallas_call(
        functools.partial(_combine_kernel, final_norm=final_norm),
        grid=(nt,),
        in_specs=[
            pl.BlockSpec((1, 1, TOP_K * tm), lambda i: (i, 0, 0), memory_space=pltpu.SMEM),
            pl.BlockSpec((1, 1, TOP_K * tm), lambda i: (jnp.minimum(i + 1, nt - 1), 0, 0),
                         memory_space=pltpu.SMEM),
            pl.BlockSpec((tm, d), lambda i: (i, 0)),
            pl.BlockSpec((tm, TOP_K), lambda i: (i, 0)),
            pl.BlockSpec((1, d), lambda i: (0, 0)),
            pl.BlockSpec(memory_space=pl.ANY),
        ],
        out_specs=pl.BlockSpec((tm, d), lambda i: (i, 0)),
        out_shape=jax.ShapeDtypeStruct((t, d), F32),
        scratch_shapes=[pltpu.VMEM((2, TOP_K, tm * SUBLANES, LANES), F32),
                        pltpu.SemaphoreType.DMA((2,))],
        compiler_params=pltpu.CompilerParams(
            dimension_semantics=("arbitrary",), vmem_limit_bytes=VMEM_LIMIT),
        name="combine",
    )(pos3, pos3, h2, gates, final_g, ys)


def _moe(h2, g, wcat, w1, w3, w2, j, final_g, final_norm):
    t, d = h2.shape
    eid, gates, rank, counts = _router(h2, g, wcat)
    n_assign = t * TOP_K
    n_blocks = -(-n_assign // MOE_TM) + N_EXPERTS
    cap = n_blocks * MOE_TM
    counts = counts[:, 0]
    padded = (counts + MOE_TM - 1) // MOE_TM * MOE_TM
    pad_end = jnp.cumsum(padded)
    pad_start = pad_end - padded
    experts = jnp.arange(N_EXPERTS, dtype=jnp.int32)[:, None, None]
    dest = rank + jnp.sum(jnp.where(eid[None] == experts, pad_start[:, None, None], 0), axis=0)
    block_start = jnp.arange(n_blocks, dtype=jnp.int32) * MOE_TM
    block_expert = jnp.minimum(
        jnp.sum(block_start[:, None] >= pad_end[None, :], axis=1), N_EXPERTS - 1
    ).astype(jnp.int32)
    valid = jnp.clip(counts[block_expert] - (block_start - pad_start[block_expert]),
                     0, MOE_TM).astype(jnp.int32)
    tail_rows = pad_end[-1] + jnp.arange(N_EXPERTS, dtype=jnp.int32) * MOE_TM
    zrows = jnp.concatenate([
        jnp.where(padded > 0, pad_end - MOE_TM, -1),
        jnp.where(tail_rows < cap, tail_rows, -1),
    ]).astype(jnp.int32)
    xs = _dispatch(h2, g, dest, zrows, cap)
    ys = _expert_ffn(xs, g, block_expert, valid, w1, w3, w2, j)
    return _combine(h2, gates.T, dest, ys, final_g, final_norm)


def _gate_blocks(w_r, w_i):
    nh, hd, _ = w_r.shape
    per = MXU_DIM // hd
    eye = jnp.eye(per, dtype=w_r.dtype)

    def blocks(w):
        wb = w.reshape(nh // per, per, hd, hd)
        return jnp.einsum("cpde,pq->cpdqe", wb, eye).reshape(nh // per, MXU_DIM, MXU_DIM)

    return jnp.concatenate([blocks(w_r), blocks(w_i)], axis=-1)


def _group_matrices(d_mix):
    grp = jnp.arange(d_mix) // HEAD_DIM
    cols = jnp.arange(LANES)
    g1 = jnp.where(grp[:, None] == cols[None, :], 1.0 / HEAD_DIM, 0.0).astype(BF16)
    e = jnp.where(cols[:, None] == grp[None, :], 1.0, 0.0).astype(BF16)
    return g1, jnp.concatenate([e, e], axis=0)


def kernel(x, norm_mix_g, w_in, conv_a_w, conv_a_b, conv_b_w, conv_b_b, lru_w_r, lru_b_r,
           lru_w_i, lru_b_i, lru_lambda, mix_out_g, w_out, norm_ffn_g, ffn_w1, ffn_w3, ffn_w2,
           router_w, expert_w1, expert_w3, expert_w2, final_g):
    b, s, d = x.shape
    depth = w_in.shape[0]
    assert depth % 2 == 0
    g1, e2 = _group_matrices(D_CONV + D_LRU)
    row = lambda v: v.reshape(1, -1)
    h = x
    for layer in range(depth):
        p = {
            "norm_mix_g": row(norm_mix_g[layer]),
            "w_in": w_in[layer].astype(BF16),
            "conv_a_w": conv_a_w[layer], "conv_a_b": row(conv_a_b[layer]),
            "conv_b_w": conv_b_w[layer], "conv_b_b": row(conv_b_b[layer]),
            "w_gate": _gate_blocks(lru_w_r[layer], lru_w_i[layer]).astype(BF16),
            "b_r": row(lru_b_r[layer]), "b_i": row(lru_b_i[layer]), "lam": row(lru_lambda[layer]),
            "g_mix": row(mix_out_g[layer]), "g1": g1, "e2": e2,
            "w_out": w_out[layer].astype(BF16),
        }
        h = _mixer(h, p, layer == 0, layer == depth - 1)
        h2 = h.reshape(b * s, d)
        gf = row(norm_ffn_g[layer])
        j = layer // 2
        if layer % 2 == 0:
            h2 = _dense_ffn(h2, gf, ffn_w1, ffn_w3, ffn_w2, j)
        else:
            w_hi = router_w[j].astype(BF16)
            w_lo = (router_w[j] - w_hi.astype(F32)).astype(BF16)
            h2 = _moe(h2, gf, jnp.concatenate([w_hi.T, w_lo.T], axis=0),
                      expert_w1, expert_w3, expert_w2, j, row(final_g), layer == depth - 1)
        h = h2.reshape(b, s, d)
    return h
```
